```python
import functools
import jax, jax.numpy as jnp
from jax import lax
import numpy as np

D_MODEL = 4096
BATCH = 4
SEQ = 2048
DEPTH = 2
DEC_BATCH = 8
DEC_SEQ = 1
PAST_LEN = 16384
PAGE_SIZE = 128

A_HEADS = 16
A_DK = 128
A_DV = 128
A_KW = A_HEADS * A_DK
A_VW = A_HEADS * A_DV
HGRN_CHUNK = 64
LB_FLOOR = 1e-30
B_HEADS = 16
B_KV_HEADS = 4
B_HEAD_DIM = 128
B_QW = B_HEADS * B_HEAD_DIM
B_KVW = B_KV_HEADS * B_HEAD_DIM
BLOCK = 64
TOP_N = 16
WINDOW = 512
Q_BLK_SEL = 32
Q_BLK_WIN = 128
ATTN_SCALE = B_HEAD_DIM ** -0.5
FORCE_BONUS = 1e4
NEG = -1e30
D_FF = 11008
EPS = 1e-6

_IN_SPLITS = (A_KW, A_KW, A_VW, A_VW,
              B_QW, B_KVW, B_KVW, B_KVW, B_KVW, B_KVW, B_KVW,
              3 * B_HEADS,
              D_MODEL, D_MODEL)
IN_SPLIT_POINTS = tuple(int(s) for s in np.cumsum(_IN_SPLITS)[:-1])
N_IN = int(sum(_IN_SPLITS))

kernel_name = 'hgrn2_nsa_macaron_decode_step'


def rms_norm(x, g):
    xf = x.astype(jnp.float32)
    y = xf * lax.rsqrt(jnp.mean(xf * xf, axis=-1, keepdims=True) + EPS)
    return (y * g.astype(jnp.float32)).astype(x.dtype)


def swiglu(x, w_gu, w_down):
    gate, up = jnp.split(x @ w_gu, 2, axis=-1)
    return (jax.nn.silu(gate) * up) @ w_down


def hgrn_lower_bounds(lb_param):
    p = jax.nn.softmax(lb_param.astype(jnp.float32), axis=0)
    return jnp.cumsum(p, axis=0) - p[0]


def project(u, w_in, lb, qk_g):
    Bn, T, _ = u.shape
    (qa, fa, va, ga, qb, kc, vc, ks, vs, kw, vw, bg, mg_a, mg_b) = jnp.split(u @ w_in, IN_SPLIT_POINTS, axis=-1)

    def heads(a, h):
        return a.reshape(Bn, T, h, -1)

    zf = fa.astype(jnp.float32)
    lbf = lb.astype(jnp.float32)
    log_f = jnp.logaddexp(jnp.log(jnp.maximum(lbf, LB_FLOOR)), jnp.log1p(-lbf) + jax.nn.log_sigmoid(zf))
    ka = (1.0 - lbf) * jax.nn.sigmoid(-zf)
    hgrn = (heads(jax.nn.silu(qa), A_HEADS), heads(log_f, A_HEADS), heads(ka, A_HEADS), heads(va, A_HEADS))
    nsa = (rms_norm(heads(qb, B_HEADS), qk_g[0]),
           heads(kc, B_KV_HEADS), heads(vc, B_KV_HEADS),
           rms_norm(heads(ks, B_KV_HEADS), qk_g[2]), heads(vs, B_KV_HEADS),
           rms_norm(heads(kw, B_KV_HEADS), qk_g[3]), heads(vw, B_KV_HEADS))
    gates = (ga, jax.nn.sigmoid(heads(bg, B_HEADS)), jax.nn.sigmoid(mg_a), jax.nn.sigmoid(mg_b))
    return hgrn, nsa, gates


def hgrn2_recurrence(q, log_f, k, v, s0):
    out_dtype = v.dtype
    q, log_f, k, v = (a.astype(jnp.float32) for a in (q, log_f, k, v))
    Bn, T, H, _ = q.shape
    c = min(HGRN_CHUNK, T)
    n_chunks = -(-T // c)
    pad = n_chunks * c - T

    def prep(a):
        a = jnp.pad(a, ((0, 0), (0, pad), (0, 0), (0, 0)))
        return a.reshape(Bn, n_chunks, c, H, a.shape[-1]).transpose(1, 0, 3, 2, 4)

    qs, fs, ks, vs = prep(q), prep(log_f), prep(k), prep(v)
    causal = jnp.tril(jnp.ones((c, c), bool))

    def step(S, inp):
        qc, fc, kc, vc = inp
        b = jnp.cumsum(fc, axis=2)
        o_inter = jnp.einsum('bhtk,bhkv->bhtv', qc * jnp.exp(b), S)
        diff = b[:, :, :, None, :] - b[:, :, None, :, :]
        decay = jnp.exp(jnp.where(causal[:, :, None], diff, NEG))
        attn = jnp.einsum('bhtk,bhsk,bhtsk->bhts', qc, kc, decay)
        o = o_inter + jnp.einsum('bhts,bhsv->bhtv', attn, vc)
        b_last = b[:, :, -1:, :]
        S_new = jnp.exp(b_last[:, :, 0, :])[..., None] * S + jnp.einsum('bhsk,bhsv->bhkv', kc * jnp.exp(b_last - b), vc)
        return S_new, o

    S, outs = lax.scan(step, s0.astype(jnp.float32), (qs, fs, ks, vs))
    o = outs.transpose(1, 0, 3, 2, 4).reshape(Bn, n_chunks * c, H, -1)[:, :T]
    return o.astype(out_dtype), S


def compress_blocks(k_rows, v_rows, pool_w, g_k):
    Bn, L, G, d = k_rows.shape
    nb = L // BLOCK
    kb = k_rows[:, :nb * BLOCK].reshape(Bn, nb, BLOCK, G, d)
    vb = v_rows[:, :nb * BLOCK].reshape(Bn, nb, BLOCK, G, d)
    kc = rms_norm(jnp.einsum('bnjgd,jg->bngd', kb, pool_w[0]), g_k)
    vc = jnp.einsum('bnjgd,jg->bngd', vb, pool_w[1])
    return kc, vc


def cmp_attend(q, kc, vc, qpos):
    Bn, Tq, H, d = q.shape
    G, nb = kc.shape[2], kc.shape[1]
    qg = q.reshape(Bn, Tq, G, H // G, d)
    s = jnp.einsum('btgrd,bngd->bgrtn', qg, kc).astype(jnp.float32) * ATTN_SCALE
    visible = (jnp.arange(nb)[None, :] + 1) * BLOCK <= qpos[:, None] + 1
    p = jax.nn.softmax(jnp.where(visible, s, NEG), axis=-1) * visible
    o = jnp.einsum('bgrtn,bngd->btgrd', p.astype(vc.dtype), vc).reshape(Bn, Tq, H, d)
    return o, p.sum(axis=2)


def select_blocks(importance, qpos, n_blocks):
    imp = jnp.pad(importance, ((0, 0), (0, 0), (0, 0), (0, n_blocks - importance.shape[-1])))
    cur = qpos // BLOCK
    j = jnp.arange(n_blocks)[None, :]
    forced = (j == 0) | (j == cur[:, None]) | (j == cur[:, None] - 1)
    causal = j <= cur[:, None]
    score = jnp.where(causal, imp + jnp.where(forced, FORCE_BONUS, 0.0), NEG)
    _, idx = lax.top_k(score, min(TOP_N, n_blocks))
    valid = idx <= cur[None, None, :, None]
    return idx, valid


def select_attend(q, idx, valid, qpos, fetch):
    Bn, tq, H, d = q.shape
    G = idx.shape[1]
    pos = idx[..., None] * BLOCK + jnp.arange(BLOCK, dtype=jnp.int32)
    k, v = fetch(pos)
    mask = valid[..., None] & (pos <= qpos[None, None, :, None, None])
    qg = q.reshape(Bn, tq, G, H // G, d)
    s = jnp.einsum('btgrd,bgtnjd->bgrtnj', qg, k).astype(jnp.float32) * ATTN_SCALE
    s = jnp.where(mask[:, :, None], s, NEG)
    p = jax.nn.softmax(s.reshape(s.shape[:4] + (-1,)), axis=-1).reshape(s.shape)
    o = jnp.einsum('bgrtnj,bgtnjd->btgrd', p.astype(v.dtype), v)
    return o.reshape(Bn, tq, H, d)


def make_local_fetch(k_all, v_all):
    bi = jnp.arange(k_all.shape[0])[:, None, None, None, None]
    gi = jnp.arange(k_all.shape[2])[None, :, None, None, None]

    def fetch(pos):
        return k_all[bi, pos, gi], v_all[bi, pos, gi]
    return fetch


def make_paged_fetch(cache_k, cache_v, page_table, k_new, v_new):
    past = page_table.shape[1] * PAGE_SIZE
    tn = k_new.shape[1]
    bi = jnp.arange(page_table.shape[0])[:, None, None, None, None]
    gi = jnp.arange(k_new.shape[2])[None, :, None, None, None]

    def fetch(pos):
        in_past = (pos < past)[..., None]
        pp = jnp.minimum(pos, past - 1)
        phys = page_table[bi, pp // PAGE_SIZE]
        off = pp % PAGE_SIZE
        pn = jnp.clip(pos - past, 0, tn - 1)
        k = jnp.where(in_past, cache_k[phys, off, gi], k_new[bi, pn, gi])
        v = jnp.where(in_past, cache_v[phys, off, gi], v_new[bi, pn, gi])
        return k, v
    return fetch


def gather_pages(cache, page_table):
    rows = cache[page_table]
    return rows.reshape(rows.shape[0], -1, rows.shape[3], rows.shape[4])


def prompt_select(q, idx, valid, fetch):
    Bn, T, H, d = q.shape
    G, n = idx.shape[1], idx.shape[-1]
    qb = min(Q_BLK_SEL, T)
    nq = T // qb
    qs = q.reshape(Bn, nq, qb, H, d).transpose(1, 0, 2, 3, 4)
    ids = idx.reshape(Bn, G, nq, qb, n).transpose(2, 0, 1, 3, 4)
    vals = valid.reshape(Bn, G, nq, qb, n).transpose(2, 0, 1, 3, 4)
    qps = jnp.arange(T, dtype=jnp.int32).reshape(nq, qb)
    out = lax.map(lambda a: select_attend(a[0], a[1], a[2], a[3], fetch), (qs, ids, vals, qps))
    return out.transpose(1, 0, 2, 3, 4).reshape(Bn, T, H, d)


def gqa_attend(qg, k, v, mask):
    s = jnp.einsum('bxqgrd,bxkgd->bxgrqk', qg, k).astype(jnp.float32) * ATTN_SCALE
    p = jax.nn.softmax(jnp.where(mask, s, NEG), axis=-1)
    return jnp.einsum('bxgrqk,bxkgd->bxqgrd', p.astype(v.dtype), v)


def prompt_window(q, k, v):
    Bn, T, H, d = q.shape
    G = k.shape[2]
    qb = min(Q_BLK_WIN, T)
    nq = T // qb
    span = WINDOW + qb
    pad = ((0, 0), (WINDOW, 0), (0, 0), (0, 0))
    kidx = jnp.arange(nq)[:, None] * qb + jnp.arange(span)[None, :]
    kblk = jnp.pad(k, pad)[:, kidx]
    vblk = jnp.pad(v, pad)[:, kidx]
    kpos = kidx - WINDOW
    qpos = jnp.arange(T).reshape(nq, qb)
    diff = qpos[:, :, None] - kpos[:, None, :]
    mask = (diff >= 0) & (diff <= WINDOW) & (kpos[:, None, :] >= 0)
    o = gqa_attend(q.reshape(Bn, nq, qb, G, H // G, d), kblk, vblk, mask[None, :, None, None])
    return o.reshape(Bn, T, H, d)


def prompt_mixers(hgrn, nsa, qk_g, pool_w):
    qa, log_f, ka, va = hgrn
    qb, kc, vc, ks, vs, kw, vw = nsa
    Bn, T = qa.shape[:2]
    s0 = jnp.zeros((Bn, A_HEADS, A_DK, A_DV), jnp.float32)
    o_a, s_new = hgrn2_recurrence(qa, log_f, ka, va, s0)
    qpos = jnp.arange(T, dtype=jnp.int32)
    kcb, vcb = compress_blocks(kc, vc, pool_w, qk_g[1])
    o_cmp, imp = cmp_attend(qb, kcb, vcb, qpos)
    idx, valid = select_blocks(imp, qpos, -(-T // BLOCK))
    o_sel = prompt_select(qb, idx, valid, make_local_fetch(ks, vs))
    o_win = prompt_window(qb, kw, vw)
    wb = min(WINDOW, T)
    return (o_a, o_cmp, o_sel, o_win), (s_new, kc, vc, ks, vs, kw[:, T - wb:], vw[:, T - wb:])


def sample_mixers(hgrn, nsa, qk_g, pool_w, s0, cache_cmp_k, cache_cmp_v, cache_sel_k, cache_sel_v,
                  cache_win_k, cache_win_v, page_table):
    qa, log_f, ka, va = hgrn
    qb, kc, vc, ks, vs, kw, vw = nsa
    Bn, Tn = qa.shape[:2]
    G = kw.shape[2]
    past = page_table.shape[1] * PAGE_SIZE
    o_a, s_new = hgrn2_recurrence(qa, log_f, ka, va, s0)
    qpos = past + jnp.arange(Tn, dtype=jnp.int32)
    kc_rows = jnp.concatenate([gather_pages(cache_cmp_k, page_table), kc], axis=1)
    vc_rows = jnp.concatenate([gather_pages(cache_cmp_v, page_table), vc], axis=1)
    kcb, vcb = compress_blocks(kc_rows, vc_rows, pool_w, qk_g[1])
    o_cmp, imp = cmp_attend(qb, kcb, vcb, qpos)
    idx, valid = select_blocks(imp, qpos, -(-(past + Tn) // BLOCK))
    o_sel = select_attend(qb, idx, valid, qpos, make_paged_fetch(cache_sel_k, cache_sel_v, page_table, ks, vs))
    wb = cache_win_k.shape[1]
    k_win = jnp.concatenate([cache_win_k, kw], axis=1)
    v_win = jnp.concatenate([cache_win_v, vw], axis=1)
    kpos = past - wb + jnp.arange(wb + Tn, dtype=jnp.int32)
    diff = qpos[:, None] - kpos[None, :]
    mask = (diff >= 0) & (diff <= WINDOW)
    o_win = gqa_attend(qb.reshape(Bn, 1, Tn, G, B_HEADS // G, B_HEAD_DIM), k_win[:, None], v_win[:, None], mask)
    o_win = o_win.reshape(Bn, Tn, B_HEADS, B_HEAD_DIM)
    return (o_a, o_cmp, o_sel, o_win), (s_new, kc, vc, ks, vs, k_win[:, -wb:], v_win[:, -wb:])


def merge_branches(o_a, o_cmp, o_sel, o_win, gates, out_g, w_proj_a, w_proj_b, w_out):
    g_a, g_br, m_a, m_b = gates
    Bn, T = g_a.shape[:2]
    y_a = rms_norm(o_a, out_g).reshape(Bn, T, A_VW) * jax.nn.silu(g_a)
    y_b = (g_br[..., 0:1] * o_cmp + g_br[..., 1:2] * o_sel + g_br[..., 2:3] * o_win).reshape(Bn, T, B_QW)
    return (m_a * (y_a @ w_proj_a) + m_b * (y_b @ w_proj_b)) @ w_out


def decoder_layer(x, lw, mixers):
    (norm_g, w_in, lb, qk_g, pool_w, out_g, w_pa, w_pb, w_o, f1_gu, f1_dn, f2_gu, f2_dn) = lw
    h = x + 0.5 * swiglu(rms_norm(x, norm_g[0]), f1_gu, f1_dn)
    hgrn, nsa, gates = project(rms_norm(h, norm_g[1]), w_in, lb, qk_g)
    (o_a, o_cmp, o_sel, o_win), state = mixers(hgrn, nsa, qk_g, pool_w)
    h = h + merge_branches(o_a, o_cmp, o_sel, o_win, gates, out_g, w_pa, w_pb, w_o)
    return h + 0.5 * swiglu(rms_norm(h, norm_g[2]), f2_gu, f2_dn), state


def setup_inputs(seed: int = 0) -> dict:
    key = jax.random.key(seed)
    ks = jax.random.split(key, 32)
    f32 = jnp.float32
    n_pages = PAST_LEN // PAGE_SIZE
    n_used = DEC_BATCH * n_pages
    n_pool = n_used + max(1, n_used // 4)
    wb = min(WINDOW, PAST_LEN)

    def nrm(k, shape, scale):
        return scale * jax.random.normal(k, shape, f32)

    def gain(k, shape):
        return 1.0 + 0.02 * jax.random.normal(k, shape, f32)

    page_table = jax.random.permutation(ks[9], n_pool)[:n_used].reshape(DEC_BATCH, n_pages).astype(jnp.int32)
    paged = (DEPTH, n_pool, PAGE_SIZE, B_KV_HEADS, B_HEAD_DIM)
    win = (DEPTH, DEC_BATCH, wb, B_KV_HEADS, B_HEAD_DIM)
    return {
        'x_prompt': nrm(ks[0], (BATCH, SEQ, D_MODEL), 1.0),
        'x_sample': nrm(ks[1], (DEC_BATCH, DEC_SEQ, D_MODEL), 1.0),
        'state_hgrn': nrm(ks[2], (DEPTH, DEC_BATCH, A_HEADS, A_DK, A_DV), 0.5),
        'cache_cmp_k': nrm(ks[3], paged, 1.0),
        'cache_cmp_v': nrm(ks[4], paged, 1.0),
        'cache_sel_k': nrm(ks[5], paged, 1.0),
        'cache_sel_v': nrm(ks[6], paged, 1.0),
        'cache_win_k': nrm(ks[7], win, 1.0),
        'cache_win_v': nrm(ks[8], win, 1.0),
        'page_table': page_table,
        'norm_g': gain(ks[10], (DEPTH, 3, D_MODEL)),
        'w_in': nrm(ks[11], (DEPTH, D_MODEL, N_IN), D_MODEL ** -0.5),
        'hgrn_lb': nrm(ks[12], (DEPTH, A_KW), 1.0),
        'hgrn_out_g': gain(ks[13], (DEPTH, A_DV)),
        'qk_g': gain(ks[14], (DEPTH, 4, B_HEAD_DIM)),
        'cmp_pool': (1.0 + 0.1 * jax.random.normal(ks[15], (DEPTH, 2, BLOCK, B_KV_HEADS), f32)) / BLOCK,
        'w_proj_a': nrm(ks[16], (DEPTH, A_VW, D_MODEL), A_VW ** -0.5),
        'w_proj_b': nrm(ks[17], (DEPTH, B_QW, D_MODEL), B_QW ** -0.5),
        'w_out': nrm(ks[18], (DEPTH, D_MODEL, D_MODEL), D_MODEL ** -0.5),
        'ffn1_gu': nrm(ks[19], (DEPTH, D_MODEL, 2 * D_FF), D_MODEL ** -0.5),
        'ffn1_down': nrm(ks[20], (DEPTH, D_FF, D_MODEL), D_FF ** -0.5),
        'ffn2_gu': nrm(ks[21], (DEPTH, D_MODEL, 2 * D_FF), D_MODEL ** -0.5),
        'ffn2_down': nrm(ks[22], (DEPTH, D_FF, D_MODEL), D_FF ** -0.5),
    }


def reference(x_prompt, x_sample, state_hgrn, cache_cmp_k, cache_cmp_v, cache_sel_k, cache_sel_v,
              cache_win_k, cache_win_v, page_table, norm_g, w_in, hgrn_lb, hgrn_out_g, qk_g, cmp_pool,
              w_proj_a, w_proj_b, w_out, ffn1_gu, ffn1_down, ffn2_gu, ffn2_down):
    lbs = hgrn_lower_bounds(hgrn_lb)
    y_p, y_s = x_prompt, x_sample
    st_p, st_s = [], []
    for l in range(DEPTH):
        lw = (norm_g[l], w_in[l], lbs[l], qk_g[l], cmp_pool[l], hgrn_out_g[l], w_proj_a[l], w_proj_b[l],
              w_out[l], ffn1_gu[l], ffn1_down[l], ffn2_gu[l], ffn2_down[l])
        y_p, sp = decoder_layer(y_p, lw, prompt_mixers)
        sample_fn = functools.partial(
            sample_mixers, s0=state_hgrn[l], cache_cmp_k=cache_cmp_k[l], cache_cmp_v=cache_cmp_v[l],
            cache_sel_k=cache_sel_k[l], cache_sel_v=cache_sel_v[l], cache_win_k=cache_win_k[l],
            cache_win_v=cache_win_v[l], page_table=page_table)
        y_s, ss = decoder_layer(y_s, lw, sample_fn)
        st_p.append(sp)
        st_s.append(ss)

    def stack(states, i):
        return jnp.stack([s[i] for s in states])

    hgrn_p, cmp_k_p, cmp_v_p = stack(st_p, 0), stack(st_p, 1), stack(st_p, 2)
    sel_k_p, sel_v_p, win_k_p, win_v_p = stack(st_p, 3), stack(st_p, 4), stack(st_p, 5), stack(st_p, 6)
    hgrn_s, cmp_k_s, cmp_v_s = stack(st_s, 0), stack(st_s, 1), stack(st_s, 2)
    sel_k_s, sel_v_s, win_k_s, win_v_s = stack(st_s, 3), stack(st_s, 4), stack(st_s, 5), stack(st_s, 6)
    return (y_p, y_s, hgrn_p, cmp_k_p, cmp_v_p, sel_k_p, sel_v_p, win_k_p, win_v_p,
            hgrn_s, cmp_k_s, cmp_v_s, sel_k_s, sel_v_s, win_k_s, win_v_s)
```

```python
import functools

import jax
import jax.numpy as jnp
from jax import lax
from jax.experimental import pallas as pl
from jax.experimental.pallas import tpu as pltpu

F32 = jnp.float32
BF16 = jnp.bfloat16
I32 = jnp.int32

HEAD = 128
BLOCK = 64
TOP_N = 16
WINDOW = 512
PAGE_SIZE = 128
EPS = 1e-6
NEG = -1e30
LB_FLOOR = 1e-30
FORCE_BONUS = 1e4
ATTN_SCALE = HEAD ** -0.5
HGRN_CHUNK = 128
DIAG = 8
FF_ALIGN = 1024
VMEM_BYTES_V7X = 64 * 1024 * 1024
VMEM_LIMIT = VMEM_BYTES_V7X - 8 * 1024 * 1024


def _params(*sem):
    return pltpu.CompilerParams(dimension_semantics=sem, vmem_limit_bytes=VMEM_LIMIT)


def _tile(dim, candidates):
    for c in candidates:
        if dim % c == 0:
            return c
    return dim


def _dot(a, b):
    return jnp.dot(a, b, preferred_element_type=F32)


def _dot_nt(a, b):
    return lax.dot_general(a, b, (((1,), (1,)), ((), ())), preferred_element_type=F32)


def _dot_tn(a, b):
    return lax.dot_general(a, b, (((0,), (0,)), ((), ())), preferred_element_type=F32)


def _sigmoid(x):
    return jax.nn.sigmoid(x)


def _silu(x):
    return x * jax.nn.sigmoid(x)


def _log_sigmoid(x):
    return jnp.minimum(x, 0.0) - jnp.log1p(jnp.exp(-jnp.abs(x)))


def _logaddexp(a, b):
    return jnp.maximum(a, b) + jnp.log1p(jnp.exp(-jnp.abs(a - b)))


def _rms(x, g):
    ms = jnp.mean(x * x, axis=-1, keepdims=True)
    return x * lax.rsqrt(ms + EPS) * g


def _softmax_masked(s, mask):
    s = jnp.where(mask, s, NEG)
    m = jnp.max(s, axis=-1, keepdims=True)
    e = jnp.exp(s - m)
    return e / jnp.sum(e, axis=-1, keepdims=True)


def _iota(shape, dim):
    return lax.broadcasted_iota(I32, shape, dim)


def _rmsnorm_kernel(x_ref, g_ref, o_ref):
    o_ref[...] = _rms(x_ref[...], g_ref[...]).astype(o_ref.dtype)


def rmsnorm_rows(x, g):
    M, D = x.shape
    tm = _tile(M, (256, 128, 64, 32, 16, 8))
    return pl.pallas_call(
        _rmsnorm_kernel,
        grid=(M // tm,),
        in_specs=[pl.BlockSpec((tm, D), lambda i: (i, 0)), pl.BlockSpec((1, D), lambda i: (0, 0))],
        out_specs=pl.BlockSpec((tm, D), lambda i: (i, 0)),
        out_shape=jax.ShapeDtypeStruct((M, D), BF16),
        compiler_params=_params("parallel"),
        name="rmsnorm_rows",
    )(x, g.reshape(1, D))


def _mm_kernel(x_ref, w_ref, *rest, nk, scale, has_res):
    if has_res:
        r_ref, o_ref = rest[0], rest[1]
        scratch = rest[2:]
    else:
        r_ref, o_ref = None, rest[0]
        scratch = rest[1:]

    def finish(acc):
        if has_res:
            o_ref[...] = (r_ref[...] + scale * acc).astype(o_ref.dtype)
        else:
            o_ref[...] = acc.astype(o_ref.dtype)

    if nk == 1:
        finish(_dot(x_ref[...], w_ref[...]))
    else:
        acc_ref = scratch[0]
        k = pl.program_id(2)

        @pl.when(k == 0)
        def _():
            acc_ref[...] = jnp.zeros_like(acc_ref)

        acc_ref[...] += _dot(x_ref[...], w_ref[...])

        @pl.when(k == nk - 1)
        def _():
            finish(acc_ref[...])


def matmul(x, w, *, out_dtype=F32, res=None, scale=1.0, tm=None, tn=None, tk=None):
    M, K = x.shape
    N = w.shape[1]
    tm = tm or _tile(M, (1024, 512, 256, 128, 64, 32, 16, 8))
    tn = tn or _tile(N, (512, 256, 128))
    tk = tk or (K if K <= 4096 else _tile(K, (1024, 512, 256, 128)))
    nk = K // tk
    has_res = res is not None
    in_specs = [pl.BlockSpec((tm, tk), lambda i, j, k: (i, k)), pl.BlockSpec((tk, tn), lambda i, j, k: (k, j))]
    args = [x, w]
    if has_res:
        in_specs.append(pl.BlockSpec((tm, tn), lambda i, j, k: (i, j)))
        args.append(res)
    return pl.pallas_call(
        functools.partial(_mm_kernel, nk=nk, scale=scale, has_res=has_res),
        grid=(M // tm, N // tn, nk),
        in_specs=in_specs,
        out_specs=pl.BlockSpec((tm, tn), lambda i, j, k: (i, j)),
        out_shape=jax.ShapeDtypeStruct((M, N), out_dtype),
        scratch_shapes=[pltpu.VMEM((tm, tn), F32)] if nk > 1 else [],
        compiler_params=_params("parallel", "parallel", "arbitrary"),
        name="matmul",
    )(*args)


def _swiglu_kernel(x_ref, wg_ref, wu_ref, o_ref):
    x = x_ref[...]
    g = _dot(x, wg_ref[...])
    u = _dot(x, wu_ref[...])
    o_ref[...] = (_silu(g) * u).astype(o_ref.dtype)


def matmul_swiglu(x, w_gu):
    M, K = x.shape
    F = w_gu.shape[1] // 2
    tm = _tile(M, (1024, 512, 256, 128, 64, 32, 16, 8))
    tn = _tile(F, (512, 256, 128))
    nj = F // tn
    return pl.pallas_call(
        _swiglu_kernel,
        grid=(M // tm, nj),
        in_specs=[pl.BlockSpec((tm, K), lambda i, j: (i, 0)),
                  pl.BlockSpec((K, tn), lambda i, j: (0, j)),
                  pl.BlockSpec((K, tn), lambda i, j: (0, j + nj))],
        out_specs=pl.BlockSpec((tm, tn), lambda i, j: (i, j)),
        out_shape=jax.ShapeDtypeStruct((M, F), BF16),
        compiler_params=_params("parallel", "parallel"),
        name="matmul_swiglu",
    )(x, w_gu, w_gu)


def _merge_kernel(ya_ref, yb_ref, wa_ref, wb_ref, ga_ref, gb_ref, o_ref):
    pa = _dot(ya_ref[...], wa_ref[...])
    pb = _dot(yb_ref[...], wb_ref[...])
    o_ref[...] = (_sigmoid(ga_ref[...]) * pa + _sigmoid(gb_ref[...]) * pb).astype(o_ref.dtype)


def merge_project(ya, yb, wa, wb, z, col_a, col_b):
    M, Ka = ya.shape
    Kb = yb.shape[1]
    N = wa.shape[1]
    tm = _tile(M, (1024, 512, 256, 128, 64, 32, 16, 8))
    tn = _tile(N, (512, 256, 128))
    ja, jb = col_a // tn, col_b // tn
    return pl.pallas_call(
        _merge_kernel,
        grid=(M // tm, N // tn),
        in_specs=[pl.BlockSpec((tm, Ka), lambda i, j: (i, 0)),
                  pl.BlockSpec((tm, Kb), lambda i, j: (i, 0)),
                  pl.BlockSpec((Ka, tn), lambda i, j: (0, j)),
                  pl.BlockSpec((Kb, tn), lambda i, j: (0, j)),
                  pl.BlockSpec((tm, tn), lambda i, j: (i, j + ja)),
                  pl.BlockSpec((tm, tn), lambda i, j: (i, j + jb))],
        out_specs=pl.BlockSpec((tm, tn), lambda i, j: (i, j)),
        out_shape=jax.ShapeDtypeStruct((M, N), BF16),
        compiler_params=_params("parallel", "parallel"),
        name="merge_project",
    )(ya, yb, wa, wb, z, z)


def _split3(x):
    hi = x.astype(BF16)
    r1 = x - hi.astype(F32)
    mid = r1.astype(BF16)
    lo = (r1 - mid.astype(F32)).astype(BF16)
    return hi, mid, lo


def _hgrn_gates(zq, zf, lb):
    q = _silu(zq)
    log_f = _logaddexp(jnp.log(jnp.maximum(lb, LB_FLOOR)), jnp.log1p(-lb) + _log_sigmoid(zf))
    k = (1.0 - lb) * _sigmoid(-zf)
    return q, log_f, k


def _hgrn_prompt_kernel(q_ref, f_ref, v_ref, g_ref, lb_ref, og_ref, y_ref, s_ref, st_ref, *, nc):
    C = HGRN_CHUNK
    c = pl.program_id(2)

    @pl.when(c == 0)
    def _():
        st_ref[...] = jnp.zeros_like(st_ref)

    q, log_f, k = _hgrn_gates(q_ref[...], f_ref[...], lb_ref[...])
    v = v_ref[...]
    vb = v.astype(BF16)
    row = _iota((C, C), 0)
    col = _iota((C, C), 1)

    tri = jnp.where(col <= row, 1.0, 0.0).astype(BF16)
    hi, mid, lo = _split3(log_f)
    b = _dot(tri, hi) + _dot(tri, mid) + _dot(tri, lo)

    a = jnp.zeros((C, C), F32)
    h = C // 2
    while h >= DIAG:
        P = 2 * h
        bref = jnp.concatenate(
            [jnp.broadcast_to(b[m * P + h - 1:m * P + h, :], (P, HEAD)) for m in range(C // P)], axis=0)
        second = (row & (P - 1)) >= h
        qt = q * jnp.exp(jnp.where(second, b - bref, NEG))
        kt = k * jnp.exp(jnp.where(second, NEG, bref - b))
        lvl = _dot_nt(qt.astype(BF16), kt.astype(BF16))
        if P < C:
            lvl = jnp.where((row // P) == (col // P), lvl, 0.0)
        a = a + lvl
        h //= 2

    rloc = row & (DIAG - 1)
    for d in range(DIAG):
        if d == 0:
            p = q * k
        else:
            bs = pltpu.roll(b, d, 0)
            ks = pltpu.roll(k, d, 0)
            p = q * jnp.exp(jnp.where(rloc >= d, b - bs, NEG)) * ks
        a = jnp.where((col == row - d) & (rloc >= d), jnp.sum(p, axis=1, keepdims=True), a)

    st = st_ref[...]
    o = _dot_nt((q * jnp.exp(b)).astype(BF16), st.astype(BF16)) + _dot(a.astype(BF16), vb)
    b_last = b[C - 1:C, :]
    kd = k * jnp.exp(b_last - b)
    st_new = st * jnp.exp(b_last) + _dot_tn(vb, kd.astype(BF16))
    st_ref[...] = st_new

    zg = g_ref[...]
    y_ref[...] = (_rms(o, og_ref[...]) * _silu(zg)).astype(y_ref.dtype)

    @pl.when(c == nc - 1)
    def _():
        s_ref[0, 0] = st_new.T


def hgrn_prompt(z, lb, out_g, Bn, T, H):
    C = HGRN_CHUNK
    nc = T // C
    blk = lambda off: pl.BlockSpec((C, HEAD), lambda b, h, c: (b * nc + c, off + h))
    return pl.pallas_call(
        functools.partial(_hgrn_prompt_kernel, nc=nc),
        grid=(Bn, H, nc),
        in_specs=[blk(0), blk(H), blk(2 * H), blk(3 * H),
                  pl.BlockSpec((1, HEAD), lambda b, h, c: (0, h)),
                  pl.BlockSpec((1, HEAD), lambda b, h, c: (0, 0))],
        out_specs=[pl.BlockSpec((C, HEAD), lambda b, h, c: (b * nc + c, h)),
                   pl.BlockSpec((1, 1, HEAD, HEAD), lambda b, h, c: (b, h, 0, 0))],
        out_shape=[jax.ShapeDtypeStruct((Bn * T, H * HEAD), BF16),
                   jax.ShapeDtypeStruct((Bn, H, HEAD, HEAD), F32)],
        scratch_shapes=[pltpu.VMEM((HEAD, HEAD), F32)],
        compiler_params=_params("parallel", "parallel", "arbitrary"),
        name="hgrn_prompt",
    )(z, z, z, z, lb.reshape(1, H * HEAD), out_g.reshape(1, HEAD))


def _column(rowvec, n):
    eye = _iota((n, n), 0) == _iota((n, n), 1)
    return jnp.sum(jnp.where(eye, jnp.broadcast_to(rowvec, (n, n)), 0.0), axis=1, keepdims=True)


def _hgrn_sample_kernel(q_ref, f_ref, v_ref, g_ref, lb_ref, og_ref, s0_ref, y_ref, s_ref):
    q, log_f, k = _hgrn_gates(q_ref[0], f_ref[0], lb_ref[...])
    v = v_ref[0]
    s_new = _column(jnp.exp(log_f), HEAD) * s0_ref[0, 0, 0] + _column(k, HEAD) * v
    o = jnp.sum(_column(q, HEAD) * s_new, axis=0, keepdims=True)
    y_ref[0] = (_rms(o, og_ref[...]) * _silu(g_ref[0])).astype(y_ref.dtype)
    s_ref[0, 0] = s_new


def hgrn_sample(z3, lb, out_g, state, layer, H):
    Bs = z3.shape[0]
    blk = lambda off: pl.BlockSpec((1, 1, HEAD), lambda b, h: (b, 0, off + h))
    return pl.pallas_call(
        _hgrn_sample_kernel,
        grid=(Bs, H),
        in_specs=[blk(0), blk(H), blk(2 * H), blk(3 * H),
                  pl.BlockSpec((1, HEAD), lambda b, h: (0, h)),
                  pl.BlockSpec((1, HEAD), lambda b, h: (0, 0)),
                  pl.BlockSpec((1, 1, 1, HEAD, HEAD), lambda b, h: (layer, b, h, 0, 0))],
        out_specs=[pl.BlockSpec((1, 1, HEAD), lambda b, h: (b, 0, h)),
                   pl.BlockSpec((1, 1, HEAD, HEAD), lambda b, h: (b, h, 0, 0))],
        out_shape=[jax.ShapeDtypeStruct((Bs, 1, H * HEAD), BF16),
                   jax.ShapeDtypeStruct((Bs, H, HEAD, HEAD), F32)],
        compiler_params=_params("parallel", "parallel"),
        name="hgrn_sample",
    )(z3, z3, z3, z3, lb.reshape(1, H * HEAD), out_g.reshape(1, HEAD), state)


def _nsa_prep_kernel(qb_ref, kv_ref, ks_ref, kw_ref, pw_ref, g_ref, qn_ref, ksn_ref, kwn_ref, *pool_refs,
                     HQ, G, pool):
    g = g_ref[...]
    for h in range(HQ):
        sl = slice(h * HEAD, (h + 1) * HEAD)
        qn_ref[:, sl] = _rms(qb_ref[:, sl], g[0:1]).astype(qn_ref.dtype)
    for j in range(G):
        sl = slice(j * HEAD, (j + 1) * HEAD)
        ksn_ref[:, sl] = _rms(ks_ref[:, sl], g[2:3])
        kwn_ref[:, sl] = _rms(kw_ref[:, sl], g[3:4])
    if pool:
        pool_ref = pool_refs[0]
        kvw = G * HEAD
        w = pw_ref[...]
        for n in range(kv_ref.shape[0] // BLOCK):
            blk = jnp.sum(kv_ref[n * BLOCK:(n + 1) * BLOCK, :] * w, axis=0, keepdims=True)
            for j in range(G):
                sl = slice(j * HEAD, (j + 1) * HEAD)
                pool_ref[n:n + 1, sl] = _rms(blk[:, sl], g[1:2])
            pool_ref[n:n + 1, kvw:] = blk[:, kvw:]


def nsa_prep(z, pool_w2, qk_g, HQ, G, col_q, pool):
    M = z.shape[0]
    qw, kvw = HQ * HEAD, G * HEAD
    tm = _tile(M, (512, 256, 128, 64, 8))
    col_kv = col_q + qw
    out_shape = [jax.ShapeDtypeStruct((M, qw), BF16 if pool else F32),
                 jax.ShapeDtypeStruct((M, kvw), F32), jax.ShapeDtypeStruct((M, kvw), F32)]
    out_specs = [pl.BlockSpec((tm, qw), lambda i: (i, 0)),
                 pl.BlockSpec((tm, kvw), lambda i: (i, 0)), pl.BlockSpec((tm, kvw), lambda i: (i, 0))]
    if pool:
        out_shape.append(jax.ShapeDtypeStruct((M // BLOCK, 2 * kvw), F32))
        out_specs.append(pl.BlockSpec((tm // BLOCK, 2 * kvw), lambda i: (i, 0)))
    return pl.pallas_call(
        functools.partial(_nsa_prep_kernel, HQ=HQ, G=G, pool=pool),
        grid=(M // tm,),
        in_specs=[pl.BlockSpec((tm, qw), lambda i: (i, col_q // qw)),
                  pl.BlockSpec((tm, 2 * kvw), lambda i: (i, col_kv // (2 * kvw))),
                  pl.BlockSpec((tm, kvw), lambda i: (i, col_kv // kvw + 2)),
                  pl.BlockSpec((tm, kvw), lambda i: (i, col_kv // kvw + 4)),
                  pl.BlockSpec((BLOCK, 2 * kvw), lambda i: (0, 0)),
                  pl.BlockSpec((4, HEAD), lambda i: (0, 0))],
        out_specs=out_specs,
        out_shape=out_shape,
        compiler_params=_params("parallel"),
        name="nsa_prep",
    )(z, z, z, z, pool_w2, qk_g)


def _gate_col(gates, c):
    return jnp.sum(jnp.where(_iota(gates.shape, 1) == c, gates, 0.0), axis=1, keepdims=True)


def _top_blocks(imp, cur, n_top):
    nb = imp.shape[1]
    j = _iota(imp.shape, 1)
    forced = (j == 0) | (j == cur) | (j == cur - 1)
    causal = j <= cur
    score = jnp.where(causal, imp + jnp.where(forced, FORCE_BONUS, 0.0), NEG)
    rank = jnp.zeros(imp.shape, I32)
    for i in range(nb):
        si = score[:, i:i + 1]
        rank = rank + ((si > score) | ((si == score) & (i < j))).astype(I32)
    return (rank < n_top) & causal


def _nsa_prompt_kernel(q_ref, pool_ref, vpool_ref, ks_ref, vs_ref, kw_ref, vw_ref, bg_ref, o_ref, *, tq, T, R):
    g = pl.program_id(1)
    t0 = pl.program_id(2) * tq
    rows = R * tq
    nb = T // BLOCK
    q = jnp.concatenate([q_ref[:, r * HEAD:(r + 1) * HEAD] for r in range(R)], axis=0)

    kcb = pool_ref[0].astype(BF16)
    vcb = vpool_ref[0].astype(BF16)
    sc = _dot_nt(q, kcb) * ATTN_SCALE
    tpos = t0 + _iota((rows, nb), 0) % tq
    vis = (_iota((rows, nb), 1) + 1) * BLOCK <= tpos + 1
    p = jnp.where(vis, _softmax_masked(sc, vis), 0.0)
    o_cmp = _dot(p.astype(BF16), vcb)
    imp = p[0:tq]
    for r in range(1, R):
        imp = imp + p[r * tq:(r + 1) * tq]

    cur = (t0 + _iota((tq, nb), 0)) // BLOCK
    sel = _top_blocks(imp, cur, min(TOP_N, nb))
    expand = jnp.where(_iota((nb, T), 1) // BLOCK == _iota((nb, T), 0), 1.0, 0.0).astype(BF16)
    selk = _dot(jnp.where(sel, 1.0, 0.0).astype(BF16), expand)
    selk = jnp.concatenate([selk] * R, axis=0)

    tpos = t0 + _iota((rows, T), 0) % tq
    spos = _iota((rows, T), 1)
    ss = _dot_nt(q, ks_ref[...].astype(BF16)) * ATTN_SCALE
    p = _softmax_masked(ss, (selk > 0.5) & (spos <= tpos))
    o_sel = _dot(p.astype(BF16), vs_ref[...].astype(BF16))

    wlen = min(T, WINDOW + tq)
    start = pl.multiple_of(jnp.clip(t0 - WINDOW, 0, T - wlen), HEAD)
    tpos = t0 + _iota((rows, wlen), 0) % tq
    diff = tpos - (start + _iota((rows, wlen), 1))
    sw = _dot_nt(q, kw_ref[pl.ds(start, wlen), :].astype(BF16)) * ATTN_SCALE
    p = _softmax_masked(sw, (diff >= 0) & (diff <= WINDOW))
    o_win = _dot(p.astype(BF16), vw_ref[pl.ds(start, wlen), :].astype(BF16))

    gates = _sigmoid(bg_ref[...])
    for r in range(R):
        c0 = (g * R + r) * 3
        sl = slice(r * tq, (r + 1) * tq)
        y = (_gate_col(gates, c0) * o_cmp[sl] + _gate_col(gates, c0 + 1) * o_sel[sl]
             + _gate_col(gates, c0 + 2) * o_win[sl])
        o_ref[:, r * HEAD:(r + 1) * HEAD] = y.astype(o_ref.dtype)


def nsa_prompt(qn, pooled, ksn, kwn, z, zbg, Bn, T, HQ, G, col_kv):
    R = HQ // G
    tq = _tile(T, (128, 64))
    nq = T // tq
    nb = T // BLOCK
    kvw = G * HEAD
    pooled3 = pooled.reshape(Bn, nb, 2 * kvw)
    cv = col_kv // HEAD
    seq = lambda off: pl.BlockSpec((T, HEAD), lambda b, g, i: (b, off + g))
    return pl.pallas_call(
        functools.partial(_nsa_prompt_kernel, tq=tq, T=T, R=R),
        grid=(Bn, G, nq),
        in_specs=[pl.BlockSpec((tq, R * HEAD), lambda b, g, i: (b * nq + i, g)),
                  pl.BlockSpec((1, nb, HEAD), lambda b, g, i: (b, 0, g)),
                  pl.BlockSpec((1, nb, HEAD), lambda b, g, i: (b, 0, G + g)),
                  seq(0), seq(cv + 3 * G), seq(0), seq(cv + 5 * G),
                  pl.BlockSpec((tq, HEAD), lambda b, g, i: (b * nq + i, 0))],
        out_specs=pl.BlockSpec((tq, R * HEAD), lambda b, g, i: (b * nq + i, g)),
        out_shape=jax.ShapeDtypeStruct((Bn * T, HQ * HEAD), BF16),
        compiler_params=_params("parallel", "parallel", "arbitrary"),
        name="nsa_prompt",
    )(qn, pooled3, pooled3, ksn, z, kwn, z, zbg)


def _pool_pages_kernel(pt_ref, ck_ref, cv_ref, pw_ref, pk_ref, pv_ref):
    kvw = ck_ref.shape[-1]
    w = pw_ref[...]
    for half in range(PAGE_SIZE // BLOCK):
        sl = slice(half * BLOCK, (half + 1) * BLOCK)
        pk_ref[0, 0, half:half + 1, :] = jnp.sum(ck_ref[0, 0, sl, :] * w[:, :kvw], axis=0, keepdims=True)
        pv_ref[0, 0, half:half + 1, :] = jnp.sum(cv_ref[0, 0, sl, :] * w[:, kvw:], axis=0, keepdims=True)


def pool_pages(cache_k, cache_v, page_table, pool_w2, layer):
    Bs, n_pages = page_table.shape
    depth, n_pool = cache_k.shape[:2]
    kvw = cache_k.shape[3] * cache_k.shape[4]
    ck = cache_k.reshape(depth, n_pool, PAGE_SIZE, kvw)
    cv = cache_v.reshape(depth, n_pool, PAGE_SIZE, kvw)
    per = PAGE_SIZE // BLOCK
    page = pl.BlockSpec((1, 1, PAGE_SIZE, kvw), lambda b, p, pt: (layer, pt[b * n_pages + p], 0, 0))
    out = pl.BlockSpec((1, 1, per, kvw), lambda b, p, pt: (b, p, 0, 0))
    pk, pv = pl.pallas_call(
        _pool_pages_kernel,
        grid_spec=pltpu.PrefetchScalarGridSpec(
            num_scalar_prefetch=1,
            grid=(Bs, n_pages),
            in_specs=[page, page, pl.BlockSpec((BLOCK, 2 * kvw), lambda b, p, pt: (0, 0))],
            out_specs=[out, out]),
        out_shape=[jax.ShapeDtypeStruct((Bs, n_pages, per, kvw), F32)] * 2,
        compiler_params=_params("parallel", "parallel"),
        name="pool_pages",
    )(page_table.reshape(-1), ck, cv, pool_w2)
    return pk.reshape(Bs, n_pages * per, kvw), pv.reshape(Bs, n_pages * per, kvw)


def _head_rows(qrow, g, R):
    rows = [qrow[:, (g * R + r) * HEAD:(g * R + r + 1) * HEAD] for r in range(R)]
    rows.append(jnp.zeros((8 - R, HEAD), F32))
    return jnp.concatenate(rows, axis=0).astype(BF16)


def _sample_cmp_win_kernel(q_ref, pk_ref, pv_ref, cwk_ref, cwv_ref, kwn_ref, vw_ref, bg_ref, g_ref,
                           pc_ref, pw_ref, idx_ref, *, G, R, past):
    nbp = pk_ref.shape[1]
    wb = cwk_ref.shape[2]
    qpos = past
    cur = qpos // BLOCK
    nlane = (cur // HEAD + 1) * HEAD
    qrow = q_ref[0]
    gates = _sigmoid(bg_ref[0])
    qk_g = g_ref[...]
    for g in range(G):
        sl = slice(g * HEAD, (g + 1) * HEAD)
        qg = _head_rows(qrow, g, R)

        kcb = _rms(pk_ref[0][:, sl], qk_g[1:2]).astype(BF16)
        sc = _dot_nt(qg, kcb) * ATTN_SCALE
        vis = (_iota((8, nbp), 1) + 1) * BLOCK <= qpos + 1
        p = jnp.where(vis, _softmax_masked(sc, vis), 0.0)
        o_cmp = _dot(p.astype(BF16), pv_ref[0][:, sl].astype(BF16))
        imp = p[0:1]
        for r in range(1, R):
            imp = imp + p[r:r + 1]

        imp = jnp.concatenate([imp, jnp.zeros((1, nlane - nbp), F32)], axis=1)
        j = _iota((1, nlane), 1)
        forced = (j == 0) | (j == cur) | (j == cur - 1)
        score = jnp.where(j <= cur, imp + jnp.where(forced, FORCE_BONUS, 0.0), NEG)
        s_col = _column(score, nlane)
        ii = _iota((nlane, nlane), 0)
        jj = _iota((nlane, nlane), 1)
        ahead = (s_col > score) | ((s_col == score) & (ii < jj))
        rank = jnp.sum(ahead.astype(F32), axis=0, keepdims=True)
        rank_col = _column(rank, nlane)
        hit = rank_col == _iota((nlane, HEAD), 1).astype(F32)
        idx = jnp.sum(jnp.where(hit, _iota((nlane, HEAD), 0).astype(F32), 0.0), axis=0, keepdims=True)
        idx_ref[0, g:g + 1, :] = idx.astype(I32)

        kwc = cwk_ref[0, 0][:, sl].astype(BF16)
        sw = _dot_nt(qg, kwc) * ATTN_SCALE
        diff = qpos - (past - wb + _iota((8, wb), 1))
        sw = jnp.where((diff >= 0) & (diff <= WINDOW), sw, NEG)
        kn = kwn_ref[0][:, sl].astype(BF16).astype(F32)
        vn = vw_ref[0][:, sl].astype(BF16).astype(F32)
        sn = jnp.sum(qg.astype(F32) * kn, axis=1, keepdims=True) * ATTN_SCALE
        m = jnp.maximum(jnp.max(sw, axis=1, keepdims=True), sn)
        e = jnp.exp(sw - m)
        en = jnp.exp(sn - m)
        den = jnp.sum(e, axis=1, keepdims=True) + en
        o_win = (_dot((e / den).astype(BF16), cwv_ref[0, 0][:, sl].astype(BF16))
                 + (en / den).astype(BF16).astype(F32) * vn)

        for r in range(R):
            c0 = (g * R + r) * 3
            hs = slice((g * R + r) * HEAD, (g * R + r + 1) * HEAD)
            pc_ref[0, :, hs] = gates[:, c0:c0 + 1] * o_cmp[r:r + 1]
            pw_ref[0, :, hs] = gates[:, c0 + 2:c0 + 3] * o_win[r:r + 1]


def sample_cmp_win(qn3, pk, pv, cache_win_k, cache_win_v, kwn3, z3, zbg3, qk_g, layer, G, R, past, col_vw):
    Bs = qn3.shape[0]
    depth, _, wb = cache_win_k.shape[:3]
    kvw = G * HEAD
    qw = G * R * HEAD
    nbp = pk.shape[1]
    cwk = cache_win_k.reshape(depth, Bs, wb, kvw)
    cwv = cache_win_v.reshape(depth, Bs, wb, kvw)
    row = lambda w, j=0: pl.BlockSpec((1, 1, w), lambda b: (b, 0, j))
    return pl.pallas_call(
        functools.partial(_sample_cmp_win_kernel, G=G, R=R, past=past),
        grid=(Bs,),
        in_specs=[row(qw),
                  pl.BlockSpec((1, nbp, kvw), lambda b: (b, 0, 0)),
                  pl.BlockSpec((1, nbp, kvw), lambda b: (b, 0, 0)),
                  pl.BlockSpec((1, 1, wb, kvw), lambda b: (layer, b, 0, 0)),
                  pl.BlockSpec((1, 1, wb, kvw), lambda b: (layer, b, 0, 0)),
                  row(kvw), row(kvw, col_vw // kvw), row(HEAD),
                  pl.BlockSpec((4, HEAD), lambda b: (0, 0))],
        out_specs=[row(qw), row(qw), pl.BlockSpec((1, G, HEAD), lambda b: (b, 0, 0))],
        out_shape=[jax.ShapeDtypeStruct((Bs, 1, qw), F32), jax.ShapeDtypeStruct((Bs, 1, qw), F32),
                   jax.ShapeDtypeStruct((Bs, G, HEAD), I32)],
        compiler_params=_params("parallel"),
        name="sample_cmp_win",
    )(qn3, pk, pv, cwk, cwv, kwn3, z3, zbg3, qk_g)


def _sample_select_kernel(idx_ref, pt_ref, q_ref, ck_ref, cv_ref, kn_ref, vn_ref, bg_ref, pc_ref, pw_ref,
                          o_ref, m_ref, l_ref, acc_ref, *, G, R, n_top, past):
    b, g, n = pl.program_id(0), pl.program_id(1), pl.program_id(2)
    qpos = past
    j = idx_ref[(b * G + g) * n_top + n]

    @pl.when(n == 0)
    def _():
        m_ref[...] = jnp.full_like(m_ref, NEG)
        l_ref[...] = jnp.zeros_like(l_ref)
        acc_ref[...] = jnp.zeros_like(acc_ref)

    qg = _head_rows(q_ref[0], 0, R)
    s = _dot_nt(qg, ck_ref[0, 0].astype(BF16)) * ATTN_SCALE
    pos = j * BLOCK + _iota((8, BLOCK), 1)
    s = jnp.where((pos < past) & (pos <= qpos), s, NEG)
    kn = kn_ref[0].astype(BF16).astype(F32)
    vn = vn_ref[0].astype(BF16).astype(F32)
    sn = jnp.sum(qg.astype(F32) * kn, axis=1, keepdims=True) * ATTN_SCALE
    sn = jnp.where(j == past // BLOCK, sn, NEG)

    m_old = m_ref[:, 0:1]
    m_new = jnp.maximum(m_old, jnp.maximum(jnp.max(s, axis=1, keepdims=True), sn))
    alpha = jnp.exp(m_old - m_new)
    e = jnp.exp(s - m_new)
    en = jnp.exp(sn - m_new)
    l_new = alpha * l_ref[:, 0:1] + jnp.sum(e, axis=1, keepdims=True) + en
    acc = (alpha * acc_ref[...] + _dot(e.astype(BF16), cv_ref[0, 0].astype(BF16))
           + en.astype(BF16).astype(F32) * vn)
    m_ref[...] = jnp.broadcast_to(m_new, m_ref.shape)
    l_ref[...] = jnp.broadcast_to(l_new, l_ref.shape)
    acc_ref[...] = acc

    @pl.when(n == n_top - 1)
    def _():
        o_sel = acc / l_new
        gates = _sigmoid(bg_ref[0])
        for r in range(R):
            hs = slice(r * HEAD, (r + 1) * HEAD)
            gs = _gate_col(gates, (g * R + r) * 3 + 1)
            o_ref[0, :, hs] = ((pc_ref[0, :, hs] + gs * o_sel[r:r + 1]) + pw_ref[0, :, hs]).astype(o_ref.dtype)


def sample_select(idx, page_table, qn3, cache_k, cache_v, ksn3, z3, zbg3, pc, pw, layer, G, R, past, col_vs):
    Bs, n_pages = page_table.shape
    n_top = idx.shape[-1]
    depth, n_pool = cache_k.shape[:2]
    kvw = G * HEAD
    per = PAGE_SIZE // BLOCK
    ck = cache_k.reshape(depth, n_pool * per, BLOCK, kvw)
    cv = cache_v.reshape(depth, n_pool * per, BLOCK, kvw)
    nbp = past // BLOCK

    def blk_map(b, g, n, idx_ref, pt_ref):
        jc = jnp.minimum(idx_ref[(b * G + g) * n_top + n], nbp - 1)
        return (layer, pt_ref[b * n_pages + jc // per] * per + jc % per, 0, g)

    grp = lambda w, off=0: pl.BlockSpec((1, 1, w), lambda b, g, n, i_, p_: (b, 0, off + g))
    return pl.pallas_call(
        functools.partial(_sample_select_kernel, G=G, R=R, n_top=n_top, past=past),
        grid_spec=pltpu.PrefetchScalarGridSpec(
            num_scalar_prefetch=2,
            grid=(Bs, G, n_top),
            in_specs=[grp(R * HEAD),
                      pl.BlockSpec((1, 1, BLOCK, HEAD), blk_map), pl.BlockSpec((1, 1, BLOCK, HEAD), blk_map),
                      grp(HEAD), grp(HEAD, col_vs // HEAD),
                      pl.BlockSpec((1, 1, HEAD), lambda b, g, n, i_, p_: (b, 0, 0)),
                      grp(R * HEAD), grp(R * HEAD)],
            out_specs=grp(R * HEAD),
            scratch_shapes=[pltpu.VMEM((8, HEAD), F32), pltpu.VMEM((8, HEAD), F32), pltpu.VMEM((8, HEAD), F32)]),
        out_shape=jax.ShapeDtypeStruct((Bs, 1, G * R * HEAD), BF16),
        compiler_params=_params("parallel", "parallel", "arbitrary"),
        name="sample_select",
    )(idx.reshape(-1), page_table.reshape(-1), qn3, ck, cv, ksn3, z3, zbg3, pc, pw)


def _layer_weights(norm_g, w_in, qk_g, cmp_pool, w_pa, w_pb, w_o, f1_gu, f1_dn, f2_gu, f2_dn, H, HQ, G):
    kw, qw, kvw = H * HEAD, HQ * HEAD, G * HEAD
    D = w_in.shape[0]
    c_bg = 4 * kw + qw + 6 * kvw
    n_bg = 3 * HQ
    w_main = jnp.concatenate([w_in[:, :c_bg], w_in[:, c_bg + n_bg:]], axis=1).astype(BF16)
    w_bg = jnp.pad(w_in[:, c_bg:c_bg + n_bg], ((0, 0), (0, HEAD - n_bg))).astype(BF16)

    def ffn(gu, dn):
        F = dn.shape[0]
        Fp = -(-F // FF_ALIGN) * FF_ALIGN
        gate, up = gu[:, :F], gu[:, F:]
        pad = ((0, 0), (0, Fp - F))
        gu_p = jnp.concatenate([jnp.pad(gate, pad), jnp.pad(up, pad)], axis=1).astype(BF16)
        return gu_p, jnp.pad(dn, ((0, Fp - F), (0, 0))).astype(BF16)

    pool_w2 = jnp.repeat(jnp.transpose(cmp_pool, (1, 0, 2)).reshape(BLOCK, 2 * G), HEAD, axis=1)
    return dict(norm_g=norm_g, w_main=w_main, w_bg=w_bg, qk_g=qk_g, pool_w2=pool_w2,
                w_pa=w_pa.astype(BF16), w_pb=w_pb.astype(BF16), w_o=w_o.astype(BF16),
                ffn1=ffn(f1_gu, f1_dn), ffn2=ffn(f2_gu, f2_dn), c_bg=c_bg, D=D)


def _ffn(x, g, ffn_w):
    gu, dn = ffn_w
    h = matmul_swiglu(rmsnorm_rows(x, g), gu)
    return matmul(h, dn, res=x, scale=0.5)


def _layer(x, lw, lb, out_g, dims, mixers):
    H, HQ, G = dims
    h = _ffn(x, lw["norm_g"][0], lw["ffn1"])
    u = rmsnorm_rows(h, lw["norm_g"][1])
    z = matmul(u, lw["w_main"])
    zbg = matmul(u, lw["w_bg"])
    ya, yb, state = mixers(z, zbg)
    m = merge_project(ya, yb, lw["w_pa"], lw["w_pb"], z, lw["c_bg"], lw["c_bg"] + lw["D"])
    h = matmul(m, lw["w_o"], res=h, scale=1.0)
    return _ffn(h, lw["norm_g"][2], lw["ffn2"]), state


def kernel(x_prompt, x_sample, state_hgrn, cache_cmp_k, cache_cmp_v, cache_sel_k, cache_sel_v, cache_win_k, cache_win_v, page_table, norm_g, w_in, hgrn_lb, hgrn_out_g, qk_g, cmp_pool, w_proj_a, w_proj_b, w_out, ffn1_gu, ffn1_down, ffn2_gu, ffn2_down):
    Bn, T, D = x_prompt.shape
    Bs, Tn, _ = x_sample.shape
    depth = w_in.shape[0]
    H = hgrn_lb.shape[1] // HEAD
    HQ = w_proj_b.shape[1] // HEAD
    G = cache_cmp_k.shape[3]
    R = HQ // G
    kw, qw, kvw = H * HEAD, HQ * HEAD, G * HEAD
    past = page_table.shape[1] * PAGE_SIZE
    wb = cache_win_k.shape[2]
    assert Tn == 1 and T % HGRN_CHUNK == 0 and past % PAGE_SIZE == 0 and R <= 8
    assert hgrn_out_g.shape[1] == HEAD and w_proj_a.shape[1] == kw and wb <= WINDOW and wb <= past
    col_q = 4 * kw
    col_kv = col_q + qw
    dims = (H, HQ, G)

    p = jax.nn.softmax(hgrn_lb.astype(F32), axis=0)
    lbs = jnp.cumsum(p, axis=0) - p[0]

    y_p = x_prompt.reshape(Bn * T, D)
    y_s = x_sample.reshape(Bs, D)
    st_p, st_s = [], []
    for l in range(depth):
        lw = _layer_weights(norm_g[l], w_in[l], qk_g[l], cmp_pool[l], w_proj_a[l], w_proj_b[l], w_out[l],
                            ffn1_gu[l], ffn1_down[l], ffn2_gu[l], ffn2_down[l], H, HQ, G)

        def prompt_mixers(z, zbg):
            ya, s_new = hgrn_prompt(z, lbs[l], hgrn_out_g[l], Bn, T, H)
            qn, ksn, kwn, pooled = nsa_prep(z, lw["pool_w2"], lw["qk_g"], HQ, G, col_q, True)
            yb = nsa_prompt(qn, pooled, ksn, kwn, z, zbg, Bn, T, HQ, G, col_kv)
            kv = lambda i: z[:, col_kv + i * kvw:col_kv + (i + 1) * kvw].reshape(Bn, T, G, HEAD)
            tail = lambda a: a[:, T - min(WINDOW, T):]
            return ya, yb, (s_new, kv(0), kv(1), ksn.reshape(Bn, T, G, HEAD), kv(3),
                            tail(kwn.reshape(Bn, T, G, HEAD)), tail(kv(5)))

        def sample_mixers(z, zbg):
            z3, zbg3 = z.reshape(Bs, 1, -1), zbg.reshape(Bs, 1, HEAD)
            ya, s_new = hgrn_sample(z3, lbs[l], hgrn_out_g[l], state_hgrn, l, H)
            qn, ksn, kwn = nsa_prep(z, lw["pool_w2"], lw["qk_g"], HQ, G, col_q, False)
            qn3, ksn3, kwn3 = qn.reshape(Bs, 1, qw), ksn.reshape(Bs, 1, kvw), kwn.reshape(Bs, 1, kvw)
            pk, pv = pool_pages(cache_cmp_k, cache_cmp_v, page_table, lw["pool_w2"], l)
            pc, pw, idx = sample_cmp_win(qn3, pk, pv, cache_win_k, cache_win_v, kwn3, z3, zbg3, lw["qk_g"],
                                         l, G, R, past, col_kv + 5 * kvw)
            n_top = min(TOP_N, -(-(past + Tn) // BLOCK))
            yb = sample_select(idx[:, :, :n_top], page_table, qn3, cache_sel_k, cache_sel_v, ksn3, z3, zbg3,
                               pc, pw, l, G, R, past, col_kv + 3 * kvw)
            kv = lambda i: z[:, col_kv + i * kvw:col_kv + (i + 1) * kvw].reshape(Bs, 1, G, HEAD)
            win_k = jnp.concatenate([cache_win_k[l], kwn.reshape(Bs, 1, G, HEAD)], axis=1)[:, -wb:]
            win_v = jnp.concatenate([cache_win_v[l], kv(5)], axis=1)[:, -wb:]
            return ya.reshape(Bs, kw), yb.reshape(Bs, qw), (
                s_new, kv(0), kv(1), ksn.reshape(Bs, 1, G, HEAD), kv(3), win_k, win_v)

        y_p, sp = _layer(y_p, lw, lbs[l], hgrn_out_g[l], dims, prompt_mixers)
        y_s, ss = _layer(y_s, lw, lbs[l], hgrn_out_g[l], dims, sample_mixers)
        st_p.append(sp)
        st_s.append(ss)

    stack = lambda states, i: jnp.stack([s[i] for s in states])
    return ((y_p.reshape(Bn, T, D), y_s.reshape(Bs, Tn, D))
            + tuple(stack(st_p, i) for i in range(7)) + tuple(stack(st_s, i) for i in range(7)))
```

```python
import functools

import jax
import jax.numpy as jnp
from jax import lax
from jax.experimental import pallas as pl
from jax.experimental.pallas import tpu as pltpu

F32 = jnp.float32
BF16 = jnp.bfloat16
I32 = jnp.int32

HEAD = 128
BLOCK = 64
TOP_N = 16
WINDOW = 512
PAGE_SIZE = 128
EPS = 1e-6
NEG = -1e30
LB_FLOOR = 1e-30
FORCE_BONUS = 1e4
ATTN_SCALE = HEAD ** -0.5
HGRN_CHUNK = 128
DIAG = 8
CAST_BLOCK_BYTES = 8 * 1024 * 1024
VMEM_BYTES_V7X = 64 * 1024 * 1024
VMEM_LIMIT = VMEM_BYTES_V7X - 8 * 1024 * 1024


def _params(*sem):
    return pltpu.CompilerParams(dimension_semantics=sem, vmem_limit_bytes=VMEM_LIMIT)


def _tile(dim, candidates):
    for c in candidates:
        if dim % c == 0:
            return c
    return dim


def _dot(a, b):
    return jnp.dot(a, b, preferred_element_type=F32)


def _dot_nt(a, b):
    return lax.dot_general(a, b, (((1,), (1,)), ((), ())), preferred_element_type=F32)


def _dot_tn(a, b):
    return lax.dot_general(a, b, (((0,), (0,)), ((), ())), preferred_element_type=F32)


def _sigmoid(x):
    return jax.nn.sigmoid(x)


def _silu(x):
    return x * jax.nn.sigmoid(x)


def _log_sigmoid(x):
    return jnp.minimum(x, 0.0) - jnp.log1p(jnp.exp(-jnp.abs(x)))


def _logaddexp(a, b):
    return jnp.maximum(a, b) + jnp.log1p(jnp.exp(-jnp.abs(a - b)))


def _rms(x, g):
    ms = jnp.mean(x * x, axis=-1, keepdims=True)
    return x * lax.rsqrt(ms + EPS) * g


def _softmax_masked(s, mask):
    s = jnp.where(mask, s, NEG)
    m = jnp.max(s, axis=-1, keepdims=True)
    e = jnp.exp(s - m)
    return e / jnp.sum(e, axis=-1, keepdims=True)


def _iota(shape, dim):
    return lax.broadcasted_iota(I32, shape, dim)


def _rmsnorm_kernel(x_ref, g_ref, o_ref):
    o_ref[...] = _rms(x_ref[...], g_ref[...]).astype(o_ref.dtype)


def rmsnorm_rows(x, g):
    M, D = x.shape
    tm = _tile(M, (256, 128, 64, 32, 16, 8))
    return pl.pallas_call(
        _rmsnorm_kernel,
        grid=(M // tm,),
        in_specs=[pl.BlockSpec((tm, D), lambda i: (i, 0)), pl.BlockSpec((1, D), lambda i: (0, 0))],
        out_specs=pl.BlockSpec((tm, D), lambda i: (i, 0)),
        out_shape=jax.ShapeDtypeStruct((M, D), BF16),
        compiler_params=_params("parallel"),
        name="rmsnorm_rows",
    )(x, g.reshape(1, D))


def _mm_kernel(x_ref, w_ref, *rest, nk, scale, has_res):
    if has_res:
        r_ref, o_ref = rest[0], rest[1]
        scratch = rest[2:]
    else:
        r_ref, o_ref = None, rest[0]
        scratch = rest[1:]

    def finish(acc):
        if has_res:
            o_ref[...] = (r_ref[...] + scale * acc).astype(o_ref.dtype)
        else:
            o_ref[...] = acc.astype(o_ref.dtype)

    if nk == 1:
        finish(_dot(x_ref[...], w_ref[...]))
    else:
        acc_ref = scratch[0]
        k = pl.program_id(2)

        @pl.when(k == 0)
        def _():
            acc_ref[...] = jnp.zeros_like(acc_ref)

        acc_ref[...] += _dot(x_ref[...], w_ref[...])

        @pl.when(k == nk - 1)
        def _():
            finish(acc_ref[...])


def matmul(x, w, *, out_dtype=F32, res=None, scale=1.0, tm=None, tn=None, tk=None):
    M, K = x.shape
    N = w.shape[1]
    tm = tm or _tile(M, (1024, 512, 256, 128, 64, 32, 16, 8))
    if K <= 4096:
        tk = tk or K
        tn = tn or _tile(N, (512, 256, 128) if res is not None else (1024, 512, 256, 128))
    else:
        tk = tk or next((K // d for d in (2, 3, 4, 5, 6, 8) if K % (d * HEAD) == 0), HEAD)
        tn = tn or _tile(N, (512, 256, 128))
    nk = K // tk
    has_res = res is not None
    in_specs = [pl.BlockSpec((tm, tk), lambda i, j, k: (i, k)), pl.BlockSpec((tk, tn), lambda i, j, k: (k, j))]
    args = [x, w]
    if has_res:
        in_specs.append(pl.BlockSpec((tm, tn), lambda i, j, k: (i, j)))
        args.append(res)
    return pl.pallas_call(
        functools.partial(_mm_kernel, nk=nk, scale=scale, has_res=has_res),
        grid=(M // tm, N // tn, nk),
        in_specs=in_specs,
        out_specs=pl.BlockSpec((tm, tn), lambda i, j, k: (i, j)),
        out_shape=jax.ShapeDtypeStruct((M, N), out_dtype),
        scratch_shapes=[pltpu.VMEM((tm, tn), F32)] if nk > 1 else [],
        compiler_params=_params("parallel", "parallel", "arbitrary"),
        name="matmul",
    )(*args)


def _swiglu_kernel(x_ref, wg_ref, wu_ref, o_ref):
    x = x_ref[...]
    g = _dot(x, wg_ref[...])
    u = _dot(x, wu_ref[...])
    o_ref[...] = (_silu(g) * u).astype(o_ref.dtype)


def matmul_swiglu(x, w_gu):
    M, K = x.shape
    F = w_gu.shape[1] // 2
    tn = _tile(F, (512, 256, 128))
    tm = _tile(M, (2048 if tn <= 256 else 1024, 1024, 512, 256, 128, 64, 32, 16, 8))
    nj = F // tn
    return pl.pallas_call(
        _swiglu_kernel,
        grid=(M // tm, nj),
        in_specs=[pl.BlockSpec((tm, K), lambda i, j: (i, 0)),
                  pl.BlockSpec((K, tn), lambda i, j: (0, j)),
                  pl.BlockSpec((K, tn), lambda i, j: (0, j + nj))],
        out_specs=pl.BlockSpec((tm, tn), lambda i, j: (i, j)),
        out_shape=jax.ShapeDtypeStruct((M, F), BF16),
        compiler_params=_params("parallel", "parallel"),
        name="matmul_swiglu",
    )(x, w_gu, w_gu)


def _merge_kernel(ya_ref, yb_ref, wa_ref, wb_ref, ga_ref, gb_ref, o_ref):
    pa = _dot(ya_ref[...], wa_ref[...])
    pb = _dot(yb_ref[...], wb_ref[...])
    o_ref[...] = (_sigmoid(ga_ref[...]) * pa + _sigmoid(gb_ref[...]) * pb).astype(o_ref.dtype)


def merge_project(ya, yb, wa, wb, z, col_a, col_b):
    M, Ka = ya.shape
    Kb = yb.shape[1]
    N = wa.shape[1]
    tm = _tile(M, (1024, 512, 256, 128, 64, 32, 16, 8))
    tn = _tile(N, (512, 256, 128))
    ja, jb = col_a // tn, col_b // tn
    return pl.pallas_call(
        _merge_kernel,
        grid=(M // tm, N // tn),
        in_specs=[pl.BlockSpec((tm, Ka), lambda i, j: (i, 0)),
                  pl.BlockSpec((tm, Kb), lambda i, j: (i, 0)),
                  pl.BlockSpec((Ka, tn), lambda i, j: (0, j)),
                  pl.BlockSpec((Kb, tn), lambda i, j: (0, j)),
                  pl.BlockSpec((tm, tn), lambda i, j: (i, j + ja)),
                  pl.BlockSpec((tm, tn), lambda i, j: (i, j + jb))],
        out_specs=pl.BlockSpec((tm, tn), lambda i, j: (i, j)),
        out_shape=jax.ShapeDtypeStruct((M, N), BF16),
        compiler_params=_params("parallel", "parallel"),
        name="merge_project",
    )(ya, yb, wa, wb, z, z)


def _split3(x):
    hi = x.astype(BF16)
    r1 = x - hi.astype(F32)
    mid = r1.astype(BF16)
    lo = (r1 - mid.astype(F32)).astype(BF16)
    return hi, mid, lo


def _hgrn_gates(zq, zf, lb):
    q = _silu(zq)
    log_f = _logaddexp(jnp.log(jnp.maximum(lb, LB_FLOOR)), jnp.log1p(-lb) + _log_sigmoid(zf))
    k = (1.0 - lb) * _sigmoid(-zf)
    return q, log_f, k


def _hgrn_prompt_kernel(q_ref, f_ref, v_ref, g_ref, lb_ref, og_ref, y_ref, s_ref, st_ref, *, nc):
    C = HGRN_CHUNK
    c = pl.program_id(2)

    @pl.when(c == 0)
    def _():
        st_ref[...] = jnp.zeros_like(st_ref)

    q, log_f, k = _hgrn_gates(q_ref[...], f_ref[...], lb_ref[...])
    v = v_ref[...]
    vb = v.astype(BF16)
    row = _iota((C, C), 0)
    col = _iota((C, C), 1)

    tri = jnp.where(col <= row, 1.0, 0.0).astype(BF16)
    hi, mid, lo = _split3(log_f)
    b = _dot(tri, hi) + _dot(tri, mid) + _dot(tri, lo)

    a = jnp.zeros((C, C), F32)
    h = C // 2
    while h >= DIAG:
        P = 2 * h
        bref = jnp.concatenate(
            [jnp.broadcast_to(b[m * P + h - 1:m * P + h, :], (P, HEAD)) for m in range(C // P)], axis=0)
        second = (row & (P - 1)) >= h
        qt = q * jnp.exp(jnp.where(second, b - bref, NEG))
        kt = k * jnp.exp(jnp.where(second, NEG, bref - b))
        lvl = _dot_nt(qt.astype(BF16), kt.astype(BF16))
        if P < C:
            lvl = jnp.where((row // P) == (col // P), lvl, 0.0)
        a = a + lvl
        h //= 2

    rloc = row & (DIAG - 1)
    for d in range(DIAG):
        if d == 0:
            p = q * k
        else:
            bs = pltpu.roll(b, d, 0)
            ks = pltpu.roll(k, d, 0)
            p = q * jnp.exp(jnp.where(rloc >= d, b - bs, NEG)) * ks
        a = jnp.where((col == row - d) & (rloc >= d), jnp.sum(p, axis=1, keepdims=True), a)

    st = st_ref[...]
    o = _dot_nt((q * jnp.exp(b)).astype(BF16), st.astype(BF16)) + _dot(a.astype(BF16), vb)
    b_last = b[C - 1:C, :]
    kd = k * jnp.exp(b_last - b)
    st_new = st * jnp.exp(b_last) + _dot_tn(vb, kd.astype(BF16))
    st_ref[...] = st_new

    zg = g_ref[...]
    y_ref[...] = (_rms(o, og_ref[...]) * _silu(zg)).astype(y_ref.dtype)

    @pl.when(c == nc - 1)
    def _():
        s_ref[0, 0] = st_new.T


def hgrn_prompt(z, lb, out_g, Bn, T, H):
    C = HGRN_CHUNK
    nc = T // C
    blk = lambda off: pl.BlockSpec((C, HEAD), lambda b, h, c: (b * nc + c, off + h))
    return pl.pallas_call(
        functools.partial(_hgrn_prompt_kernel, nc=nc),
        grid=(Bn, H, nc),
        in_specs=[blk(0), blk(H), blk(2 * H), blk(3 * H),
                  pl.BlockSpec((1, HEAD), lambda b, h, c: (0, h)),
                  pl.BlockSpec((1, HEAD), lambda b, h, c: (0, 0))],
        out_specs=[pl.BlockSpec((C, HEAD), lambda b, h, c: (b * nc + c, h)),
                   pl.BlockSpec((1, 1, HEAD, HEAD), lambda b, h, c: (b, h, 0, 0))],
        out_shape=[jax.ShapeDtypeStruct((Bn * T, H * HEAD), BF16),
                   jax.ShapeDtypeStruct((Bn, H, HEAD, HEAD), F32)],
        scratch_shapes=[pltpu.VMEM((HEAD, HEAD), F32)],
        compiler_params=_params("parallel", "parallel", "arbitrary"),
        name="hgrn_prompt",
    )(z, z, z, z, lb.reshape(1, H * HEAD), out_g.reshape(1, HEAD))


def _column(rowvec, n):
    eye = _iota((n, n), 0) == _iota((n, n), 1)
    return jnp.sum(jnp.where(eye, jnp.broadcast_to(rowvec, (n, n)), 0.0), axis=1, keepdims=True)


def _hgrn_sample_kernel(q_ref, f_ref, v_ref, g_ref, lb_ref, og_ref, s0_ref, y_ref, s_ref):
    q, log_f, k = _hgrn_gates(q_ref[0], f_ref[0], lb_ref[...])
    v = v_ref[0]
    s_new = _column(jnp.exp(log_f), HEAD) * s0_ref[0, 0, 0] + _column(k, HEAD) * v
    o = jnp.sum(_column(q, HEAD) * s_new, axis=0, keepdims=True)
    y_ref[0] = (_rms(o, og_ref[...]) * _silu(g_ref[0])).astype(y_ref.dtype)
    s_ref[0, 0] = s_new


def hgrn_sample(z3, lb, out_g, state, layer, H):
    Bs = z3.shape[0]
    blk = lambda off: pl.BlockSpec((1, 1, HEAD), lambda b, h: (b, 0, off + h))
    return pl.pallas_call(
        _hgrn_sample_kernel,
        grid=(Bs, H),
        in_specs=[blk(0), blk(H), blk(2 * H), blk(3 * H),
                  pl.BlockSpec((1, HEAD), lambda b, h: (0, h)),
                  pl.BlockSpec((1, HEAD), lambda b, h: (0, 0)),
                  pl.BlockSpec((1, 1, 1, HEAD, HEAD), lambda b, h: (layer, b, h, 0, 0))],
        out_specs=[pl.BlockSpec((1, 1, HEAD), lambda b, h: (b, 0, h)),
                   pl.BlockSpec((1, 1, HEAD, HEAD), lambda b, h: (b, h, 0, 0))],
        out_shape=[jax.ShapeDtypeStruct((Bs, 1, H * HEAD), BF16),
                   jax.ShapeDtypeStruct((Bs, H, HEAD, HEAD), F32)],
        compiler_params=_params("parallel", "parallel"),
        name="hgrn_sample",
    )(z3, z3, z3, z3, lb.reshape(1, H * HEAD), out_g.reshape(1, HEAD), state)


def _nsa_prep_kernel(qb_ref, kv_ref, ks_ref, kw_ref, pw_ref, g_ref, qn_ref, ksn_ref, kwn_ref, *pool_refs,
                     HQ, G, pool):
    g = g_ref[...]
    for h in range(HQ):
        sl = slice(h * HEAD, (h + 1) * HEAD)
        qn_ref[:, sl] = (_rms(qb_ref[:, sl], g[0:1]) * ATTN_SCALE).astype(qn_ref.dtype)
    for j in range(G):
        sl = slice(j * HEAD, (j + 1) * HEAD)
        ksn_ref[:, sl] = _rms(ks_ref[:, sl], g[2:3])
        kwn_ref[:, sl] = _rms(kw_ref[:, sl], g[3:4])
    if pool:
        pool_ref = pool_refs[0]
        kvw = G * HEAD
        w = pw_ref[...]
        for n in range(kv_ref.shape[0] // BLOCK):
            blk = jnp.sum(kv_ref[n * BLOCK:(n + 1) * BLOCK, :] * w, axis=0, keepdims=True)
            for j in range(G):
                sl = slice(j * HEAD, (j + 1) * HEAD)
                pool_ref[n:n + 1, sl] = _rms(blk[:, sl], g[1:2])
            pool_ref[n:n + 1, kvw:] = blk[:, kvw:]


def nsa_prep(z, pool_w2, qk_g, HQ, G, col_q, pool):
    M = z.shape[0]
    qw, kvw = HQ * HEAD, G * HEAD
    tm = _tile(M, (512, 256, 128, 64, 8))
    col_kv = col_q + qw
    out_shape = [jax.ShapeDtypeStruct((M, qw), BF16 if pool else F32),
                 jax.ShapeDtypeStruct((M, kvw), F32), jax.ShapeDtypeStruct((M, kvw), F32)]
    out_specs = [pl.BlockSpec((tm, qw), lambda i: (i, 0)),
                 pl.BlockSpec((tm, kvw), lambda i: (i, 0)), pl.BlockSpec((tm, kvw), lambda i: (i, 0))]
    if pool:
        out_shape.append(jax.ShapeDtypeStruct((M // BLOCK, 2 * kvw), F32))
        out_specs.append(pl.BlockSpec((tm // BLOCK, 2 * kvw), lambda i: (i, 0)))
    return pl.pallas_call(
        functools.partial(_nsa_prep_kernel, HQ=HQ, G=G, pool=pool),
        grid=(M // tm,),
        in_specs=[pl.BlockSpec((tm, qw), lambda i: (i, col_q // qw)),
                  pl.BlockSpec((tm, 2 * kvw), lambda i: (i, col_kv // (2 * kvw))),
                  pl.BlockSpec((tm, kvw), lambda i: (i, col_kv // kvw + 2)),
                  pl.BlockSpec((tm, kvw), lambda i: (i, col_kv // kvw + 4)),
                  pl.BlockSpec((BLOCK, 2 * kvw), lambda i: (0, 0)),
                  pl.BlockSpec((4, HEAD), lambda i: (0, 0))],
        out_specs=out_specs,
        out_shape=out_shape,
        compiler_params=_params("parallel"),
        name="nsa_prep",
    )(z, z, z, z, pool_w2, qk_g)


def _gate_col(gates, c):
    return jnp.sum(jnp.where(_iota(gates.shape, 1) == c, gates, 0.0), axis=1, keepdims=True)


def _top_blocks(imp, cur, n_top):
    nb = imp.shape[1]
    j = _iota(imp.shape, 1)
    forced = (j == 0) | (j == cur) | (j == cur - 1)
    causal = j <= cur
    score = jnp.where(causal, imp + jnp.where(forced, FORCE_BONUS, 0.0), NEG)
    rank = jnp.zeros(imp.shape, I32)
    for i in range(nb):
        si = score[:, i:i + 1]
        rank = rank + ((si > score) | ((si == score) & (i < j))).astype(I32)
    return (rank < n_top) & causal


def _nsa_prompt_kernel(q_ref, pool_ref, vpool_ref, ks_ref, vs_ref, kw_ref, vw_ref, bg_ref, o_ref,
                       bias_ref, osel_ref, *, tq, T, R, kc):
    g = pl.program_id(1)
    t0 = pl.program_id(2) * tq
    rows = R * tq
    nb = T // BLOCK
    q = jnp.concatenate([q_ref[:, r * HEAD:(r + 1) * HEAD] for r in range(R)], axis=0)

    kcb = pool_ref[0].astype(BF16)
    vcb = vpool_ref[0].astype(BF16)
    sc = _dot_nt(q, kcb)
    tpos = t0 + _iota((rows, nb), 0) % tq
    vis = (_iota((rows, nb), 1) + 1) * BLOCK <= tpos + 1
    p = jnp.where(vis, _softmax_masked(sc, vis), 0.0)
    o_cmp = _dot(p.astype(BF16), vcb)
    imp = p[0:tq]
    for r in range(1, R):
        imp = imp + p[r * tq:(r + 1) * tq]

    cur = (t0 + _iota((tq, nb), 0)) // BLOCK
    sel = jnp.where(_top_blocks(imp, cur, min(TOP_N, nb)), 1.0, 0.0).astype(BF16)

    def attend(qr, k, v, bias):
        s = _dot_nt(qr, k) + bias
        e = jnp.exp(s - jnp.max(s, axis=1, keepdims=True))
        return _dot(e.astype(BF16), v) / jnp.sum(e, axis=1, keepdims=True)

    for c in range(T // kc):
        @pl.when((t0 + tq - 1) // kc == c)
        def _():
            W = (c + 1) * kc
            expand = jnp.where(_iota((nb, W), 1) // BLOCK == _iota((nb, W), 0), 1.0, 0.0).astype(BF16)
            allowed = (_dot(sel, expand) > 0.5) & (_iota((tq, W), 1) <= t0 + _iota((tq, W), 0))
            bias_ref[:, :W] = jnp.where(allowed, 0.0, NEG)
            k = ks_ref[:W, :].astype(BF16)
            v = vs_ref[:W, :].astype(BF16)
            for r in range(R):
                osel_ref[r * tq:(r + 1) * tq, :] = attend(q[r * tq:(r + 1) * tq], k, v, bias_ref[:, :W])

    wlen = min(T, WINDOW + tq)
    start = pl.multiple_of(jnp.clip(t0 - WINDOW, 0, T - wlen), HEAD)
    diff = (t0 + _iota((tq, wlen), 0)) - (start + _iota((tq, wlen), 1))
    wbias = jnp.where((diff >= 0) & (diff <= WINDOW), 0.0, NEG)
    kw = kw_ref[pl.ds(start, wlen), :].astype(BF16)
    vw = vw_ref[pl.ds(start, wlen), :].astype(BF16)

    gates = _sigmoid(bg_ref[...])
    for r in range(R):
        c0 = (g * R + r) * 3
        sl = slice(r * tq, (r + 1) * tq)
        o_win = attend(q[sl], kw, vw, wbias)
        y = (_gate_col(gates, c0) * o_cmp[sl] + _gate_col(gates, c0 + 1) * osel_ref[sl, :]
             + _gate_col(gates, c0 + 2) * o_win)
        o_ref[:, r * HEAD:(r + 1) * HEAD] = y.astype(o_ref.dtype)


def nsa_prompt(qn, pooled, ksn, kwn, z, zbg, Bn, T, HQ, G, col_kv):
    R = HQ // G
    tq = _tile(T, (128, 64))
    nq = T // tq
    nb = T // BLOCK
    kvw = G * HEAD
    pooled3 = pooled.reshape(Bn, nb, 2 * kvw)
    cv = col_kv // HEAD
    kc = _tile(T, (512, 256, 128, 64))
    seq = lambda off: pl.BlockSpec((T, HEAD), lambda b, g, i: (b, off + g))
    return pl.pallas_call(
        functools.partial(_nsa_prompt_kernel, tq=tq, T=T, R=R, kc=kc),
        grid=(Bn, G, nq),
        in_specs=[pl.BlockSpec((tq, R * HEAD), lambda b, g, i: (b * nq + i, g)),
                  pl.BlockSpec((1, nb, HEAD), lambda b, g, i: (b, 0, g)),
                  pl.BlockSpec((1, nb, HEAD), lambda b, g, i: (b, 0, G + g)),
                  seq(0), seq(cv + 3 * G), seq(0), seq(cv + 5 * G),
                  pl.BlockSpec((tq, HEAD), lambda b, g, i: (b * nq + i, 0))],
        out_specs=pl.BlockSpec((tq, R * HEAD), lambda b, g, i: (b * nq + i, g)),
        out_shape=jax.ShapeDtypeStruct((Bn * T, HQ * HEAD), BF16),
        scratch_shapes=[pltpu.VMEM((tq, T), F32), pltpu.VMEM((R * tq, HEAD), F32)],
        compiler_params=_params("parallel", "parallel", "arbitrary"),
        name="nsa_prompt",
    )(qn, pooled3, pooled3, ksn, z, kwn, z, zbg)


def _pool_pages_kernel(pt_ref, *refs, pp):
    ck_refs, cv_refs = refs[:pp], refs[pp:2 * pp]
    pw_ref, pk_ref, pv_ref = refs[2 * pp:]
    per = PAGE_SIZE // BLOCK
    for q in range(pp):
        for half in range(per):
            sl = slice(half * BLOCK, (half + 1) * BLOCK)
            pk_ref[0, q * per + half] = jnp.sum(ck_refs[q][0, 0, sl] * pw_ref[0], axis=0)
            pv_ref[0, q * per + half] = jnp.sum(cv_refs[q][0, 0, sl] * pw_ref[1], axis=0)


def pool_pages(cache_k, cache_v, page_table, pool_w3, layer):
    Bs, n_pages = page_table.shape
    G = cache_k.shape[3]
    per = PAGE_SIZE // BLOCK
    pp = _tile(n_pages, (8, 4, 2, 1))
    page = lambda q: pl.BlockSpec((1, 1, PAGE_SIZE, G, HEAD),
                                  lambda b, p, pt: (layer, pt[b * n_pages + p * pp + q], 0, 0, 0))
    out = pl.BlockSpec((1, pp * per, G, HEAD), lambda b, p, pt: (b, p, 0, 0))
    return pl.pallas_call(
        functools.partial(_pool_pages_kernel, pp=pp),
        grid_spec=pltpu.PrefetchScalarGridSpec(
            num_scalar_prefetch=1,
            grid=(Bs, n_pages // pp),
            in_specs=[page(q) for q in range(pp)] * 2
                     + [pl.BlockSpec((2, BLOCK, G, HEAD), lambda b, p, pt: (0, 0, 0, 0))],
            out_specs=[out, out]),
        out_shape=[jax.ShapeDtypeStruct((Bs, n_pages * per, G, HEAD), F32)] * 2,
        compiler_params=_params("parallel", "parallel"),
        name="pool_pages",
    )(page_table.reshape(-1), *([cache_k] * pp), *([cache_v] * pp), pool_w3)


def _head_rows(qrow, g, R):
    rows = [qrow[:, (g * R + r) * HEAD:(g * R + r + 1) * HEAD] for r in range(R)]
    rows.append(jnp.zeros((8 - R, HEAD), F32))
    return jnp.concatenate(rows, axis=0).astype(BF16)


def _sample_cmp_win_kernel(q_ref, pk_ref, pv_ref, cwk_ref, cwv_ref, kwn_ref, vw_ref, bg_ref, g_ref,
                           pc_ref, pw_ref, idx_ref, *, G, R, past):
    nbp = pk_ref.shape[1]
    wb = cwk_ref.shape[2]
    qpos = past
    cur = qpos // BLOCK
    nlane = (cur // HEAD + 1) * HEAD
    qrow = q_ref[0]
    gates = _sigmoid(bg_ref[0])
    qk_g = g_ref[...]
    for g in range(G):
        sl = slice(g * HEAD, (g + 1) * HEAD)
        qg = _head_rows(qrow, g, R)

        kcb = _rms(pk_ref[0, :, g, :], qk_g[1:2]).astype(BF16)
        sc = _dot_nt(qg, kcb)
        vis = (_iota((8, nbp), 1) + 1) * BLOCK <= qpos + 1
        p = jnp.where(vis, _softmax_masked(sc, vis), 0.0)
        o_cmp = _dot(p.astype(BF16), pv_ref[0, :, g, :].astype(BF16))
        imp = p[0:1]
        for r in range(1, R):
            imp = imp + p[r:r + 1]

        imp = jnp.concatenate([imp, jnp.zeros((1, nlane - nbp), F32)], axis=1)
        j = _iota((1, nlane), 1)
        forced = (j == 0) | (j == cur) | (j == cur - 1)
        score = jnp.where(j <= cur, imp + jnp.where(forced, FORCE_BONUS, 0.0), NEG)
        s_col = _column(score, nlane)
        ii = _iota((nlane, nlane), 0)
        jj = _iota((nlane, nlane), 1)
        ahead = (s_col > score) | ((s_col == score) & (ii < jj))
        rank = jnp.sum(ahead.astype(F32), axis=0, keepdims=True)
        rank_col = _column(rank, nlane)
        hit = rank_col == _iota((nlane, HEAD), 1).astype(F32)
        idx = jnp.sum(jnp.where(hit, _iota((nlane, HEAD), 0).astype(F32), 0.0), axis=0, keepdims=True)
        idx_ref[0, g:g + 1, :] = idx.astype(I32)

        kwc = cwk_ref[0, 0, :, g, :].astype(BF16)
        sw = _dot_nt(qg, kwc)
        diff = qpos - (past - wb + _iota((8, wb), 1))
        sw = jnp.where((diff >= 0) & (diff <= WINDOW), sw, NEG)
        kn = kwn_ref[0][:, sl].astype(BF16).astype(F32)
        vn = vw_ref[0][:, sl].astype(BF16).astype(F32)
        sn = jnp.sum(qg.astype(F32) * kn, axis=1, keepdims=True)
        m = jnp.maximum(jnp.max(sw, axis=1, keepdims=True), sn)
        e = jnp.exp(sw - m)
        en = jnp.exp(sn - m)
        den = jnp.sum(e, axis=1, keepdims=True) + en
        o_win = (_dot((e / den).astype(BF16), cwv_ref[0, 0, :, g, :].astype(BF16))
                 + (en / den).astype(BF16).astype(F32) * vn)

        for r in range(R):
            c0 = (g * R + r) * 3
            hs = slice((g * R + r) * HEAD, (g * R + r + 1) * HEAD)
            pc_ref[0, :, hs] = gates[:, c0:c0 + 1] * o_cmp[r:r + 1]
            pw_ref[0, :, hs] = gates[:, c0 + 2:c0 + 3] * o_win[r:r + 1]


def sample_cmp_win(qn3, pk, pv, cache_win_k, cache_win_v, kwn3, z3, zbg3, qk_g, layer, G, R, past, col_vw):
    Bs = qn3.shape[0]
    depth, _, wb = cache_win_k.shape[:3]
    kvw = G * HEAD
    qw = G * R * HEAD
    nbp = pk.shape[1]
    row = lambda w, j=0: pl.BlockSpec((1, 1, w), lambda b: (b, 0, j))
    return pl.pallas_call(
        functools.partial(_sample_cmp_win_kernel, G=G, R=R, past=past),
        grid=(Bs,),
        in_specs=[row(qw),
                  pl.BlockSpec((1, nbp, G, HEAD), lambda b: (b, 0, 0, 0)),
                  pl.BlockSpec((1, nbp, G, HEAD), lambda b: (b, 0, 0, 0)),
                  pl.BlockSpec((1, 1, wb, G, HEAD), lambda b: (layer, b, 0, 0, 0)),
                  pl.BlockSpec((1, 1, wb, G, HEAD), lambda b: (layer, b, 0, 0, 0)),
                  row(kvw), row(kvw, col_vw // kvw), row(HEAD),
                  pl.BlockSpec((4, HEAD), lambda b: (0, 0))],
        out_specs=[row(qw), row(qw), pl.BlockSpec((1, G, HEAD), lambda b: (b, 0, 0))],
        out_shape=[jax.ShapeDtypeStruct((Bs, 1, qw), F32), jax.ShapeDtypeStruct((Bs, 1, qw), F32),
                   jax.ShapeDtypeStruct((Bs, G, HEAD), I32)],
        compiler_params=_params("parallel"),
        name="sample_cmp_win",
    )(qn3, pk, pv, cache_win_k, cache_win_v, kwn3, z3, zbg3, qk_g)


def _sample_select_kernel(idx_ref, pt_ref, q_ref, *refs, G, R, n_top, past):
    ck_refs, cv_refs = refs[:G], refs[G:2 * G]
    kn_ref, vn_ref, bg_ref, pc_ref, pw_ref, o_ref, m_ref, l_ref, acc_ref = refs[2 * G:]
    b, n = pl.program_id(0), pl.program_id(1)
    qpos = past

    @pl.when(n == 0)
    def _():
        m_ref[...] = jnp.full_like(m_ref, NEG)
        l_ref[...] = jnp.zeros_like(l_ref)
        acc_ref[...] = jnp.zeros_like(acc_ref)

    qrow = q_ref[0]
    for g in range(G):
        sl = slice(g * HEAD, (g + 1) * HEAD)
        j = idx_ref[(b * G + g) * n_top + n]
        qg = _head_rows(qrow, g, R)
        s = _dot_nt(qg, ck_refs[g][0, 0, :, g, :].astype(BF16))
        pos = j * BLOCK + _iota((8, BLOCK), 1)
        s = jnp.where((pos < past) & (pos <= qpos), s, NEG)
        kn = kn_ref[0][:, sl].astype(BF16).astype(F32)
        vn = vn_ref[0][:, sl].astype(BF16).astype(F32)
        sn = jnp.sum(qg.astype(F32) * kn, axis=1, keepdims=True)
        sn = jnp.where(j == past // BLOCK, sn, NEG)

        m_old = m_ref[g][:, 0:1]
        m_new = jnp.maximum(m_old, jnp.maximum(jnp.max(s, axis=1, keepdims=True), sn))
        alpha = jnp.exp(m_old - m_new)
        e = jnp.exp(s - m_new)
        en = jnp.exp(sn - m_new)
        l_new = alpha * l_ref[g][:, 0:1] + jnp.sum(e, axis=1, keepdims=True) + en
        acc = (alpha * acc_ref[g] + _dot(e.astype(BF16), cv_refs[g][0, 0, :, g, :].astype(BF16))
               + en.astype(BF16).astype(F32) * vn)
        m_ref[g] = jnp.broadcast_to(m_new, (8, HEAD))
        l_ref[g] = jnp.broadcast_to(l_new, (8, HEAD))
        acc_ref[g] = acc

    @pl.when(n == n_top - 1)
    def _():
        gates = _sigmoid(bg_ref[0])
        for g in range(G):
            o_sel = acc_ref[g] / l_ref[g][:, 0:1]
            for r in range(R):
                c = (g * R + r) * 3 + 1
                hs = slice((g * R + r) * HEAD, (g * R + r + 1) * HEAD)
                o_ref[0, :, hs] = ((pc_ref[0, :, hs] + gates[:, c:c + 1] * o_sel[r:r + 1])
                                   + pw_ref[0, :, hs]).astype(o_ref.dtype)


def sample_select(idx, page_table, qn3, cache_k, cache_v, ksn3, z3, zbg3, pc, pw, layer, G, R, past, col_vs):
    Bs, n_pages = page_table.shape
    n_top = idx.shape[-1]
    depth, n_pool = cache_k.shape[:2]
    kvw, qw = G * HEAD, G * R * HEAD
    per = PAGE_SIZE // BLOCK
    ck = cache_k.reshape(depth, n_pool * per, BLOCK, G, HEAD)
    cv = cache_v.reshape(depth, n_pool * per, BLOCK, G, HEAD)
    nbp = past // BLOCK

    def blk(g):
        def index(b, n, idx_ref, pt_ref):
            jc = jnp.minimum(idx_ref[(b * G + g) * n_top + n], nbp - 1)
            return (layer, pt_ref[b * n_pages + jc // per] * per + jc % per, 0, 0, 0)
        return pl.BlockSpec((1, 1, BLOCK, G, HEAD), index)

    row = lambda w, j=0: pl.BlockSpec((1, 1, w), lambda b, n, i_, p_: (b, 0, j))
    return pl.pallas_call(
        functools.partial(_sample_select_kernel, G=G, R=R, n_top=n_top, past=past),
        grid_spec=pltpu.PrefetchScalarGridSpec(
            num_scalar_prefetch=2,
            grid=(Bs, n_top),
            in_specs=[row(qw)] + [blk(g) for g in range(G)] * 2
                     + [row(kvw), row(kvw, col_vs // kvw), row(HEAD), row(qw), row(qw)],
            out_specs=row(qw),
            scratch_shapes=[pltpu.VMEM((G, 8, HEAD), F32)] * 3),
        out_shape=jax.ShapeDtypeStruct((Bs, 1, qw), BF16),
        compiler_params=_params("parallel", "arbitrary"),
        name="sample_select",
    )(idx.reshape(-1), page_table.reshape(-1), qn3, *([ck] * G), *([cv] * G), ksn3, z3, zbg3, pc, pw)


def _cast_kernel(x_ref, o_ref):
    o_ref[...] = x_ref[0].astype(o_ref.dtype)


def cast_layer_bf16(w, layer):
    _, R, C = w.shape
    tr = next((t for t in (512, 256, 128, 64, 32, 16) if R % t == 0 and t * C * 4 <= CAST_BLOCK_BYTES), R)
    return pl.pallas_call(
        _cast_kernel,
        grid=(R // tr,),
        in_specs=[pl.BlockSpec((1, tr, C), lambda i: (layer, i, 0))],
        out_specs=pl.BlockSpec((tr, C), lambda i: (i, 0)),
        out_shape=jax.ShapeDtypeStruct((R, C), BF16),
        compiler_params=_params("parallel"),
        name="cast_layer_bf16",
    )(w)


def _layer_weights(l, norm_g, w_in, qk_g, cmp_pool, w_pa, w_pb, w_o, f1_gu, f1_dn, f2_gu, f2_dn, H, HQ, G):
    kw, qw, kvw = H * HEAD, HQ * HEAD, G * HEAD
    D = w_in.shape[1]
    c_bg = 4 * kw + qw + 6 * kvw
    n_bg = 3 * HQ
    w_main = jnp.concatenate([w_in[l, :, :c_bg], w_in[l, :, c_bg + n_bg:]], axis=1).astype(BF16)
    w_bg = jnp.pad(w_in[l, :, c_bg:c_bg + n_bg], ((0, 0), (0, HEAD - n_bg))).astype(BF16)
    pool_w2 = jnp.repeat(jnp.transpose(cmp_pool[l], (1, 0, 2)).reshape(BLOCK, 2 * G), HEAD, axis=1)
    pool_w3 = jnp.broadcast_to(cmp_pool[l][..., None], cmp_pool.shape[1:] + (HEAD,))
    cast = lambda w: cast_layer_bf16(w, l)
    return dict(norm_g=norm_g[l], w_main=w_main, w_bg=w_bg, qk_g=qk_g[l], pool_w2=pool_w2, pool_w3=pool_w3,
                w_pa=cast(w_pa), w_pb=cast(w_pb), w_o=cast(w_o),
                ffn1=(cast(f1_gu), cast(f1_dn)), ffn2=(cast(f2_gu), cast(f2_dn)), c_bg=c_bg, D=D)


def _ffn(x, g, ffn_w):
    gu, dn = ffn_w
    h = matmul_swiglu(rmsnorm_rows(x, g), gu)
    return matmul(h, dn, res=x, scale=0.5)


def _layer(x, lw, lb, out_g, dims, mixers):
    H, HQ, G = dims
    h = _ffn(x, lw["norm_g"][0], lw["ffn1"])
    u = rmsnorm_rows(h, lw["norm_g"][1])
    z = matmul(u, lw["w_main"])
    zbg = matmul(u, lw["w_bg"])
    ya, yb, state = mixers(z, zbg)
    m = merge_project(ya, yb, lw["w_pa"], lw["w_pb"], z, lw["c_bg"], lw["c_bg"] + lw["D"])
    h = matmul(m, lw["w_o"], res=h, scale=1.0)
    return _ffn(h, lw["norm_g"][2], lw["ffn2"]), state


def kernel(x_prompt, x_sample, state_hgrn, cache_cmp_k, cache_cmp_v, cache_sel_k, cache_sel_v, cache_win_k, cache_win_v, page_table, norm_g, w_in, hgrn_lb, hgrn_out_g, qk_g, cmp_pool, w_proj_a, w_proj_b, w_out, ffn1_gu, ffn1_down, ffn2_gu, ffn2_down):
    Bn, T, D = x_prompt.shape
    Bs, Tn, _ = x_sample.shape
    depth = w_in.shape[0]
    H = hgrn_lb.shape[1] // HEAD
    HQ = w_proj_b.shape[1] // HEAD
    G = cache_cmp_k.shape[3]
    R = HQ // G
    kw, qw, kvw = H * HEAD, HQ * HEAD, G * HEAD
    past = page_table.shape[1] * PAGE_SIZE
    wb = cache_win_k.shape[2]
    assert Tn == 1 and T % HGRN_CHUNK == 0 and past % PAGE_SIZE == 0 and R <= 8
    assert hgrn_out_g.shape[1] == HEAD and w_proj_a.shape[1] == kw and wb <= WINDOW and wb <= past
    col_q = 4 * kw
    col_kv = col_q + qw
    dims = (H, HQ, G)

    p = jax.nn.softmax(hgrn_lb.astype(F32), axis=0)
    lbs = jnp.cumsum(p, axis=0) - p[0]

    y_p = x_prompt.reshape(Bn * T, D)
    y_s = x_sample.reshape(Bs, D)
    st_p, st_s = [], []
    for l in range(depth):
        lw = _layer_weights(l, norm_g, w_in, qk_g, cmp_pool, w_proj_a, w_proj_b, w_out,
                            ffn1_gu, ffn1_down, ffn2_gu, ffn2_down, H, HQ, G)

        def prompt_mixers(z, zbg):
            ya, s_new = hgrn_prompt(z, lbs[l], hgrn_out_g[l], Bn, T, H)
            qn, ksn, kwn, pooled = nsa_prep(z, lw["pool_w2"], lw["qk_g"], HQ, G, col_q, True)
            yb = nsa_prompt(qn, pooled, ksn, kwn, z, zbg, Bn, T, HQ, G, col_kv)
            kv = lambda i: z[:, col_kv + i * kvw:col_kv + (i + 1) * kvw].reshape(Bn, T, G, HEAD)
            tail = lambda a: a[:, T - min(WINDOW, T):]
            return ya, yb, (s_new, kv(0), kv(1), ksn.reshape(Bn, T, G, HEAD), kv(3),
                            tail(kwn.reshape(Bn, T, G, HEAD)), tail(kv(5)))

        def sample_mixers(z, zbg):
            z3, zbg3 = z.reshape(Bs, 1, -1), zbg.reshape(Bs, 1, HEAD)
            ya, s_new = hgrn_sample(z3, lbs[l], hgrn_out_g[l], state_hgrn, l, H)
            qn, ksn, kwn = nsa_prep(z, lw["pool_w2"], lw["qk_g"], HQ, G, col_q, False)
            qn3, ksn3, kwn3 = qn.reshape(Bs, 1, qw), ksn.reshape(Bs, 1, kvw), kwn.reshape(Bs, 1, kvw)
            pk, pv = pool_pages(cache_cmp_k, cache_cmp_v, page_table, lw["pool_w3"], l)
            pc, pw, idx = sample_cmp_win(qn3, pk, pv, cache_win_k, cache_win_v, kwn3, z3, zbg3, lw["qk_g"],
                                         l, G, R, past, col_kv + 5 * kvw)
            n_top = min(TOP_N, -(-(past + Tn) // BLOCK))
            yb = sample_select(idx[:, :, :n_top], page_table, qn3, cache_sel_k, cache_sel_v, ksn3, z3, zbg3,
                               pc, pw, l, G, R, past, col_kv + 3 * kvw)
            kv = lambda i: z[:, col_kv + i * kvw:col_kv + (i + 1) * kvw].reshape(Bs, 1, G, HEAD)
            win_k = jnp.concatenate([cache_win_k[l], kwn.reshape(Bs, 1, G, HEAD)], axis=1)[:, -wb:]
            win_v = jnp.concatenate([cache_win_v[l], kv(5)], axis=1)[:, -wb:]
            return ya.reshape(Bs, kw), yb.reshape(Bs, qw), (
                s_new, kv(0), kv(1), ksn.reshape(Bs, 1, G, HEAD), kv(3), win_k, win_v)

        y_p, sp = _layer(y_p, lw, lbs[l], hgrn_out_g[l], dims, prompt_mixers)
        y_s, ss = _layer(y_s, lw, lbs[l], hgrn_out_g[l], dims, sample_mixers)
        st_p.append(sp)
        st_s.append(ss)

    stack = lambda states, i: jnp.stack([s[i] for s in states])
    return ((y_p.reshape(Bn, T, D), y_s.reshape(Bs, Tn, D))
            + tuple(stack(st_p, i) for i in range(7)) + tuple(stack(st_s, i) for i in range(7)))
```

```python
import functools

import jax
import jax.numpy as jnp
from jax import lax
from jax.experimental import pallas as pl
from jax.experimental.pallas import tpu as pltpu

F32 = jnp.float32
BF16 = jnp.bfloat16
I32 = jnp.int32

HEAD = 128
BLOCK = 64
TOP_N = 16
WINDOW = 512
PAGE_SIZE = 128
EPS = 1e-6
NEG = -1e30
LB_FLOOR = 1e-30
FORCE_BONUS = 1e4
ATTN_SCALE = HEAD ** -0.5
HGRN_CHUNK = 128
DIAG = 4
STRIP = 16
CAST_BLOCK_BYTES = 8 * 1024 * 1024
VMEM_BYTES_V7X = 64 * 1024 * 1024
VMEM_LIMIT = VMEM_BYTES_V7X - 8 * 1024 * 1024


def _params(*sem):
    return pltpu.CompilerParams(dimension_semantics=sem, vmem_limit_bytes=VMEM_LIMIT)


def _tile(dim, candidates):
    for c in candidates:
        if dim % c == 0:
            return c
    return dim


def _dot(a, b):
    return jnp.dot(a, b, preferred_element_type=F32)


def _dot_nt(a, b):
    return lax.dot_general(a, b, (((1,), (1,)), ((), ())), preferred_element_type=F32)


def _dot_tn(a, b):
    return lax.dot_general(a, b, (((0,), (0,)), ((), ())), preferred_element_type=F32)


def _sigmoid(x):
    return jax.nn.sigmoid(x)


def _silu(x):
    return x * jax.nn.sigmoid(x)


def _log_sigmoid(x):
    return jnp.minimum(x, 0.0) - jnp.log1p(jnp.exp(-jnp.abs(x)))


def _logaddexp(a, b):
    return jnp.maximum(a, b) + jnp.log1p(jnp.exp(-jnp.abs(a - b)))


def _rms(x, g):
    ms = jnp.mean(x * x, axis=-1, keepdims=True)
    return x * lax.rsqrt(ms + EPS) * g


def _softmax_masked(s, mask):
    s = jnp.where(mask, s, NEG)
    m = jnp.max(s, axis=-1, keepdims=True)
    e = jnp.exp(s - m)
    return e / jnp.sum(e, axis=-1, keepdims=True)


def _iota(shape, dim):
    return lax.broadcasted_iota(I32, shape, dim)


def _rmsnorm_kernel(x_ref, g_ref, o_ref):
    o_ref[...] = _rms(x_ref[...], g_ref[...]).astype(o_ref.dtype)


def rmsnorm_rows(x, g):
    M, D = x.shape
    tm = _tile(M, (256, 128, 64, 32, 16, 8))
    return pl.pallas_call(
        _rmsnorm_kernel,
        grid=(M // tm,),
        in_specs=[pl.BlockSpec((tm, D), lambda i: (i, 0)), pl.BlockSpec((1, D), lambda i: (0, 0))],
        out_specs=pl.BlockSpec((tm, D), lambda i: (i, 0)),
        out_shape=jax.ShapeDtypeStruct((M, D), BF16),
        compiler_params=_params("parallel"),
        name="rmsnorm_rows",
    )(x, g.reshape(1, D))


def _mm_kernel(x_ref, w_ref, *rest, nk, scale, has_res):
    if has_res:
        r_ref, o_ref = rest[0], rest[1]
        scratch = rest[2:]
    else:
        r_ref, o_ref = None, rest[0]
        scratch = rest[1:]

    def finish(acc):
        if has_res:
            o_ref[...] = (r_ref[...] + scale * acc).astype(o_ref.dtype)
        else:
            o_ref[...] = acc.astype(o_ref.dtype)

    if nk == 1:
        finish(_dot(x_ref[...], w_ref[...]))
    else:
        acc_ref = scratch[0]
        k = pl.program_id(2)

        @pl.when(k == 0)
        def _():
            acc_ref[...] = jnp.zeros_like(acc_ref)

        acc_ref[...] += _dot(x_ref[...], w_ref[...])

        @pl.when(k == nk - 1)
        def _():
            finish(acc_ref[...])


def matmul(x, w, *, out_dtype=F32, res=None, scale=1.0, tm=None, tn=None, tk=None):
    M, K = x.shape
    N = w.shape[1]
    tm = tm or _tile(M, (1024, 512, 256, 128, 64, 32, 16, 8))
    if K <= 4096:
        tk = tk or K
        tn = tn or _tile(N, (512, 256, 128) if res is not None else (1024, 512, 256, 128))
    else:
        tk = tk or next((K // d for d in (2, 3, 4, 5, 6, 8) if K % (d * HEAD) == 0), HEAD)
        tn = tn or _tile(N, (512, 256, 128))
    nk = K // tk
    has_res = res is not None
    in_specs = [pl.BlockSpec((tm, tk), lambda i, j, k: (i, k)), pl.BlockSpec((tk, tn), lambda i, j, k: (k, j))]
    args = [x, w]
    if has_res:
        in_specs.append(pl.BlockSpec((tm, tn), lambda i, j, k: (i, j)))
        args.append(res)
    return pl.pallas_call(
        functools.partial(_mm_kernel, nk=nk, scale=scale, has_res=has_res),
        grid=(M // tm, N // tn, nk),
        in_specs=in_specs,
        out_specs=pl.BlockSpec((tm, tn), lambda i, j, k: (i, j)),
        out_shape=jax.ShapeDtypeStruct((M, N), out_dtype),
        scratch_shapes=[pltpu.VMEM((tm, tn), F32)] if nk > 1 else [],
        compiler_params=_params("parallel", "parallel", "arbitrary"),
        name="matmul",
    )(*args)


def _swiglu_kernel(x_ref, wg_ref, wu_ref, o_ref):
    x = x_ref[...]
    g = _dot(x, wg_ref[...])
    u = _dot(x, wu_ref[...])
    o_ref[...] = (_silu(g) * u).astype(o_ref.dtype)


def matmul_swiglu(x, w_gu):
    M, K = x.shape
    F = w_gu.shape[1] // 2
    tn = _tile(F, (512, 256, 128))
    tm = _tile(M, (2048 if tn <= 256 else 1024, 1024, 512, 256, 128, 64, 32, 16, 8))
    nj = F // tn
    return pl.pallas_call(
        _swiglu_kernel,
        grid=(M // tm, nj),
        in_specs=[pl.BlockSpec((tm, K), lambda i, j: (i, 0)),
                  pl.BlockSpec((K, tn), lambda i, j: (0, j)),
                  pl.BlockSpec((K, tn), lambda i, j: (0, j + nj))],
        out_specs=pl.BlockSpec((tm, tn), lambda i, j: (i, j)),
        out_shape=jax.ShapeDtypeStruct((M, F), BF16),
        compiler_params=_params("parallel", "parallel"),
        name="matmul_swiglu",
    )(x, w_gu, w_gu)


def _merge_kernel(ya_ref, yb_ref, wa_ref, wb_ref, ga_ref, gb_ref, o_ref):
    pa = _dot(ya_ref[...], wa_ref[...])
    pb = _dot(yb_ref[...], wb_ref[...])
    o_ref[...] = (_sigmoid(ga_ref[...]) * pa + _sigmoid(gb_ref[...]) * pb).astype(o_ref.dtype)


def merge_project(ya, yb, wa, wb, z, col_a, col_b):
    M, Ka = ya.shape
    Kb = yb.shape[1]
    N = wa.shape[1]
    tm = _tile(M, (1024, 512, 256, 128, 64, 32, 16, 8))
    tn = _tile(N, (512, 256, 128))
    ja, jb = col_a // tn, col_b // tn
    return pl.pallas_call(
        _merge_kernel,
        grid=(M // tm, N // tn),
        in_specs=[pl.BlockSpec((tm, Ka), lambda i, j: (i, 0)),
                  pl.BlockSpec((tm, Kb), lambda i, j: (i, 0)),
                  pl.BlockSpec((Ka, tn), lambda i, j: (0, j)),
                  pl.BlockSpec((Kb, tn), lambda i, j: (0, j)),
                  pl.BlockSpec((tm, tn), lambda i, j: (i, j + ja)),
                  pl.BlockSpec((tm, tn), lambda i, j: (i, j + jb))],
        out_specs=pl.BlockSpec((tm, tn), lambda i, j: (i, j)),
        out_shape=jax.ShapeDtypeStruct((M, N), BF16),
        compiler_params=_params("parallel", "parallel"),
        name="merge_project",
    )(ya, yb, wa, wb, z, z)


def _split3(x):
    hi = x.astype(BF16)
    r1 = x - hi.astype(F32)
    mid = r1.astype(BF16)
    lo = (r1 - mid.astype(F32)).astype(BF16)
    return hi, mid, lo


def _hgrn_gates(zq, zf, lb):
    q = _silu(zq)
    log_f = _logaddexp(jnp.log(jnp.maximum(lb, LB_FLOOR)), jnp.log1p(-lb) + _log_sigmoid(zf))
    k = (1.0 - lb) * _sigmoid(-zf)
    return q, log_f, k


def _hgrn_chunk(zq, zf, v, zg, lb, og, st):
    C = HGRN_CHUNK
    q, log_f, k = _hgrn_gates(zq, zf, lb)
    vb = v.astype(BF16)
    row = _iota((C, C), 0)
    col = _iota((C, C), 1)

    tri = jnp.where(col <= row, 1.0, 0.0).astype(BF16)
    hi, mid, lo = _split3(log_f)
    b = _dot(tri, hi) + _dot(tri, mid) + _dot(tri, lo)

    a = jnp.zeros((C, C), F32)
    h = C // 2
    while h >= DIAG:
        P = 2 * h
        bref = jnp.concatenate(
            [jnp.broadcast_to(b[m * P + h - 1:m * P + h, :], (P, HEAD)) for m in range(C // P)], axis=0)
        second = (row & (P - 1)) >= h
        qt = q * jnp.exp(jnp.where(second, b - bref, NEG))
        kt = k * jnp.exp(jnp.where(second, NEG, bref - b))
        lvl = _dot_nt(qt.astype(BF16), kt.astype(BF16))
        if P < C:
            lvl = jnp.where((row // P) == (col // P), lvl, 0.0)
        a = a + lvl
        h //= 2

    rloc = row & (DIAG - 1)
    for d in range(DIAG):
        if d == 0:
            p = q * k
        else:
            bs = pltpu.roll(b, d, 0)
            ks = pltpu.roll(k, d, 0)
            p = q * jnp.exp(jnp.where(rloc >= d, b - bs, NEG)) * ks
        a = jnp.where((col == row - d) & (rloc >= d), jnp.sum(p, axis=1, keepdims=True), a)

    o = _dot_nt((q * jnp.exp(b)).astype(BF16), st.astype(BF16)) + _dot(a.astype(BF16), vb)
    b_last = b[C - 1:C, :]
    kd = k * jnp.exp(b_last - b)
    st_new = st * jnp.exp(b_last) + _dot_tn(vb, kd.astype(BF16))
    return _rms(o, og) * _silu(zg), st_new


def _hgrn_prompt_kernel(q_ref, f_ref, v_ref, g_ref, lb_ref, og_ref, y_ref, s_ref, st_ref, *, nc, hb):
    c = pl.program_id(2)

    @pl.when(c == 0)
    def _():
        st_ref[...] = jnp.zeros_like(st_ref)

    for i in range(hb):
        sl = slice(i * HEAD, (i + 1) * HEAD)
        y, st_new = _hgrn_chunk(q_ref[:, sl], f_ref[:, sl], v_ref[:, sl], g_ref[:, sl], lb_ref[:, sl],
                                og_ref[...], st_ref[i])
        st_ref[i] = st_new
        y_ref[:, sl] = y.astype(y_ref.dtype)

        @pl.when(c == nc - 1)
        def _():
            s_ref[0, i] = st_new.T


def hgrn_prompt(z, lb, out_g, Bn, T, H):
    C = HGRN_CHUNK
    nc = T // C
    hb = _tile(H, (4, 2, 1))
    w = hb * HEAD
    blk = lambda off: pl.BlockSpec((C, w), lambda b, h, c: (b * nc + c, off + h))
    return pl.pallas_call(
        functools.partial(_hgrn_prompt_kernel, nc=nc, hb=hb),
        grid=(Bn, H // hb, nc),
        in_specs=[blk(0), blk(H // hb), blk(2 * H // hb), blk(3 * H // hb),
                  pl.BlockSpec((1, w), lambda b, h, c: (0, h)),
                  pl.BlockSpec((1, HEAD), lambda b, h, c: (0, 0))],
        out_specs=[pl.BlockSpec((C, w), lambda b, h, c: (b * nc + c, h)),
                   pl.BlockSpec((1, hb, HEAD, HEAD), lambda b, h, c: (b, h, 0, 0))],
        out_shape=[jax.ShapeDtypeStruct((Bn * T, H * HEAD), BF16),
                   jax.ShapeDtypeStruct((Bn, H, HEAD, HEAD), F32)],
        scratch_shapes=[pltpu.VMEM((hb, HEAD, HEAD), F32)],
        compiler_params=_params("parallel", "parallel", "arbitrary"),
        name="hgrn_prompt",
    )(z, z, z, z, lb.reshape(1, H * HEAD), out_g.reshape(1, HEAD))


def _column(rowvec, n):
    eye = _iota((n, n), 0) == _iota((n, n), 1)
    return jnp.sum(jnp.where(eye, jnp.broadcast_to(rowvec, (n, n)), 0.0), axis=1, keepdims=True)


def _hgrn_sample_kernel(q_ref, f_ref, v_ref, g_ref, lb_ref, og_ref, s0_ref, y_ref, s_ref):
    q, log_f, k = _hgrn_gates(q_ref[0], f_ref[0], lb_ref[...])
    v = v_ref[0]
    s_new = _column(jnp.exp(log_f), HEAD) * s0_ref[0, 0, 0] + _column(k, HEAD) * v
    o = jnp.sum(_column(q, HEAD) * s_new, axis=0, keepdims=True)
    y_ref[0] = (_rms(o, og_ref[...]) * _silu(g_ref[0])).astype(y_ref.dtype)
    s_ref[0, 0] = s_new


def hgrn_sample(z3, lb, out_g, state, layer, H):
    Bs = z3.shape[0]
    blk = lambda off: pl.BlockSpec((1, 1, HEAD), lambda b, h: (b, 0, off + h))
    return pl.pallas_call(
        _hgrn_sample_kernel,
        grid=(Bs, H),
        in_specs=[blk(0), blk(H), blk(2 * H), blk(3 * H),
                  pl.BlockSpec((1, HEAD), lambda b, h: (0, h)),
                  pl.BlockSpec((1, HEAD), lambda b, h: (0, 0)),
                  pl.BlockSpec((1, 1, 1, HEAD, HEAD), lambda b, h: (layer, b, h, 0, 0))],
        out_specs=[pl.BlockSpec((1, 1, HEAD), lambda b, h: (b, 0, h)),
                   pl.BlockSpec((1, 1, HEAD, HEAD), lambda b, h: (b, h, 0, 0))],
        out_shape=[jax.ShapeDtypeStruct((Bs, 1, H * HEAD), BF16),
                   jax.ShapeDtypeStruct((Bs, H, HEAD, HEAD), F32)],
        compiler_params=_params("parallel", "parallel"),
        name="hgrn_sample",
    )(z3, z3, z3, z3, lb.reshape(1, H * HEAD), out_g.reshape(1, HEAD), state)


def _nsa_prep_kernel(qb_ref, kv_ref, ks_ref, kw_ref, pw_ref, g_ref, qn_ref, ksn_ref, kwn_ref, *pool_refs,
                     HQ, G, pool):
    g = g_ref[...]
    for h in range(HQ):
        sl = slice(h * HEAD, (h + 1) * HEAD)
        qn_ref[:, sl] = (_rms(qb_ref[:, sl], g[0:1]) * ATTN_SCALE).astype(qn_ref.dtype)
    for j in range(G):
        sl = slice(j * HEAD, (j + 1) * HEAD)
        ksn_ref[:, sl] = _rms(ks_ref[:, sl], g[2:3])
        kwn_ref[:, sl] = _rms(kw_ref[:, sl], g[3:4])
    if pool:
        pool_ref = pool_refs[0]
        kvw = G * HEAD
        w = pw_ref[...]
        for n in range(kv_ref.shape[0] // BLOCK):
            blk = jnp.sum(kv_ref[n * BLOCK:(n + 1) * BLOCK, :] * w, axis=0, keepdims=True)
            for j in range(G):
                sl = slice(j * HEAD, (j + 1) * HEAD)
                pool_ref[n:n + 1, sl] = _rms(blk[:, sl], g[1:2])
            pool_ref[n:n + 1, kvw:] = blk[:, kvw:]


def nsa_prep(z, pool_w2, qk_g, HQ, G, col_q, pool):
    M = z.shape[0]
    qw, kvw = HQ * HEAD, G * HEAD
    tm = _tile(M, (512, 256, 128, 64, 8))
    col_kv = col_q + qw
    out_shape = [jax.ShapeDtypeStruct((M, qw), BF16 if pool else F32),
                 jax.ShapeDtypeStruct((M, kvw), F32), jax.ShapeDtypeStruct((M, kvw), F32)]
    out_specs = [pl.BlockSpec((tm, qw), lambda i: (i, 0)),
                 pl.BlockSpec((tm, kvw), lambda i: (i, 0)), pl.BlockSpec((tm, kvw), lambda i: (i, 0))]
    if pool:
        out_shape.append(jax.ShapeDtypeStruct((M // BLOCK, 2 * kvw), F32))
        out_specs.append(pl.BlockSpec((tm // BLOCK, 2 * kvw), lambda i: (i, 0)))
    return pl.pallas_call(
        functools.partial(_nsa_prep_kernel, HQ=HQ, G=G, pool=pool),
        grid=(M // tm,),
        in_specs=[pl.BlockSpec((tm, qw), lambda i: (i, col_q // qw)),
                  pl.BlockSpec((tm, 2 * kvw), lambda i: (i, col_kv // (2 * kvw))),
                  pl.BlockSpec((tm, kvw), lambda i: (i, col_kv // kvw + 2)),
                  pl.BlockSpec((tm, kvw), lambda i: (i, col_kv // kvw + 4)),
                  pl.BlockSpec((BLOCK, 2 * kvw), lambda i: (0, 0)),
                  pl.BlockSpec((4, HEAD), lambda i: (0, 0))],
        out_specs=out_specs,
        out_shape=out_shape,
        compiler_params=_params("parallel"),
        name="nsa_prep",
    )(z, z, z, z, pool_w2, qk_g)


def _gate_col(gates, c):
    return jnp.sum(jnp.where(_iota(gates.shape, 1) == c, gates, 0.0), axis=1, keepdims=True)


def _top_blocks(imp, cur, n_top):
    nb = imp.shape[1]
    j = _iota(imp.shape, 1)
    forced = (j == 0) | (j == cur) | (j == cur - 1)
    causal = j <= cur
    score = jnp.where(causal, imp + jnp.where(forced, FORCE_BONUS, 0.0), NEG)
    rank = jnp.zeros(imp.shape, I32)
    for i in range(nb):
        si = score[:, i:i + 1]
        rank = rank + ((si > score) | ((si == score) & (i < j))).astype(I32)
    return (rank < n_top) & causal


def _nsa_prompt_kernel(q_ref, pool_ref, vpool_ref, ks_ref, vs_ref, kw_ref, vw_ref, bg_ref, o_ref,
                       bias_ref, s_ref, e_ref, l_ref, osel_ref, *, tq, T, R, kc):
    g = pl.program_id(1)
    t0 = pl.program_id(2) * tq
    rows = R * tq
    nb = T // BLOCK
    q = jnp.concatenate([q_ref[:, r * HEAD:(r + 1) * HEAD] for r in range(R)], axis=0)

    kcb = pool_ref[0].astype(BF16)
    vcb = vpool_ref[0].astype(BF16)
    sc = _dot_nt(q, kcb)
    tpos = t0 + _iota((rows, nb), 0) % tq
    vis = (_iota((rows, nb), 1) + 1) * BLOCK <= tpos + 1
    p = jnp.where(vis, _softmax_masked(sc, vis), 0.0)
    o_cmp = _dot(p.astype(BF16), vcb)
    imp = p[0:tq]
    for r in range(1, R):
        imp = imp + p[r * tq:(r + 1) * tq]

    cur = (t0 + _iota((tq, nb), 0)) // BLOCK
    sel = jnp.where(_top_blocks(imp, cur, min(TOP_N, nb)), 1.0, 0.0).astype(BF16)

    def attend(k, v, W):
        s_ref[:, :W] = _dot_nt(q, k)
        for r in range(R):
            for i in range(tq // STRIP):
                bs = slice(i * STRIP, (i + 1) * STRIP)
                rs = slice(r * tq + i * STRIP, r * tq + (i + 1) * STRIP)
                sr = s_ref[rs, :W] + bias_ref[bs, :W]
                e = jnp.exp(sr - jnp.max(sr, axis=1, keepdims=True))
                l_ref[rs, :] = jnp.sum(e, axis=1, keepdims=True)
                e_ref[rs, :W] = e.astype(BF16)
        return _dot(e_ref[:, :W], v) / l_ref[...]

    for c in range(T // kc):
        @pl.when((t0 + tq - 1) // kc == c)
        def _():
            W = (c + 1) * kc
            expand = jnp.where(_iota((nb, W), 1) // BLOCK == _iota((nb, W), 0), 1.0, 0.0).astype(BF16)
            allowed = (_dot(sel, expand) > 0.5) & (_iota((tq, W), 1) <= t0 + _iota((tq, W), 0))
            bias_ref[:, :W] = jnp.where(allowed, 0.0, NEG)
            osel_ref[...] = attend(ks_ref[:W, :].astype(BF16), vs_ref[:W, :].astype(BF16), W)

    wlen = min(T, WINDOW + tq)
    start = pl.multiple_of(jnp.clip(t0 - WINDOW, 0, T - wlen), HEAD)
    diff = (t0 + _iota((tq, wlen), 0)) - (start + _iota((tq, wlen), 1))
    bias_ref[:, :wlen] = jnp.where((diff >= 0) & (diff <= WINDOW), 0.0, NEG)
    o_win = attend(kw_ref[pl.ds(start, wlen), :].astype(BF16), vw_ref[pl.ds(start, wlen), :].astype(BF16), wlen)

    gates = _sigmoid(bg_ref[...])
    for r in range(R):
        c0 = (g * R + r) * 3
        sl = slice(r * tq, (r + 1) * tq)
        y = (_gate_col(gates, c0) * o_cmp[sl] + _gate_col(gates, c0 + 1) * osel_ref[sl, :]
             + _gate_col(gates, c0 + 2) * o_win[sl])
        o_ref[:, r * HEAD:(r + 1) * HEAD] = y.astype(o_ref.dtype)


def nsa_prompt(qn, pooled, ksn, kwn, z, zbg, Bn, T, HQ, G, col_kv):
    R = HQ // G
    tq = _tile(T, (128, 64))
    nq = T // tq
    nb = T // BLOCK
    kvw = G * HEAD
    pooled3 = pooled.reshape(Bn, nb, 2 * kvw)
    cv = col_kv // HEAD
    kc = _tile(T, (512, 256, 128, 64))
    seq = lambda off: pl.BlockSpec((T, HEAD), lambda b, g, i: (b, off + g))
    return pl.pallas_call(
        functools.partial(_nsa_prompt_kernel, tq=tq, T=T, R=R, kc=kc),
        grid=(Bn, G, nq),
        in_specs=[pl.BlockSpec((tq, R * HEAD), lambda b, g, i: (b * nq + i, g)),
                  pl.BlockSpec((1, nb, HEAD), lambda b, g, i: (b, 0, g)),
                  pl.BlockSpec((1, nb, HEAD), lambda b, g, i: (b, 0, G + g)),
                  seq(0), seq(cv + 3 * G), seq(0), seq(cv + 5 * G),
                  pl.BlockSpec((tq, HEAD), lambda b, g, i: (b * nq + i, 0))],
        out_specs=pl.BlockSpec((tq, R * HEAD), lambda b, g, i: (b * nq + i, g)),
        out_shape=jax.ShapeDtypeStruct((Bn * T, HQ * HEAD), BF16),
        scratch_shapes=[pltpu.VMEM((tq, T), F32), pltpu.VMEM((R * tq, T), F32), pltpu.VMEM((R * tq, T), BF16),
                        pltpu.VMEM((R * tq, 1), F32), pltpu.VMEM((R * tq, HEAD), F32)],
        compiler_params=_params("parallel", "parallel", "arbitrary"),
        name="nsa_prompt",
    )(qn, pooled3, pooled3, ksn, z, kwn, z, zbg)


def _pool_pages_kernel(pt_ref, *refs, pp):
    ck_refs, cv_refs = refs[:pp], refs[pp:2 * pp]
    pw_ref, pk_ref, pv_ref = refs[2 * pp:]
    per = PAGE_SIZE // BLOCK
    for q in range(pp):
        for half in range(per):
            sl = slice(half * BLOCK, (half + 1) * BLOCK)
            pk_ref[0, q * per + half] = jnp.sum(ck_refs[q][0, 0, sl] * pw_ref[0], axis=0)
            pv_ref[0, q * per + half] = jnp.sum(cv_refs[q][0, 0, sl] * pw_ref[1], axis=0)


def pool_pages(cache_k, cache_v, page_table, pool_w3, layer):
    Bs, n_pages = page_table.shape
    G = cache_k.shape[3]
    per = PAGE_SIZE // BLOCK
    pp = _tile(n_pages, (8, 4, 2, 1))
    page = lambda q: pl.BlockSpec((1, 1, PAGE_SIZE, G, HEAD),
                                  lambda b, p, pt: (layer, pt[b * n_pages + p * pp + q], 0, 0, 0))
    out = pl.BlockSpec((1, pp * per, G, HEAD), lambda b, p, pt: (b, p, 0, 0))
    return pl.pallas_call(
        functools.partial(_pool_pages_kernel, pp=pp),
        grid_spec=pltpu.PrefetchScalarGridSpec(
            num_scalar_prefetch=1,
            grid=(Bs, n_pages // pp),
            in_specs=[page(q) for q in range(pp)] * 2
                     + [pl.BlockSpec((2, BLOCK, G, HEAD), lambda b, p, pt: (0, 0, 0, 0))],
            out_specs=[out, out]),
        out_shape=[jax.ShapeDtypeStruct((Bs, n_pages * per, G, HEAD), F32)] * 2,
        compiler_params=_params("parallel", "parallel"),
        name="pool_pages",
    )(page_table.reshape(-1), *([cache_k] * pp), *([cache_v] * pp), pool_w3)


def _head_rows(qrow, g, R):
    rows = [qrow[:, (g * R + r) * HEAD:(g * R + r + 1) * HEAD] for r in range(R)]
    rows.append(jnp.zeros((8 - R, HEAD), F32))
    return jnp.concatenate(rows, axis=0).astype(BF16)


def _sample_cmp_win_kernel(q_ref, pk_ref, pv_ref, cwk_ref, cwv_ref, kwn_ref, vw_ref, bg_ref, g_ref,
                           pc_ref, pw_ref, idx_ref, *, G, R, past):
    nbp = pk_ref.shape[1]
    wb = cwk_ref.shape[2]
    qpos = past
    cur = qpos // BLOCK
    nlane = (cur // HEAD + 1) * HEAD
    qrow = q_ref[0]
    gates = _sigmoid(bg_ref[0])
    qk_g = g_ref[...]
    for g in range(G):
        sl = slice(g * HEAD, (g + 1) * HEAD)
        qg = _head_rows(qrow, g, R)

        kcb = _rms(pk_ref[0, :, g, :], qk_g[1:2]).astype(BF16)
        sc = _dot_nt(qg, kcb)
        vis = (_iota((8, nbp), 1) + 1) * BLOCK <= qpos + 1
        p = jnp.where(vis, _softmax_masked(sc, vis), 0.0)
        o_cmp = _dot(p.astype(BF16), pv_ref[0, :, g, :].astype(BF16))
        imp = p[0:1]
        for r in range(1, R):
            imp = imp + p[r:r + 1]

        imp = jnp.concatenate([imp, jnp.zeros((1, nlane - nbp), F32)], axis=1)
        j = _iota((1, nlane), 1)
        forced = (j == 0) | (j == cur) | (j == cur - 1)
        score = jnp.where(j <= cur, imp + jnp.where(forced, FORCE_BONUS, 0.0), NEG)
        s_col = _column(score, nlane)
        ii = _iota((nlane, nlane), 0)
        jj = _iota((nlane, nlane), 1)
        ahead = (s_col > score) | ((s_col == score) & (ii < jj))
        rank = jnp.sum(ahead.astype(F32), axis=0, keepdims=True)
        rank_col = _column(rank, nlane)
        hit = rank_col == _iota((nlane, HEAD), 1).astype(F32)
        idx = jnp.sum(jnp.where(hit, _iota((nlane, HEAD), 0).astype(F32), 0.0), axis=0, keepdims=True)
        idx_ref[0, g:g + 1, :] = idx.astype(I32)

        kwc = cwk_ref[0, 0, :, g, :].astype(BF16)
        sw = _dot_nt(qg, kwc)
        diff = qpos - (past - wb + _iota((8, wb), 1))
        sw = jnp.where((diff >= 0) & (diff <= WINDOW), sw, NEG)
        kn = kwn_ref[0][:, sl].astype(BF16).astype(F32)
        vn = vw_ref[0][:, sl].astype(BF16).astype(F32)
        sn = jnp.sum(qg.astype(F32) * kn, axis=1, keepdims=True)
        m = jnp.maximum(jnp.max(sw, axis=1, keepdims=True), sn)
        e = jnp.exp(sw - m)
        en = jnp.exp(sn - m)
        den = jnp.sum(e, axis=1, keepdims=True) + en
        o_win = (_dot((e / den).astype(BF16), cwv_ref[0, 0, :, g, :].astype(BF16))
                 + (en / den).astype(BF16).astype(F32) * vn)

        for r in range(R):
            c0 = (g * R + r) * 3
            hs = slice((g * R + r) * HEAD, (g * R + r + 1) * HEAD)
            pc_ref[0, :, hs] = gates[:, c0:c0 + 1] * o_cmp[r:r + 1]
            pw_ref[0, :, hs] = gates[:, c0 + 2:c0 + 3] * o_win[r:r + 1]


def sample_cmp_win(qn3, pk, pv, cache_win_k, cache_win_v, kwn3, z3, zbg3, qk_g, layer, G, R, past, col_vw):
    Bs = qn3.shape[0]
    depth, _, wb = cache_win_k.shape[:3]
    kvw = G * HEAD
    qw = G * R * HEAD
    nbp = pk.shape[1]
    row = lambda w, j=0: pl.BlockSpec((1, 1, w), lambda b: (b, 0, j))
    return pl.pallas_call(
        functools.partial(_sample_cmp_win_kernel, G=G, R=R, past=past),
        grid=(Bs,),
        in_specs=[row(qw),
                  pl.BlockSpec((1, nbp, G, HEAD), lambda b: (b, 0, 0, 0)),
                  pl.BlockSpec((1, nbp, G, HEAD), lambda b: (b, 0, 0, 0)),
                  pl.BlockSpec((1, 1, wb, G, HEAD), lambda b: (layer, b, 0, 0, 0)),
                  pl.BlockSpec((1, 1, wb, G, HEAD), lambda b: (layer, b, 0, 0, 0)),
                  row(kvw), row(kvw, col_vw // kvw), row(HEAD),
                  pl.BlockSpec((4, HEAD), lambda b: (0, 0))],
        out_specs=[row(qw), row(qw), pl.BlockSpec((1, G, HEAD), lambda b: (b, 0, 0))],
        out_shape=[jax.ShapeDtypeStruct((Bs, 1, qw), F32), jax.ShapeDtypeStruct((Bs, 1, qw), F32),
                   jax.ShapeDtypeStruct((Bs, G, HEAD), I32)],
        compiler_params=_params("parallel"),
        name="sample_cmp_win",
    )(qn3, pk, pv, cache_win_k, cache_win_v, kwn3, z3, zbg3, qk_g)


def _sample_select_kernel(idx_ref, pt_ref, q_ref, *refs, G, R, n_top, past):
    ck_refs, cv_refs = refs[:G], refs[G:2 * G]
    kn_ref, vn_ref, bg_ref, pc_ref, pw_ref, o_ref, m_ref, l_ref, acc_ref = refs[2 * G:]
    b, n = pl.program_id(0), pl.program_id(1)
    qpos = past

    @pl.when(n == 0)
    def _():
        m_ref[...] = jnp.full_like(m_ref, NEG)
        l_ref[...] = jnp.zeros_like(l_ref)
        acc_ref[...] = jnp.zeros_like(acc_ref)

    qrow = q_ref[0]
    for g in range(G):
        sl = slice(g * HEAD, (g + 1) * HEAD)
        j = idx_ref[(b * G + g) * n_top + n]
        qg = _head_rows(qrow, g, R)
        s = _dot_nt(qg, ck_refs[g][0, 0, :, g, :].astype(BF16))
        pos = j * BLOCK + _iota((8, BLOCK), 1)
        s = jnp.where((pos < past) & (pos <= qpos), s, NEG)
        kn = kn_ref[0][:, sl].astype(BF16).astype(F32)
        vn = vn_ref[0][:, sl].astype(BF16).astype(F32)
        sn = jnp.sum(qg.astype(F32) * kn, axis=1, keepdims=True)
        sn = jnp.where(j == past // BLOCK, sn, NEG)

        m_old = m_ref[g][:, 0:1]
        m_new = jnp.maximum(m_old, jnp.maximum(jnp.max(s, axis=1, keepdims=True), sn))
        alpha = jnp.exp(m_old - m_new)
        e = jnp.exp(s - m_new)
        en = jnp.exp(sn - m_new)
        l_new = alpha * l_ref[g][:, 0:1] + jnp.sum(e, axis=1, keepdims=True) + en
        acc = (alpha * acc_ref[g] + _dot(e.astype(BF16), cv_refs[g][0, 0, :, g, :].astype(BF16))
               + en.astype(BF16).astype(F32) * vn)
        m_ref[g] = jnp.broadcast_to(m_new, (8, HEAD))
        l_ref[g] = jnp.broadcast_to(l_new, (8, HEAD))
        acc_ref[g] = acc

    @pl.when(n == n_top - 1)
    def _():
        gates = _sigmoid(bg_ref[0])
        for g in range(G):
            o_sel = acc_ref[g] / l_ref[g][:, 0:1]
            for r in range(R):
                c = (g * R + r) * 3 + 1
                hs = slice((g * R + r) * HEAD, (g * R + r + 1) * HEAD)
                o_ref[0, :, hs] = ((pc_ref[0, :, hs] + gates[:, c:c + 1] * o_sel[r:r + 1])
                                   + pw_ref[0, :, hs]).astype(o_ref.dtype)


def sample_select(idx, page_table, qn3, cache_k, cache_v, ksn3, z3, zbg3, pc, pw, layer, G, R, past, col_vs):
    Bs, n_pages = page_table.shape
    n_top = idx.shape[-1]
    depth, n_pool = cache_k.shape[:2]
    kvw, qw = G * HEAD, G * R * HEAD
    per = PAGE_SIZE // BLOCK
    ck = cache_k.reshape(depth, n_pool * per, BLOCK, G, HEAD)
    cv = cache_v.reshape(depth, n_pool * per, BLOCK, G, HEAD)
    nbp = past // BLOCK

    def blk(g):
        def index(b, n, idx_ref, pt_ref):
            jc = jnp.minimum(idx_ref[(b * G + g) * n_top + n], nbp - 1)
            return (layer, pt_ref[b * n_pages + jc // per] * per + jc % per, 0, 0, 0)
        return pl.BlockSpec((1, 1, BLOCK, G, HEAD), index)

    row = lambda w, j=0: pl.BlockSpec((1, 1, w), lambda b, n, i_, p_: (b, 0, j))
    return pl.pallas_call(
        functools.partial(_sample_select_kernel, G=G, R=R, n_top=n_top, past=past),
        grid_spec=pltpu.PrefetchScalarGridSpec(
            num_scalar_prefetch=2,
            grid=(Bs, n_top),
            in_specs=[row(qw)] + [blk(g) for g in range(G)] * 2
                     + [row(kvw), row(kvw, col_vs // kvw), row(HEAD), row(qw), row(qw)],
            out_specs=row(qw),
            scratch_shapes=[pltpu.VMEM((G, 8, HEAD), F32)] * 3),
        out_shape=jax.ShapeDtypeStruct((Bs, 1, qw), BF16),
        compiler_params=_params("parallel", "arbitrary"),
        name="sample_select",
    )(idx.reshape(-1), page_table.reshape(-1), qn3, *([ck] * G), *([cv] * G), ksn3, z3, zbg3, pc, pw)


def _cast_kernel(x_ref, o_ref):
    o_ref[...] = x_ref[0].astype(o_ref.dtype)


def cast_layer_bf16(w, layer):
    _, R, C = w.shape
    tr = next((t for t in (512, 256, 128, 64, 32, 16) if R % t == 0 and t * C * 4 <= CAST_BLOCK_BYTES), R)
    return pl.pallas_call(
        _cast_kernel,
        grid=(R // tr,),
        in_specs=[pl.BlockSpec((1, tr, C), lambda i: (layer, i, 0))],
        out_specs=pl.BlockSpec((tr, C), lambda i: (i, 0)),
        out_shape=jax.ShapeDtypeStruct((R, C), BF16),
        compiler_params=_params("parallel"),
        name="cast_layer_bf16",
    )(w)


def _cast_cut_kernel(a_ref, b_ref, o_ref, *, nj0, cut):
    j = pl.program_id(1)
    tn = o_ref.shape[1]

    @pl.when(j < nj0)
    def _():
        o_ref[...] = a_ref[0].astype(o_ref.dtype)

    @pl.when(j >= nj0)
    def _():
        x = jnp.concatenate([a_ref[0], b_ref[0]], axis=1)
        o_ref[...] = x[:, cut:cut + tn].astype(o_ref.dtype)


def cast_cut_bf16(w, layer, c0, cut):
    _, R, C = w.shape
    N = C - cut
    tn = next(t for t in (1024, 512, 256, 128) if c0 % t == 0 and N % t == 0)
    assert cut < HEAD
    tr = _tile(R, (512, 256, 128, 64, 32, 16))
    nj0 = c0 // tn
    return pl.pallas_call(
        functools.partial(_cast_cut_kernel, nj0=nj0, cut=cut),
        grid=(R // tr, N // tn),
        in_specs=[pl.BlockSpec((1, tr, tn), lambda i, j: (layer, i, j)),
                  pl.BlockSpec((1, tr, HEAD), lambda i, j: (layer, i, (j + 1) * (tn // HEAD)))],
        out_specs=pl.BlockSpec((tr, tn), lambda i, j: (i, j)),
        out_shape=jax.ShapeDtypeStruct((R, N), BF16),
        compiler_params=_params("parallel", "parallel"),
        name="cast_cut_bf16",
    )(w, w)


def _layer_weights(l, norm_g, w_in, qk_g, cmp_pool, w_pa, w_pb, w_o, f1_gu, f1_dn, f2_gu, f2_dn, H, HQ, G):
    kw, qw, kvw = H * HEAD, HQ * HEAD, G * HEAD
    D = w_in.shape[1]
    c_bg = 4 * kw + qw + 6 * kvw
    n_bg = 3 * HQ
    w_main = cast_cut_bf16(w_in, l, c_bg, n_bg)
    w_bg = jnp.pad(w_in[l, :, c_bg:c_bg + n_bg], ((0, 0), (0, HEAD - n_bg))).astype(BF16)
    pool_w2 = jnp.repeat(jnp.transpose(cmp_pool[l], (1, 0, 2)).reshape(BLOCK, 2 * G), HEAD, axis=1)
    pool_w3 = jnp.broadcast_to(cmp_pool[l][..., None], cmp_pool.shape[1:] + (HEAD,))
    cast = lambda w: cast_layer_bf16(w, l)
    return dict(norm_g=norm_g[l], w_main=w_main, w_bg=w_bg, qk_g=qk_g[l], pool_w2=pool_w2, pool_w3=pool_w3,
                w_pa=cast(w_pa), w_pb=cast(w_pb), w_o=cast(w_o),
                ffn1=(cast(f1_gu), cast(f1_dn)), ffn2=(cast(f2_gu), cast(f2_dn)), c_bg=c_bg, D=D)


def _ffn(x, g, ffn_w):
    gu, dn = ffn_w
    h = matmul_swiglu(rmsnorm_rows(x, g), gu)
    return matmul(h, dn, res=x, scale=0.5)


def _layer(x, lw, lb, out_g, dims, mixers):
    H, HQ, G = dims
    h = _ffn(x, lw["norm_g"][0], lw["ffn1"])
    u = rmsnorm_rows(h, lw["norm_g"][1])
    z = matmul(u, lw["w_main"])
    zbg = matmul(u, lw["w_bg"])
    ya, yb, state = mixers(z, zbg)
    m = merge_project(ya, yb, lw["w_pa"], lw["w_pb"], z, lw["c_bg"], lw["c_bg"] + lw["D"])
    h = matmul(m, lw["w_o"], res=h, scale=1.0)
    return _ffn(h, lw["norm_g"][2], lw["ffn2"]), state


def kernel(x_prompt, x_sample, state_hgrn, cache_cmp_k, cache_cmp_v, cache_sel_k, cache_sel_v, cache_win_k, cache_win_v, page_table, norm_g, w_in, hgrn_lb, hgrn_out_g, qk_g, cmp_pool, w_proj_a, w_proj_b, w_out, ffn1_gu, ffn1_down, ffn2_gu, ffn2_down):
    Bn, T, D = x_prompt.shape
    Bs, Tn, _ = x_sample.shape
    depth = w_in.shape[0]
    H = hgrn_lb.shape[1] // HEAD
    HQ = w_proj_b.shape[1] // HEAD
    G = cache_cmp_k.shape[3]
    R = HQ // G
    kw, qw, kvw = H * HEAD, HQ * HEAD, G * HEAD
    past = page_table.shape[1] * PAGE_SIZE
    wb = cache_win_k.shape[2]
    assert Tn == 1 and T % HGRN_CHUNK == 0 and past % PAGE_SIZE == 0 and R <= 8
    assert hgrn_out_g.shape[1] == HEAD and w_proj_a.shape[1] == kw and wb <= WINDOW and wb <= past
    col_q = 4 * kw
    col_kv = col_q + qw
    dims = (H, HQ, G)

    p = jax.nn.softmax(hgrn_lb.astype(F32), axis=0)
    lbs = jnp.cumsum(p, axis=0) - p[0]

    y_p = x_prompt.reshape(Bn * T, D)
    y_s = x_sample.reshape(Bs, D)
    st_p, st_s = [], []
    for l in range(depth):
        lw = _layer_weights(l, norm_g, w_in, qk_g, cmp_pool, w_proj_a, w_proj_b, w_out,
                            ffn1_gu, ffn1_down, ffn2_gu, ffn2_down, H, HQ, G)

        def prompt_mixers(z, zbg):
            ya, s_new = hgrn_prompt(z, lbs[l], hgrn_out_g[l], Bn, T, H)
            qn, ksn, kwn, pooled = nsa_prep(z, lw["pool_w2"], lw["qk_g"], HQ, G, col_q, True)
            yb = nsa_prompt(qn, pooled, ksn, kwn, z, zbg, Bn, T, HQ, G, col_kv)
            kv = lambda i: z[:, col_kv + i * kvw:col_kv + (i + 1) * kvw].reshape(Bn, T, G, HEAD)
            tail = lambda a: a[:, T - min(WINDOW, T):]
            return ya, yb, (s_new, kv(0), kv(1), ksn.reshape(Bn, T, G, HEAD), kv(3),
                            tail(kwn.reshape(Bn, T, G, HEAD)), tail(kv(5)))

        def sample_mixers(z, zbg):
            z3, zbg3 = z.reshape(Bs, 1, -1), zbg.reshape(Bs, 1, HEAD)
            ya, s_new = hgrn_sample(z3, lbs[l], hgrn_out_g[l], state_hgrn, l, H)
            qn, ksn, kwn = nsa_prep(z, lw["pool_w2"], lw["qk_g"], HQ, G, col_q, False)
            qn3, ksn3, kwn3 = qn.reshape(Bs, 1, qw), ksn.reshape(Bs, 1, kvw), kwn.reshape(Bs, 1, kvw)
            pk, pv = pool_pages(cache_cmp_k, cache_cmp_v, page_table, lw["pool_w3"], l)
            pc, pw, idx = sample_cmp_win(qn3, pk, pv, cache_win_k, cache_win_v, kwn3, z3, zbg3, lw["qk_g"],
                                         l, G, R, past, col_kv + 5 * kvw)
            n_top = min(TOP_N, -(-(past + Tn) // BLOCK))
            yb = sample_select(idx[:, :, :n_top], page_table, qn3, cache_sel_k, cache_sel_v, ksn3, z3, zbg3,
                               pc, pw, l, G, R, past, col_kv + 3 * kvw)
            kv = lambda i: z[:, col_kv + i * kvw:col_kv + (i + 1) * kvw].reshape(Bs, 1, G, HEAD)
            win_k = jnp.concatenate([cache_win_k[l], kwn.reshape(Bs, 1, G, HEAD)], axis=1)[:, -wb:]
            win_v = jnp.concatenate([cache_win_v[l], kv(5)], axis=1)[:, -wb:]
            return ya.reshape(Bs, kw), yb.reshape(Bs, qw), (
                s_new, kv(0), kv(1), ksn.reshape(Bs, 1, G, HEAD), kv(3), win_k, win_v)

        y_p, sp = _layer(y_p, lw, lbs[l], hgrn_out_g[l], dims, prompt_mixers)
        y_s, ss = _layer(y_s, lw, lbs[l], hgrn_out_g[l], dims, sample_mixers)
        st_p.append(sp)
        st_s.append(ss)

    stack = lambda states, i: jnp.stack([s[i] for s in states])
    return ((y_p.reshape(Bn, T, D), y_s.reshape(Bs, Tn, D))
            + tuple(stack(st_p, i) for i in range(7)) + tuple(stack(st_s, i) for i in range(7)))
```

```python
import functools

import jax
import jax.numpy as jnp
from jax import lax
from jax.experimental import pallas as pl
from jax.experimental.pallas import tpu as pltpu

F32 = jnp.float32
BF16 = jnp.bfloat16
I32 = jnp.int32

HEAD = 128
BLOCK = 64
TOP_N = 16
WINDOW = 512
PAGE_SIZE = 128
EPS = 1e-6
NEG = -1e30
LB_FLOOR = 1e-30
FORCE_BONUS = 1e4
ATTN_SCALE = HEAD ** -0.5
HGRN_CHUNK = 128
DIAG = 4
STRIP = 16
CAST_BLOCK_BYTES = 8 * 1024 * 1024
VMEM_BYTES_V7X = 64 * 1024 * 1024
VMEM_LIMIT = VMEM_BYTES_V7X - 8 * 1024 * 1024


def _params(*sem):
    return pltpu.CompilerParams(dimension_semantics=sem, vmem_limit_bytes=VMEM_LIMIT)


def _tile(dim, candidates):
    for c in candidates:
        if dim % c == 0:
            return c
    return dim


def _dot(a, b):
    return jnp.dot(a, b, preferred_element_type=F32)


def _dot_nt(a, b):
    return lax.dot_general(a, b, (((1,), (1,)), ((), ())), preferred_element_type=F32)


def _dot_tn(a, b):
    return lax.dot_general(a, b, (((0,), (0,)), ((), ())), preferred_element_type=F32)


def _sigmoid(x):
    return jax.nn.sigmoid(x)


def _silu(x):
    return x * jax.nn.sigmoid(x)


def _log_sigmoid(x):
    return jnp.minimum(x, 0.0) - jnp.log1p(jnp.exp(-jnp.abs(x)))


def _logaddexp(a, b):
    return jnp.maximum(a, b) + jnp.log1p(jnp.exp(-jnp.abs(a - b)))


def _rms(x, g):
    ms = jnp.mean(x * x, axis=-1, keepdims=True)
    return x * lax.rsqrt(ms + EPS) * g


def _softmax_masked(s, mask):
    s = jnp.where(mask, s, NEG)
    m = jnp.max(s, axis=-1, keepdims=True)
    e = jnp.exp(s - m)
    return e / jnp.sum(e, axis=-1, keepdims=True)


def _iota(shape, dim):
    return lax.broadcasted_iota(I32, shape, dim)


def _rmsnorm_kernel(x_ref, g_ref, o_ref):
    o_ref[...] = _rms(x_ref[...], g_ref[...]).astype(o_ref.dtype)


def rmsnorm_rows(x, g):
    M, D = x.shape
    tm = _tile(M, (256, 128, 64, 32, 16, 8))
    return pl.pallas_call(
        _rmsnorm_kernel,
        grid=(M // tm,),
        in_specs=[pl.BlockSpec((tm, D), lambda i: (i, 0)), pl.BlockSpec((1, D), lambda i: (0, 0))],
        out_specs=pl.BlockSpec((tm, D), lambda i: (i, 0)),
        out_shape=jax.ShapeDtypeStruct((M, D), BF16),
        compiler_params=_params("parallel"),
        name="rmsnorm_rows",
    )(x, g.reshape(1, D))


def _mm_kernel(x_ref, w_ref, *rest, nk, scale, has_res):
    if has_res:
        r_ref, o_ref = rest[0], rest[1]
        scratch = rest[2:]
    else:
        r_ref, o_ref = None, rest[0]
        scratch = rest[1:]

    def finish(acc):
        if has_res:
            o_ref[...] = (r_ref[...] + scale * acc).astype(o_ref.dtype)
        else:
            o_ref[...] = acc.astype(o_ref.dtype)

    if nk == 1:
        finish(_dot(x_ref[...], w_ref[...]))
    else:
        acc_ref = scratch[0]
        k = pl.program_id(2)

        @pl.when(k == 0)
        def _():
            acc_ref[...] = jnp.zeros_like(acc_ref)

        acc_ref[...] += _dot(x_ref[...], w_ref[...])

        @pl.when(k == nk - 1)
        def _():
            finish(acc_ref[...])


def matmul(x, w, *, out_dtype=F32, res=None, scale=1.0, tm=None, tn=None, tk=None):
    M, K = x.shape
    N = w.shape[1]
    tm = tm or _tile(M, (1024, 512, 256, 128, 64, 32, 16, 8))
    if K <= 4096:
        tk = tk or K
        tn = tn or _tile(N, (512, 256, 128) if res is not None else (1024, 512, 256, 128))
    else:
        tk = tk or next((K // d for d in (2, 3, 4, 5, 6, 8) if K % (d * HEAD) == 0), HEAD)
        tn = tn or _tile(N, (512, 256, 128))
    nk = K // tk
    has_res = res is not None
    in_specs = [pl.BlockSpec((tm, tk), lambda i, j, k: (i, k)), pl.BlockSpec((tk, tn), lambda i, j, k: (k, j))]
    args = [x, w]
    if has_res:
        in_specs.append(pl.BlockSpec((tm, tn), lambda i, j, k: (i, j)))
        args.append(res)
    return pl.pallas_call(
        functools.partial(_mm_kernel, nk=nk, scale=scale, has_res=has_res),
        grid=(M // tm, N // tn, nk),
        in_specs=in_specs,
        out_specs=pl.BlockSpec((tm, tn), lambda i, j, k: (i, j)),
        out_shape=jax.ShapeDtypeStruct((M, N), out_dtype),
        scratch_shapes=[pltpu.VMEM((tm, tn), F32)] if nk > 1 else [],
        compiler_params=_params("parallel", "parallel", "arbitrary"),
        name="matmul",
    )(*args)


def _swiglu_kernel(x_ref, xs_ref, wg_ref, wu_ref, o_ref, os_ref):
    wg = wg_ref[0].astype(BF16)
    wu = wu_ref[0].astype(BF16)
    for src, dst in ((x_ref, o_ref), (xs_ref, os_ref)):
        x = src[...]
        dst[...] = (_silu(_dot(x, wg)) * _dot(x, wu)).astype(dst.dtype)


def matmul_swiglu(x, xs, w_gu, layer):
    M, K = x.shape
    Ms = xs.shape[0]
    F = w_gu.shape[2] // 2
    tn = _tile(F, (256, 128))
    tm = _tile(M, (2048, 1024, 512, 256, 128, 64, 32, 16, 8))
    nj = F // tn
    return pl.pallas_call(
        _swiglu_kernel,
        grid=(M // tm, nj),
        in_specs=[pl.BlockSpec((tm, K), lambda i, j: (i, 0), pipeline_mode=pl.Buffered(1)),
                  pl.BlockSpec((Ms, K), lambda i, j: (0, 0)),
                  pl.BlockSpec((1, K, tn), lambda i, j: (layer, 0, j)),
                  pl.BlockSpec((1, K, tn), lambda i, j: (layer, 0, j + nj))],
        out_specs=[pl.BlockSpec((tm, tn), lambda i, j: (i, j)), pl.BlockSpec((Ms, tn), lambda i, j: (0, j))],
        out_shape=[jax.ShapeDtypeStruct((M, F), BF16), jax.ShapeDtypeStruct((Ms, F), BF16)],
        compiler_params=_params("arbitrary", "arbitrary"),
        name="matmul_swiglu",
    )(x, xs, w_gu, w_gu)


def _panel_kernel(x_ref, xs_ref, w_ref, r_ref, rs_ref, o_ref, os_ref, *, scale):
    w = w_ref[...]
    for src, res, dst in ((x_ref, r_ref, o_ref), (xs_ref, rs_ref, os_ref)):
        dst[...] = res[...] + scale * _dot(src[...], w)


def matmul_panel(x, xs, w, res, ress, scale):
    M, K = x.shape
    Ms = xs.shape[0]
    N = w.shape[1]
    tm = _tile(M, (1024, 512, 256, 128, 64, 32, 16, 8))
    tn = _tile(N, (256, 128))
    return pl.pallas_call(
        functools.partial(_panel_kernel, scale=scale),
        grid=(M // tm, N // tn),
        in_specs=[pl.BlockSpec((tm, K), lambda i, j: (i, 0), pipeline_mode=pl.Buffered(1)),
                  pl.BlockSpec((Ms, K), lambda i, j: (0, 0)),
                  pl.BlockSpec((K, tn), lambda i, j: (0, j)),
                  pl.BlockSpec((tm, tn), lambda i, j: (i, j)),
                  pl.BlockSpec((Ms, tn), lambda i, j: (0, j))],
        out_specs=[pl.BlockSpec((tm, tn), lambda i, j: (i, j)), pl.BlockSpec((Ms, tn), lambda i, j: (0, j))],
        out_shape=[jax.ShapeDtypeStruct((M, N), F32), jax.ShapeDtypeStruct((Ms, N), F32)],
        compiler_params=_params("arbitrary", "arbitrary"),
        name="matmul_panel",
    )(x, xs, w, res, ress)


def _merge_kernel(ya_ref, yb_ref, wa_ref, wb_ref, ga_ref, gb_ref, o_ref):
    pa = _dot(ya_ref[...], wa_ref[...])
    pb = _dot(yb_ref[...], wb_ref[...])
    o_ref[...] = (_sigmoid(ga_ref[...]) * pa + _sigmoid(gb_ref[...]) * pb).astype(o_ref.dtype)


def merge_project(ya, yb, wa, wb, z, col_a, col_b):
    M, Ka = ya.shape
    Kb = yb.shape[1]
    N = wa.shape[1]
    tm = _tile(M, (1024, 512, 256, 128, 64, 32, 16, 8))
    tn = _tile(N, (512, 256, 128))
    ja, jb = col_a // tn, col_b // tn
    return pl.pallas_call(
        _merge_kernel,
        grid=(M // tm, N // tn),
        in_specs=[pl.BlockSpec((tm, Ka), lambda i, j: (i, 0)),
                  pl.BlockSpec((tm, Kb), lambda i, j: (i, 0)),
                  pl.BlockSpec((Ka, tn), lambda i, j: (0, j)),
                  pl.BlockSpec((Kb, tn), lambda i, j: (0, j)),
                  pl.BlockSpec((tm, tn), lambda i, j: (i, j + ja)),
                  pl.BlockSpec((tm, tn), lambda i, j: (i, j + jb))],
        out_specs=pl.BlockSpec((tm, tn), lambda i, j: (i, j)),
        out_shape=jax.ShapeDtypeStruct((M, N), BF16),
        compiler_params=_params("parallel", "parallel"),
        name="merge_project",
    )(ya, yb, wa, wb, z, z)


def _split3(x):
    hi = x.astype(BF16)
    r1 = x - hi.astype(F32)
    mid = r1.astype(BF16)
    lo = (r1 - mid.astype(F32)).astype(BF16)
    return hi, mid, lo


def _hgrn_gates(zq, zf, lb):
    q = _silu(zq)
    log_f = _logaddexp(jnp.log(jnp.maximum(lb, LB_FLOOR)), jnp.log1p(-lb) + _log_sigmoid(zf))
    k = (1.0 - lb) * _sigmoid(-zf)
    return q, log_f, k


def _hgrn_chunk(zq, zf, v, zg, lb, og, st):
    C = HGRN_CHUNK
    q, log_f, k = _hgrn_gates(zq, zf, lb)
    vb = v.astype(BF16)
    row = _iota((C, C), 0)
    col = _iota((C, C), 1)

    tri = jnp.where(col <= row, 1.0, 0.0).astype(BF16)
    hi, mid, lo = _split3(log_f)
    b = _dot(tri, hi) + _dot(tri, mid) + _dot(tri, lo)

    a = jnp.zeros((C, C), F32)
    h = C // 2
    while h >= DIAG:
        P = 2 * h
        bref = jnp.concatenate(
            [jnp.broadcast_to(b[m * P + h - 1:m * P + h, :], (P, HEAD)) for m in range(C // P)], axis=0)
        second = (row & (P - 1)) >= h
        qt = q * jnp.exp(jnp.where(second, b - bref, NEG))
        kt = k * jnp.exp(jnp.where(second, NEG, bref - b))
        lvl = _dot_nt(qt.astype(BF16), kt.astype(BF16))
        if P < C:
            lvl = jnp.where((row // P) == (col // P), lvl, 0.0)
        a = a + lvl
        h //= 2

    rloc = row & (DIAG - 1)
    for d in range(DIAG):
        if d == 0:
            p = q * k
        else:
            bs = pltpu.roll(b, d, 0)
            ks = pltpu.roll(k, d, 0)
            p = q * jnp.exp(jnp.where(rloc >= d, b - bs, NEG)) * ks
        a = jnp.where((col == row - d) & (rloc >= d), jnp.sum(p, axis=1, keepdims=True), a)

    o = _dot_nt((q * jnp.exp(b)).astype(BF16), st.astype(BF16)) + _dot(a.astype(BF16), vb)
    b_last = b[C - 1:C, :]
    kd = k * jnp.exp(b_last - b)
    st_new = st * jnp.exp(b_last) + _dot_tn(vb, kd.astype(BF16))
    return _rms(o, og) * _silu(zg), st_new


def _hgrn_prompt_kernel(q_ref, f_ref, v_ref, g_ref, lb_ref, og_ref, y_ref, s_ref, st_ref, *, nc, hb):
    c = pl.program_id(2)

    @pl.when(c == 0)
    def _():
        st_ref[...] = jnp.zeros_like(st_ref)

    for i in range(hb):
        sl = slice(i * HEAD, (i + 1) * HEAD)
        y, st_new = _hgrn_chunk(q_ref[:, sl], f_ref[:, sl], v_ref[:, sl], g_ref[:, sl], lb_ref[:, sl],
                                og_ref[...], st_ref[i])
        st_ref[i] = st_new
        y_ref[:, sl] = y.astype(y_ref.dtype)

        @pl.when(c == nc - 1)
        def _():
            s_ref[0, i] = st_new.T


def hgrn_prompt(z, lb, out_g, Bn, T, H):
    C = HGRN_CHUNK
    nc = T // C
    hb = _tile(H, (4, 2, 1))
    w = hb * HEAD
    blk = lambda off: pl.BlockSpec((C, w), lambda b, h, c: (b * nc + c, off + h))
    return pl.pallas_call(
        functools.partial(_hgrn_prompt_kernel, nc=nc, hb=hb),
        grid=(Bn, H // hb, nc),
        in_specs=[blk(0), blk(H // hb), blk(2 * H // hb), blk(3 * H // hb),
                  pl.BlockSpec((1, w), lambda b, h, c: (0, h)),
                  pl.BlockSpec((1, HEAD), lambda b, h, c: (0, 0))],
        out_specs=[pl.BlockSpec((C, w), lambda b, h, c: (b * nc + c, h)),
                   pl.BlockSpec((1, hb, HEAD, HEAD), lambda b, h, c: (b, h, 0, 0))],
        out_shape=[jax.ShapeDtypeStruct((Bn * T, H * HEAD), BF16),
                   jax.ShapeDtypeStruct((Bn, H, HEAD, HEAD), F32)],
        scratch_shapes=[pltpu.VMEM((hb, HEAD, HEAD), F32)],
        compiler_params=_params("parallel", "parallel", "arbitrary"),
        name="hgrn_prompt",
    )(z, z, z, z, lb.reshape(1, H * HEAD), out_g.reshape(1, HEAD))


def _column(rowvec, n):
    eye = _iota((n, n), 0) == _iota((n, n), 1)
    return jnp.sum(jnp.where(eye, jnp.broadcast_to(rowvec, (n, n)), 0.0), axis=1, keepdims=True)


def _hgrn_sample_kernel(q_ref, f_ref, v_ref, g_ref, lb_ref, og_ref, s0_ref, y_ref, s_ref, *, H):
    for h in range(H):
        sl = slice(h * HEAD, (h + 1) * HEAD)
        q, log_f, k = _hgrn_gates(q_ref[0][:, sl], f_ref[0][:, sl], lb_ref[:, sl])
        v = v_ref[0][:, sl]
        s_new = _column(jnp.exp(log_f), HEAD) * s0_ref[0, 0, h] + _column(k, HEAD) * v
        o = jnp.sum(_column(q, HEAD) * s_new, axis=0, keepdims=True)
        y_ref[0, :, sl] = (_rms(o, og_ref[...]) * _silu(g_ref[0][:, sl])).astype(y_ref.dtype)
        s_ref[0, h] = s_new


def hgrn_sample(z3, lb, out_g, state, layer, H):
    Bs = z3.shape[0]
    w = H * HEAD
    blk = lambda off: pl.BlockSpec((1, 1, w), lambda b: (b, 0, off))
    return pl.pallas_call(
        functools.partial(_hgrn_sample_kernel, H=H),
        grid=(Bs,),
        in_specs=[blk(0), blk(1), blk(2), blk(3),
                  pl.BlockSpec((1, w), lambda b: (0, 0)),
                  pl.BlockSpec((1, HEAD), lambda b: (0, 0)),
                  pl.BlockSpec((1, 1, H, HEAD, HEAD), lambda b: (layer, b, 0, 0, 0))],
        out_specs=[pl.BlockSpec((1, 1, w), lambda b: (b, 0, 0)),
                   pl.BlockSpec((1, H, HEAD, HEAD), lambda b: (b, 0, 0, 0))],
        out_shape=[jax.ShapeDtypeStruct((Bs, 1, w), BF16),
                   jax.ShapeDtypeStruct((Bs, H, HEAD, HEAD), F32)],
        compiler_params=_params("parallel"),
        name="hgrn_sample",
    )(z3, z3, z3, z3, lb.reshape(1, w), out_g.reshape(1, HEAD), state)


def _nsa_prep_kernel(qb_ref, kv_ref, ks_ref, kw_ref, pw_ref, g_ref, qn_ref, ksn_ref, kwn_ref, *pool_refs,
                     HQ, G, pool):
    g = g_ref[...]
    for h in range(HQ):
        sl = slice(h * HEAD, (h + 1) * HEAD)
        qn_ref[:, sl] = (_rms(qb_ref[:, sl], g[0:1]) * ATTN_SCALE).astype(qn_ref.dtype)
    for j in range(G):
        sl = slice(j * HEAD, (j + 1) * HEAD)
        ksn_ref[:, sl] = _rms(ks_ref[:, sl], g[2:3])
        kwn_ref[:, sl] = _rms(kw_ref[:, sl], g[3:4])
    if pool:
        pool_ref = pool_refs[0]
        kvw = G * HEAD
        w = pw_ref[...]
        for n in range(kv_ref.shape[0] // BLOCK):
            blk = jnp.sum(kv_ref[n * BLOCK:(n + 1) * BLOCK, :] * w, axis=0, keepdims=True)
            for j in range(G):
                sl = slice(j * HEAD, (j + 1) * HEAD)
                pool_ref[n:n + 1, sl] = _rms(blk[:, sl], g[1:2])
            pool_ref[n:n + 1, kvw:] = blk[:, kvw:]


def nsa_prep(z, pool_w2, qk_g, HQ, G, col_q, pool):
    M = z.shape[0]
    qw, kvw = HQ * HEAD, G * HEAD
    tm = _tile(M, (512, 256, 128, 64, 8))
    col_kv = col_q + qw
    out_shape = [jax.ShapeDtypeStruct((M, qw), BF16 if pool else F32),
                 jax.ShapeDtypeStruct((M, kvw), F32), jax.ShapeDtypeStruct((M, kvw), F32)]
    out_specs = [pl.BlockSpec((tm, qw), lambda i: (i, 0)),
                 pl.BlockSpec((tm, kvw), lambda i: (i, 0)), pl.BlockSpec((tm, kvw), lambda i: (i, 0))]
    if pool:
        out_shape.append(jax.ShapeDtypeStruct((M // BLOCK, 2 * kvw), F32))
        out_specs.append(pl.BlockSpec((tm // BLOCK, 2 * kvw), lambda i: (i, 0)))
    return pl.pallas_call(
        functools.partial(_nsa_prep_kernel, HQ=HQ, G=G, pool=pool),
        grid=(M // tm,),
        in_specs=[pl.BlockSpec((tm, qw), lambda i: (i, col_q // qw)),
                  pl.BlockSpec((tm, 2 * kvw), lambda i: (i, col_kv // (2 * kvw))),
                  pl.BlockSpec((tm, kvw), lambda i: (i, col_kv // kvw + 2)),
                  pl.BlockSpec((tm, kvw), lambda i: (i, col_kv // kvw + 4)),
                  pl.BlockSpec((BLOCK, 2 * kvw), lambda i: (0, 0)),
                  pl.BlockSpec((4, HEAD), lambda i: (0, 0))],
        out_specs=out_specs,
        out_shape=out_shape,
        compiler_params=_params("parallel"),
        name="nsa_prep",
    )(z, z, z, z, pool_w2, qk_g)


def _gate_col(gates, c):
    return jnp.sum(jnp.where(_iota(gates.shape, 1) == c, gates, 0.0), axis=1, keepdims=True)


def _top_blocks(imp, cur, n_top):
    nb = imp.shape[1]
    j = _iota(imp.shape, 1)
    forced = (j == 0) | (j == cur) | (j == cur - 1)
    causal = j <= cur
    score = jnp.where(causal, imp + jnp.where(forced, FORCE_BONUS, 0.0), NEG)
    rank = jnp.zeros(imp.shape, I32)
    for i in range(nb):
        si = score[:, i:i + 1]
        rank = rank + ((si > score) | ((si == score) & (i < j))).astype(I32)
    return (rank < n_top) & causal


def _nsa_prompt_kernel(q_ref, pool_ref, vpool_ref, ks_ref, vs_ref, kw_ref, vw_ref, bg_ref, o_ref,
                       bias_ref, s_ref, e_ref, l_ref, osel_ref, *, tq, T, R, kc):
    g = pl.program_id(1)
    t0 = pl.program_id(2) * tq
    rows = R * tq
    nb = T // BLOCK
    q = jnp.concatenate([q_ref[:, r * HEAD:(r + 1) * HEAD] for r in range(R)], axis=0)

    kcb = pool_ref[0].astype(BF16)
    vcb = vpool_ref[0].astype(BF16)
    sc = _dot_nt(q, kcb)
    tpos = t0 + _iota((rows, nb), 0) % tq
    vis = (_iota((rows, nb), 1) + 1) * BLOCK <= tpos + 1
    p = jnp.where(vis, _softmax_masked(sc, vis), 0.0)
    o_cmp = _dot(p.astype(BF16), vcb)
    imp = p[0:tq]
    for r in range(1, R):
        imp = imp + p[r * tq:(r + 1) * tq]

    cur = (t0 + _iota((tq, nb), 0)) // BLOCK
    sel = jnp.where(_top_blocks(imp, cur, min(TOP_N, nb)), 1.0, 0.0).astype(BF16)

    def attend(k, v, W):
        s_ref[:, :W] = _dot_nt(q, k)
        for r in range(R):
            for i in range(tq // STRIP):
                bs = slice(i * STRIP, (i + 1) * STRIP)
                rs = slice(r * tq + i * STRIP, r * tq + (i + 1) * STRIP)
                sr = s_ref[rs, :W] + bias_ref[bs, :W]
                e = jnp.exp(sr - jnp.max(sr, axis=1, keepdims=True))
                l_ref[rs, :] = jnp.sum(e, axis=1, keepdims=True)
                e_ref[rs, :W] = e.astype(BF16)
        return _dot(e_ref[:, :W], v) / l_ref[...]

    for c in range(T // kc):
        @pl.when((t0 + tq - 1) // kc == c)
        def _():
            W = (c + 1) * kc
            expand = jnp.where(_iota((nb, W), 1) // BLOCK == _iota((nb, W), 0), 1.0, 0.0).astype(BF16)
            allowed = (_dot(sel, expand) > 0.5) & (_iota((tq, W), 1) <= t0 + _iota((tq, W), 0))
            bias_ref[:, :W] = jnp.where(allowed, 0.0, NEG)
            osel_ref[...] = attend(ks_ref[:W, :].astype(BF16), vs_ref[:W, :].astype(BF16), W)

    wlen = min(T, WINDOW + tq)
    start = pl.multiple_of(jnp.clip(t0 - WINDOW, 0, T - wlen), HEAD)
    diff = (t0 + _iota((tq, wlen), 0)) - (start + _iota((tq, wlen), 1))
    bias_ref[:, :wlen] = jnp.where((diff >= 0) & (diff <= WINDOW), 0.0, NEG)
    o_win = attend(kw_ref[pl.ds(start, wlen), :].astype(BF16), vw_ref[pl.ds(start, wlen), :].astype(BF16), wlen)

    gates = _sigmoid(bg_ref[...])
    for r in range(R):
        c0 = (g * R + r) * 3
        sl = slice(r * tq, (r + 1) * tq)
        y = (_gate_col(gates, c0) * o_cmp[sl] + _gate_col(gates, c0 + 1) * osel_ref[sl, :]
             + _gate_col(gates, c0 + 2) * o_win[sl])
        o_ref[:, r * HEAD:(r + 1) * HEAD] = y.astype(o_ref.dtype)


def nsa_prompt(qn, pooled, ksn, kwn, z, zbg, Bn, T, HQ, G, col_kv):
    R = HQ // G
    tq = _tile(T, (128, 64))
    nq = T // tq
    nb = T // BLOCK
    kvw = G * HEAD
    pooled3 = pooled.reshape(Bn, nb, 2 * kvw)
    cv = col_kv // HEAD
    kc = _tile(T, (512, 256, 128, 64))
    seq = lambda off: pl.BlockSpec((T, HEAD), lambda b, g, i: (b, off + g))
    return pl.pallas_call(
        functools.partial(_nsa_prompt_kernel, tq=tq, T=T, R=R, kc=kc),
        grid=(Bn, G, nq),
        in_specs=[pl.BlockSpec((tq, R * HEAD), lambda b, g, i: (b * nq + i, g)),
                  pl.BlockSpec((1, nb, HEAD), lambda b, g, i: (b, 0, g)),
                  pl.BlockSpec((1, nb, HEAD), lambda b, g, i: (b, 0, G + g)),
                  seq(0), seq(cv + 3 * G), seq(0), seq(cv + 5 * G),
                  pl.BlockSpec((tq, HEAD), lambda b, g, i: (b * nq + i, 0))],
        out_specs=pl.BlockSpec((tq, R * HEAD), lambda b, g, i: (b * nq + i, g)),
        out_shape=jax.ShapeDtypeStruct((Bn * T, HQ * HEAD), BF16),
        scratch_shapes=[pltpu.VMEM((tq, T), F32), pltpu.VMEM((R * tq, T), F32), pltpu.VMEM((R * tq, T), BF16),
                        pltpu.VMEM((R * tq, 1), F32), pltpu.VMEM((R * tq, HEAD), F32)],
        compiler_params=_params("parallel", "parallel", "arbitrary"),
        name="nsa_prompt",
    )(qn, pooled3, pooled3, ksn, z, kwn, z, zbg)


def _pool_pages_kernel(pt_ref, *refs, pp):
    ck_refs, cv_refs = refs[:pp], refs[pp:2 * pp]
    pw_ref, pk_ref, pv_ref = refs[2 * pp:]
    per = PAGE_SIZE // BLOCK
    for q in range(pp):
        for half in range(per):
            sl = slice(half * BLOCK, (half + 1) * BLOCK)
            pk_ref[0, q * per + half] = jnp.sum(ck_refs[q][0, 0, sl] * pw_ref[0], axis=0)
            pv_ref[0, q * per + half] = jnp.sum(cv_refs[q][0, 0, sl] * pw_ref[1], axis=0)


def pool_pages(cache_k, cache_v, page_table, pool_w3, layer):
    Bs, n_pages = page_table.shape
    G = cache_k.shape[3]
    per = PAGE_SIZE // BLOCK
    pp = _tile(n_pages, (8, 4, 2, 1))
    page = lambda q: pl.BlockSpec((1, 1, PAGE_SIZE, G, HEAD),
                                  lambda b, p, pt: (layer, pt[b * n_pages + p * pp + q], 0, 0, 0))
    out = pl.BlockSpec((1, pp * per, G, HEAD), lambda b, p, pt: (b, p, 0, 0))
    return pl.pallas_call(
        functools.partial(_pool_pages_kernel, pp=pp),
        grid_spec=pltpu.PrefetchScalarGridSpec(
            num_scalar_prefetch=1,
            grid=(Bs, n_pages // pp),
            in_specs=[page(q) for q in range(pp)] * 2
                     + [pl.BlockSpec((2, BLOCK, G, HEAD), lambda b, p, pt: (0, 0, 0, 0))],
            out_specs=[out, out]),
        out_shape=[jax.ShapeDtypeStruct((Bs, n_pages * per, G, HEAD), F32)] * 2,
        compiler_params=_params("parallel", "parallel"),
        name="pool_pages",
    )(page_table.reshape(-1), *([cache_k] * pp), *([cache_v] * pp), pool_w3)


def _head_rows(qrow, g, R):
    rows = [qrow[:, (g * R + r) * HEAD:(g * R + r + 1) * HEAD] for r in range(R)]
    rows.append(jnp.zeros((8 - R, HEAD), F32))
    return jnp.concatenate(rows, axis=0).astype(BF16)


def _sample_cmp_win_kernel(q_ref, pk_ref, pv_ref, cwk_ref, cwv_ref, kwn_ref, vw_ref, bg_ref, g_ref,
                           pc_ref, pw_ref, idx_ref, *, G, R, past):
    nbp = pk_ref.shape[1]
    wb = cwk_ref.shape[2]
    qpos = past
    cur = qpos // BLOCK
    nlane = (cur // HEAD + 1) * HEAD
    qrow = q_ref[0]
    gates = _sigmoid(bg_ref[0])
    qk_g = g_ref[...]
    for g in range(G):
        sl = slice(g * HEAD, (g + 1) * HEAD)
        qg = _head_rows(qrow, g, R)

        kcb = _rms(pk_ref[0, :, g, :], qk_g[1:2]).astype(BF16)
        sc = _dot_nt(qg, kcb)
        vis = (_iota((8, nbp), 1) + 1) * BLOCK <= qpos + 1
        p = jnp.where(vis, _softmax_masked(sc, vis), 0.0)
        o_cmp = _dot(p.astype(BF16), pv_ref[0, :, g, :].astype(BF16))
        imp = p[0:1]
        for r in range(1, R):
            imp = imp + p[r:r + 1]

        imp = jnp.concatenate([imp, jnp.zeros((1, nlane - nbp), F32)], axis=1)
        j = _iota((1, nlane), 1)
        forced = (j == 0) | (j == cur) | (j == cur - 1)
        score = jnp.where(j <= cur, imp + jnp.where(forced, FORCE_BONUS, 0.0), NEG)
        s_col = _column(score, nlane)
        ii = _iota((nlane, nlane), 0)
        jj = _iota((nlane, nlane), 1)
        ahead = (s_col > score) | ((s_col == score) & (ii < jj))
        rank = jnp.sum(ahead.astype(F32), axis=0, keepdims=True)
        rank_col = _column(rank, nlane)
        hit = rank_col == _iota((nlane, HEAD), 1).astype(F32)
        idx = jnp.sum(jnp.where(hit, _iota((nlane, HEAD), 0).astype(F32), 0.0), axis=0, keepdims=True)
        idx_ref[0, g:g + 1, :] = idx.astype(I32)

        kwc = cwk_ref[0, 0, :, g, :].astype(BF16)
        sw = _dot_nt(qg, kwc)
        diff = qpos - (past - wb + _iota((8, wb), 1))
        sw = jnp.where((diff >= 0) & (diff <= WINDOW), sw, NEG)
        kn = kwn_ref[0][:, sl].astype(BF16).astype(F32)
        vn = vw_ref[0][:, sl].astype(BF16).astype(F32)
        sn = jnp.sum(qg.astype(F32) * kn, axis=1, keepdims=True)
        m = jnp.maximum(jnp.max(sw, axis=1, keepdims=True), sn)
        e = jnp.exp(sw - m)
        en = jnp.exp(sn - m)
        den = jnp.sum(e, axis=1, keepdims=True) + en
        o_win = (_dot((e / den).astype(BF16), cwv_ref[0, 0, :, g, :].astype(BF16))
                 + (en / den).astype(BF16).astype(F32) * vn)

        for r in range(R):
            c0 = (g * R + r) * 3
            hs = slice((g * R + r) * HEAD, (g * R + r + 1) * HEAD)
            pc_ref[0, :, hs] = gates[:, c0:c0 + 1] * o_cmp[r:r + 1]
            pw_ref[0, :, hs] = gates[:, c0 + 2:c0 + 3] * o_win[r:r + 1]


def sample_cmp_win(qn3, pk, pv, cache_win_k, cache_win_v, kwn3, z3, zbg3, qk_g, layer, G, R, past, col_vw):
    Bs = qn3.shape[0]
    depth, _, wb = cache_win_k.shape[:3]
    kvw = G * HEAD
    qw = G * R * HEAD
    nbp = pk.shape[1]
    row = lambda w, j=0: pl.BlockSpec((1, 1, w), lambda b: (b, 0, j))
    return pl.pallas_call(
        functools.partial(_sample_cmp_win_kernel, G=G, R=R, past=past),
        grid=(Bs,),
        in_specs=[row(qw),
                  pl.BlockSpec((1, nbp, G, HEAD), lambda b: (b, 0, 0, 0)),
                  pl.BlockSpec((1, nbp, G, HEAD), lambda b: (b, 0, 0, 0)),
                  pl.BlockSpec((1, 1, wb, G, HEAD), lambda b: (layer, b, 0, 0, 0)),
                  pl.BlockSpec((1, 1, wb, G, HEAD), lambda b: (layer, b, 0, 0, 0)),
                  row(kvw), row(kvw, col_vw // kvw), row(HEAD),
                  pl.BlockSpec((4, HEAD), lambda b: (0, 0))],
        out_specs=[row(qw), row(qw), pl.BlockSpec((1, G, HEAD), lambda b: (b, 0, 0))],
        out_shape=[jax.ShapeDtypeStruct((Bs, 1, qw), F32), jax.ShapeDtypeStruct((Bs, 1, qw), F32),
                   jax.ShapeDtypeStruct((Bs, G, HEAD), I32)],
        compiler_params=_params("parallel"),
        name="sample_cmp_win",
    )(qn3, pk, pv, cache_win_k, cache_win_v, kwn3, z3, zbg3, qk_g)


def _sample_select_kernel(idx_ref, pt_ref, q_ref, *refs, G, R, n_top, past):
    ck_refs, cv_refs = refs[:G], refs[G:2 * G]
    kn_ref, vn_ref, bg_ref, pc_ref, pw_ref, o_ref, m_ref, l_ref, acc_ref = refs[2 * G:]
    b, n = pl.program_id(0), pl.program_id(1)
    qpos = past

    @pl.when(n == 0)
    def _():
        m_ref[...] = jnp.full_like(m_ref, NEG)
        l_ref[...] = jnp.zeros_like(l_ref)
        acc_ref[...] = jnp.zeros_like(acc_ref)

    qrow = q_ref[0]
    for g in range(G):
        sl = slice(g * HEAD, (g + 1) * HEAD)
        j = idx_ref[(b * G + g) * n_top + n]
        qg = _head_rows(qrow, g, R)
        s = _dot_nt(qg, ck_refs[g][0, 0, :, g, :].astype(BF16))
        pos = j * BLOCK + _iota((8, BLOCK), 1)
        s = jnp.where((pos < past) & (pos <= qpos), s, NEG)
        kn = kn_ref[0][:, sl].astype(BF16).astype(F32)
        vn = vn_ref[0][:, sl].astype(BF16).astype(F32)
        sn = jnp.sum(qg.astype(F32) * kn, axis=1, keepdims=True)
        sn = jnp.where(j == past // BLOCK, sn, NEG)

        m_old = m_ref[g][:, 0:1]
        m_new = jnp.maximum(m_old, jnp.maximum(jnp.max(s, axis=1, keepdims=True), sn))
        alpha = jnp.exp(m_old - m_new)
        e = jnp.exp(s - m_new)
        en = jnp.exp(sn - m_new)
        l_new = alpha * l_ref[g][:, 0:1] + jnp.sum(e, axis=1, keepdims=True) + en
        acc = (alpha * acc_ref[g] + _dot(e.astype(BF16), cv_refs[g][0, 0, :, g, :].astype(BF16))
               + en.astype(BF16).astype(F32) * vn)
        m_ref[g] = jnp.broadcast_to(m_new, (8, HEAD))
        l_ref[g] = jnp.broadcast_to(l_new, (8, HEAD))
        acc_ref[g] = acc

    @pl.when(n == n_top - 1)
    def _():
        gates = _sigmoid(bg_ref[0])
        for g in range(G):
            o_sel = acc_ref[g] / l_ref[g][:, 0:1]
            for r in range(R):
                c = (g * R + r) * 3 + 1
                hs = slice((g * R + r) * HEAD, (g * R + r + 1) * HEAD)
                o_ref[0, :, hs] = ((pc_ref[0, :, hs] + gates[:, c:c + 1] * o_sel[r:r + 1])
                                   + pw_ref[0, :, hs]).astype(o_ref.dtype)


def sample_select(idx, page_table, qn3, cache_k, cache_v, ksn3, z3, zbg3, pc, pw, layer, G, R, past, col_vs):
    Bs, n_pages = page_table.shape
    n_top = idx.shape[-1]
    depth, n_pool = cache_k.shape[:2]
    kvw, qw = G * HEAD, G * R * HEAD
    per = PAGE_SIZE // BLOCK
    ck = cache_k.reshape(depth, n_pool * per, BLOCK, G, HEAD)
    cv = cache_v.reshape(depth, n_pool * per, BLOCK, G, HEAD)
    nbp = past // BLOCK

    def blk(g):
        def index(b, n, idx_ref, pt_ref):
            jc = jnp.minimum(idx_ref[(b * G + g) * n_top + n], nbp - 1)
            return (layer, pt_ref[b * n_pages + jc // per] * per + jc % per, 0, 0, 0)
        return pl.BlockSpec((1, 1, BLOCK, G, HEAD), index)

    row = lambda w, j=0: pl.BlockSpec((1, 1, w), lambda b, n, i_, p_: (b, 0, j))
    return pl.pallas_call(
        functools.partial(_sample_select_kernel, G=G, R=R, n_top=n_top, past=past),
        grid_spec=pltpu.PrefetchScalarGridSpec(
            num_scalar_prefetch=2,
            grid=(Bs, n_top),
            in_specs=[row(qw)] + [blk(g) for g in range(G)] * 2
                     + [row(kvw), row(kvw, col_vs // kvw), row(HEAD), row(qw), row(qw)],
            out_specs=row(qw),
            scratch_shapes=[pltpu.VMEM((G, 8, HEAD), F32)] * 3),
        out_shape=jax.ShapeDtypeStruct((Bs, 1, qw), BF16),
        compiler_params=_params("parallel", "arbitrary"),
        name="sample_select",
    )(idx.reshape(-1), page_table.reshape(-1), qn3, *([ck] * G), *([cv] * G), ksn3, z3, zbg3, pc, pw)


def _cast_kernel(x_ref, o_ref):
    o_ref[...] = x_ref[0].astype(o_ref.dtype)


def cast_layer_bf16(w, layer, col_block=None):
    _, R, C = w.shape
    cb = 0
    if col_block is not None:
        C, cb = HEAD, col_block
    tr = next((t for t in (512, 256, 128, 64, 32, 16) if R % t == 0 and t * C * 4 <= CAST_BLOCK_BYTES), R)
    return pl.pallas_call(
        _cast_kernel,
        grid=(R // tr,),
        in_specs=[pl.BlockSpec((1, tr, C), lambda i: (layer, i, cb))],
        out_specs=pl.BlockSpec((tr, C), lambda i: (i, 0)),
        out_shape=jax.ShapeDtypeStruct((R, C), BF16),
        compiler_params=_params("parallel"),
        name="cast_layer_bf16",
    )(w)


def _cast_cut_kernel(a_ref, b_ref, o_ref, *, nj0, cut):
    j = pl.program_id(1)
    tn = o_ref.shape[1]

    @pl.when(j < nj0)
    def _():
        o_ref[...] = a_ref[0].astype(o_ref.dtype)

    @pl.when(j >= nj0)
    def _():
        x = jnp.concatenate([a_ref[0], b_ref[0]], axis=1)
        o_ref[...] = x[:, cut:cut + tn].astype(o_ref.dtype)


def cast_cut_bf16(w, layer, c0, cut):
    _, R, C = w.shape
    N = C - cut
    tn = next(t for t in (1024, 512, 256, 128) if c0 % t == 0 and N % t == 0)
    assert cut < HEAD
    tr = _tile(R, (512, 256, 128, 64, 32, 16))
    nj0 = c0 // tn
    return pl.pallas_call(
        functools.partial(_cast_cut_kernel, nj0=nj0, cut=cut),
        grid=(R // tr, N // tn),
        in_specs=[pl.BlockSpec((1, tr, tn), lambda i, j: (layer, i, j)),
                  pl.BlockSpec((1, tr, HEAD), lambda i, j: (layer, i, (j + 1) * (tn // HEAD)))],
        out_specs=pl.BlockSpec((tr, tn), lambda i, j: (i, j)),
        out_shape=jax.ShapeDtypeStruct((R, N), BF16),
        compiler_params=_params("parallel", "parallel"),
        name="cast_cut_bf16",
    )(w, w)


def _layer_weights(l, norm_g, w_in, qk_g, cmp_pool, w_pa, w_pb, w_o, f1_gu, f1_dn, f2_gu, f2_dn, H, HQ, G):
    kw, qw, kvw = H * HEAD, HQ * HEAD, G * HEAD
    D = w_in.shape[1]
    c_bg = 4 * kw + qw + 6 * kvw
    n_bg = 3 * HQ
    w_main = cast_cut_bf16(w_in, l, c_bg, n_bg)
    w_bg = cast_layer_bf16(w_in, l, col_block=c_bg // HEAD)
    pool_w2 = jnp.repeat(jnp.transpose(cmp_pool[l], (1, 0, 2)).reshape(BLOCK, 2 * G), HEAD, axis=1)
    pool_w3 = jnp.broadcast_to(cmp_pool[l][..., None], cmp_pool.shape[1:] + (HEAD,))
    cast = lambda w: cast_layer_bf16(w, l)
    return dict(norm_g=norm_g[l], w_main=w_main, w_bg=w_bg, qk_g=qk_g[l], pool_w2=pool_w2, pool_w3=pool_w3,
                w_pa=cast(w_pa), w_pb=cast(w_pb), w_o=cast(w_o),
                ffn1=(f1_gu, cast(f1_dn)), ffn2=(f2_gu, cast(f2_dn)), c_bg=c_bg, D=D, l=l)


def _ffn(xp, xs, g, ffn_w, l):
    gu, dn = ffn_w
    hp, hs = matmul_swiglu(rmsnorm_rows(xp, g), rmsnorm_rows(xs, g), gu, l)
    return matmul_panel(hp, hs, dn, xp, xs, 0.5)


def _layer(xp, xs, lw, prompt_mixers, sample_mixers):
    hp, hs = _ffn(xp, xs, lw["norm_g"][0], lw["ffn1"], lw["l"])
    out, states = [], []
    for h, mixers in ((hp, prompt_mixers), (hs, sample_mixers)):
        u = rmsnorm_rows(h, lw["norm_g"][1])
        z = matmul(u, lw["w_main"])
        zbg = matmul(u, lw["w_bg"])
        ya, yb, state = mixers(z, zbg)
        m = merge_project(ya, yb, lw["w_pa"], lw["w_pb"], z, lw["c_bg"], lw["c_bg"] + lw["D"])
        out.append(matmul(m, lw["w_o"], res=h, scale=1.0))
        states.append(state)
    yp, ys = _ffn(out[0], out[1], lw["norm_g"][2], lw["ffn2"], lw["l"])
    return yp, ys, states[0], states[1]


def kernel(x_prompt, x_sample, state_hgrn, cache_cmp_k, cache_cmp_v, cache_sel_k, cache_sel_v, cache_win_k, cache_win_v, page_table, norm_g, w_in, hgrn_lb, hgrn_out_g, qk_g, cmp_pool, w_proj_a, w_proj_b, w_out, ffn1_gu, ffn1_down, ffn2_gu, ffn2_down):
    Bn, T, D = x_prompt.shape
    Bs, Tn, _ = x_sample.shape
    depth = w_in.shape[0]
    H = hgrn_lb.shape[1] // HEAD
    HQ = w_proj_b.shape[1] // HEAD
    G = cache_cmp_k.shape[3]
    R = HQ // G
    kw, qw, kvw = H * HEAD, HQ * HEAD, G * HEAD
    past = page_table.shape[1] * PAGE_SIZE
    wb = cache_win_k.shape[2]
    assert Tn == 1 and T % HGRN_CHUNK == 0 and past % PAGE_SIZE == 0 and R <= 8
    assert hgrn_out_g.shape[1] == HEAD and w_proj_a.shape[1] == kw and wb <= WINDOW and wb <= past
    col_q = 4 * kw
    col_kv = col_q + qw

    p = jax.nn.softmax(hgrn_lb.astype(F32), axis=0)
    lbs = jnp.cumsum(p, axis=0) - p[0]

    y_p = x_prompt.reshape(Bn * T, D)
    y_s = x_sample.reshape(Bs, D)
    st_p, st_s = [], []
    for l in range(depth):
        lw = _layer_weights(l, norm_g, w_in, qk_g, cmp_pool, w_proj_a, w_proj_b, w_out,
                            ffn1_gu, ffn1_down, ffn2_gu, ffn2_down, H, HQ, G)

        def prompt_mixers(z, zbg):
            ya, s_new = hgrn_prompt(z, lbs[l], hgrn_out_g[l], Bn, T, H)
            qn, ksn, kwn, pooled = nsa_prep(z, lw["pool_w2"], lw["qk_g"], HQ, G, col_q, True)
            yb = nsa_prompt(qn, pooled, ksn, kwn, z, zbg, Bn, T, HQ, G, col_kv)
            kv = lambda i: z[:, col_kv + i * kvw:col_kv + (i + 1) * kvw].reshape(Bn, T, G, HEAD)
            tail = lambda a: a[:, T - min(WINDOW, T):]
            return ya, yb, (s_new, kv(0), kv(1), ksn.reshape(Bn, T, G, HEAD), kv(3),
                            tail(kwn.reshape(Bn, T, G, HEAD)), tail(kv(5)))

        def sample_mixers(z, zbg):
            z3, zbg3 = z.reshape(Bs, 1, -1), zbg.reshape(Bs, 1, HEAD)
            ya, s_new = hgrn_sample(z3, lbs[l], hgrn_out_g[l], state_hgrn, l, H)
            qn, ksn, kwn = nsa_prep(z, lw["pool_w2"], lw["qk_g"], HQ, G, col_q, False)
            qn3, ksn3, kwn3 = qn.reshape(Bs, 1, qw), ksn.reshape(Bs, 1, kvw), kwn.reshape(Bs, 1, kvw)
            pk, pv = pool_pages(cache_cmp_k, cache_cmp_v, page_table, lw["pool_w3"], l)
            pc, pw, idx = sample_cmp_win(qn3, pk, pv, cache_win_k, cache_win_v, kwn3, z3, zbg3, lw["qk_g"],
                                         l, G, R, past, col_kv + 5 * kvw)
            n_top = min(TOP_N, -(-(past + Tn) // BLOCK))
            yb = sample_select(idx[:, :, :n_top], page_table, qn3, cache_sel_k, cache_sel_v, ksn3, z3, zbg3,
                               pc, pw, l, G, R, past, col_kv + 3 * kvw)
            kv = lambda i: z[:, col_kv + i * kvw:col_kv + (i + 1) * kvw].reshape(Bs, 1, G, HEAD)
            win_k = jnp.concatenate([cache_win_k[l], kwn.reshape(Bs, 1, G, HEAD)], axis=1)[:, -wb:]
            win_v = jnp.concatenate([cache_win_v[l], kv(5)], axis=1)[:, -wb:]
            return ya.reshape(Bs, kw), yb.reshape(Bs, qw), (
                s_new, kv(0), kv(1), ksn.reshape(Bs, 1, G, HEAD), kv(3), win_k, win_v)

        y_p, y_s, sp, ss = _layer(y_p, y_s, lw, prompt_mixers, sample_mixers)
        st_p.append(sp)
        st_s.append(ss)

    stack = lambda states, i: jnp.stack([s[i] for s in states])
    return ((y_p.reshape(Bn, T, D), y_s.reshape(Bs, Tn, D))
            + tuple(stack(st_p, i) for i in range(7)) + tuple(stack(st_s, i) for i in range(7)))
```

```python
import functools

import jax
import jax.numpy as jnp
from jax import lax
from jax.experimental import pallas as pl
from jax.experimental.pallas import tpu as pltpu

F32 = jnp.float32
BF16 = jnp.bfloat16
I32 = jnp.int32

HEAD = 128
BLOCK = 64
TOP_N = 16
WINDOW = 512
PAGE_SIZE = 128
EPS = 1e-6
NEG = -1e30
LB_FLOOR = 1e-30
FORCE_BONUS = 1e4
ATTN_SCALE = HEAD ** -0.5
HGRN_CHUNK = 128
DIAG = 4
STRIP = 16
CAST_BLOCK_BYTES = 8 * 1024 * 1024
VMEM_BYTES_V7X = 64 * 1024 * 1024
VMEM_LIMIT = VMEM_BYTES_V7X - 8 * 1024 * 1024


def _params(*sem):
    return pltpu.CompilerParams(dimension_semantics=sem, vmem_limit_bytes=VMEM_LIMIT)


def _tile(dim, candidates):
    for c in candidates:
        if dim % c == 0:
            return c
    return dim


def _dot(a, b):
    return jnp.dot(a, b, preferred_element_type=F32)


def _dot_nt(a, b):
    return lax.dot_general(a, b, (((1,), (1,)), ((), ())), preferred_element_type=F32)


def _dot_tn(a, b):
    return lax.dot_general(a, b, (((0,), (0,)), ((), ())), preferred_element_type=F32)


def _sigmoid(x):
    return jax.nn.sigmoid(x)


def _silu(x):
    return x * jax.nn.sigmoid(x)


def _log_sigmoid(x):
    return jnp.minimum(x, 0.0) - jnp.log1p(jnp.exp(-jnp.abs(x)))


def _logaddexp(a, b):
    return jnp.maximum(a, b) + jnp.log1p(jnp.exp(-jnp.abs(a - b)))


def _rms(x, g):
    ms = jnp.mean(x * x, axis=-1, keepdims=True)
    return x * lax.rsqrt(ms + EPS) * g


def _softmax_masked(s, mask):
    s = jnp.where(mask, s, NEG)
    m = jnp.max(s, axis=-1, keepdims=True)
    e = jnp.exp(s - m)
    return e / jnp.sum(e, axis=-1, keepdims=True)


def _iota(shape, dim):
    return lax.broadcasted_iota(I32, shape, dim)


def _rmsnorm_kernel(x_ref, g_ref, o_ref):
    o_ref[...] = _rms(x_ref[...], g_ref[...]).astype(o_ref.dtype)


def rmsnorm_rows(x, g):
    M, D = x.shape
    tm = _tile(M, (256, 128, 64, 32, 16, 8))
    return pl.pallas_call(
        _rmsnorm_kernel,
        grid=(M // tm,),
        in_specs=[pl.BlockSpec((tm, D), lambda i: (i, 0)), pl.BlockSpec((1, D), lambda i: (0, 0))],
        out_specs=pl.BlockSpec((tm, D), lambda i: (i, 0)),
        out_shape=jax.ShapeDtypeStruct((M, D), BF16),
        compiler_params=_params("parallel"),
        name="rmsnorm_rows",
    )(x, g.reshape(1, D))


def _mm_kernel(x_ref, w_ref, *rest, nk, scale, has_res):
    if has_res:
        r_ref, o_ref = rest[0], rest[1]
        scratch = rest[2:]
    else:
        r_ref, o_ref = None, rest[0]
        scratch = rest[1:]

    def finish(acc):
        if has_res:
            o_ref[...] = (r_ref[...] + scale * acc).astype(o_ref.dtype)
        else:
            o_ref[...] = acc.astype(o_ref.dtype)

    if nk == 1:
        finish(_dot(x_ref[...], w_ref[...]))
    else:
        acc_ref = scratch[0]
        k = pl.program_id(2)

        @pl.when(k == 0)
        def _():
            acc_ref[...] = jnp.zeros_like(acc_ref)

        acc_ref[...] += _dot(x_ref[...], w_ref[...])

        @pl.when(k == nk - 1)
        def _():
            finish(acc_ref[...])


def matmul(x, w, *, out_dtype=F32, res=None, scale=1.0, tm=None, tn=None, tk=None):
    M, K = x.shape
    N = w.shape[1]
    tm = tm or _tile(M, (1024, 512, 256, 128, 64, 32, 16, 8))
    if K <= 4096:
        tk = tk or K
        tn = tn or _tile(N, (512, 256, 128) if res is not None else (1024, 512, 256, 128))
    else:
        tk = tk or next((K // d for d in (2, 3, 4, 5, 6, 8) if K % (d * HEAD) == 0), HEAD)
        tn = tn or _tile(N, (512, 256, 128))
    nk = K // tk
    has_res = res is not None
    in_specs = [pl.BlockSpec((tm, tk), lambda i, j, k: (i, k)), pl.BlockSpec((tk, tn), lambda i, j, k: (k, j))]
    args = [x, w]
    if has_res:
        in_specs.append(pl.BlockSpec((tm, tn), lambda i, j, k: (i, j)))
        args.append(res)
    return pl.pallas_call(
        functools.partial(_mm_kernel, nk=nk, scale=scale, has_res=has_res),
        grid=(M // tm, N // tn, nk),
        in_specs=in_specs,
        out_specs=pl.BlockSpec((tm, tn), lambda i, j, k: (i, j)),
        out_shape=jax.ShapeDtypeStruct((M, N), out_dtype),
        scratch_shapes=[pltpu.VMEM((tm, tn), F32)] if nk > 1 else [],
        compiler_params=_params("parallel", "parallel", "arbitrary"),
        name="matmul",
    )(*args)


def _swiglu_kernel(x_ref, xs_ref, wg_ref, wu_ref, o_ref, os_ref):
    wg = wg_ref[0].astype(BF16)
    wu = wu_ref[0].astype(BF16)
    for src, dst in ((x_ref, o_ref), (xs_ref, os_ref)):
        x = src[...]
        dst[...] = (_silu(_dot(x, wg)) * _dot(x, wu)).astype(dst.dtype)


def matmul_swiglu(x, xs, w_gu, layer):
    M, K = x.shape
    Ms = xs.shape[0]
    F = w_gu.shape[2] // 2
    tn = _tile(F, (256, 128))
    tm = _tile(M, (2048, 1024, 512, 256, 128, 64, 32, 16, 8))
    nj = F // tn
    return pl.pallas_call(
        _swiglu_kernel,
        grid=(M // tm, nj),
        in_specs=[pl.BlockSpec((tm, K), lambda i, j: (i, 0), pipeline_mode=pl.Buffered(1)),
                  pl.BlockSpec((Ms, K), lambda i, j: (0, 0)),
                  pl.BlockSpec((1, K, tn), lambda i, j: (layer, 0, j)),
                  pl.BlockSpec((1, K, tn), lambda i, j: (layer, 0, j + nj))],
        out_specs=[pl.BlockSpec((tm, tn), lambda i, j: (i, j)), pl.BlockSpec((Ms, tn), lambda i, j: (0, j))],
        out_shape=[jax.ShapeDtypeStruct((M, F), BF16), jax.ShapeDtypeStruct((Ms, F), BF16)],
        compiler_params=_params("arbitrary", "arbitrary"),
        name="matmul_swiglu",
    )(x, xs, w_gu, w_gu)


def _merge_kernel(ya_ref, yb_ref, wa_ref, wb_ref, ga_ref, gb_ref, o_ref):
    pa = _dot(ya_ref[...], wa_ref[...])
    pb = _dot(yb_ref[...], wb_ref[...])
    o_ref[...] = (_sigmoid(ga_ref[...]) * pa + _sigmoid(gb_ref[...]) * pb).astype(o_ref.dtype)


def merge_project(ya, yb, wa, wb, z, col_a, col_b):
    M, Ka = ya.shape
    Kb = yb.shape[1]
    N = wa.shape[1]
    tm = _tile(M, (1024, 512, 256, 128, 64, 32, 16, 8))
    tn = _tile(N, (512, 256, 128))
    ja, jb = col_a // tn, col_b // tn
    return pl.pallas_call(
        _merge_kernel,
        grid=(M // tm, N // tn),
        in_specs=[pl.BlockSpec((tm, Ka), lambda i, j: (i, 0)),
                  pl.BlockSpec((tm, Kb), lambda i, j: (i, 0)),
                  pl.BlockSpec((Ka, tn), lambda i, j: (0, j)),
                  pl.BlockSpec((Kb, tn), lambda i, j: (0, j)),
                  pl.BlockSpec((tm, tn), lambda i, j: (i, j + ja)),
                  pl.BlockSpec((tm, tn), lambda i, j: (i, j + jb))],
        out_specs=pl.BlockSpec((tm, tn), lambda i, j: (i, j)),
        out_shape=jax.ShapeDtypeStruct((M, N), BF16),
        compiler_params=_params("parallel", "parallel"),
        name="merge_project",
    )(ya, yb, wa, wb, z, z)


def _split3(x):
    hi = x.astype(BF16)
    r1 = x - hi.astype(F32)
    mid = r1.astype(BF16)
    lo = (r1 - mid.astype(F32)).astype(BF16)
    return hi, mid, lo


def _hgrn_gates(zq, zf, lb):
    q = _silu(zq)
    log_f = _logaddexp(jnp.log(jnp.maximum(lb, LB_FLOOR)), jnp.log1p(-lb) + _log_sigmoid(zf))
    k = (1.0 - lb) * _sigmoid(-zf)
    return q, log_f, k


def _hgrn_chunk(zq, zf, v, zg, lb, og, st):
    C = HGRN_CHUNK
    q, log_f, k = _hgrn_gates(zq, zf, lb)
    vb = v.astype(BF16)
    row = _iota((C, C), 0)
    col = _iota((C, C), 1)

    tri = jnp.where(col <= row, 1.0, 0.0).astype(BF16)
    hi, mid, lo = _split3(log_f)
    b = _dot(tri, hi) + _dot(tri, mid) + _dot(tri, lo)

    a = jnp.zeros((C, C), F32)
    h = C // 2
    while h >= DIAG:
        P = 2 * h
        bref = jnp.concatenate(
            [jnp.broadcast_to(b[m * P + h - 1:m * P + h, :], (P, HEAD)) for m in range(C // P)], axis=0)
        second = (row & (P - 1)) >= h
        qt = q * jnp.exp(jnp.where(second, b - bref, NEG))
        kt = k * jnp.exp(jnp.where(second, NEG, bref - b))
        lvl = _dot_nt(qt.astype(BF16), kt.astype(BF16))
        if P < C:
            lvl = jnp.where((row // P) == (col // P), lvl, 0.0)
        a = a + lvl
        h //= 2

    rloc = row & (DIAG - 1)
    for d in range(DIAG):
        if d == 0:
            p = q * k
        else:
            bs = pltpu.roll(b, d, 0)
            ks = pltpu.roll(k, d, 0)
            p = q * jnp.exp(jnp.where(rloc >= d, b - bs, NEG)) * ks
        a = jnp.where((col == row - d) & (rloc >= d), jnp.sum(p, axis=1, keepdims=True), a)

    o = _dot_nt((q * jnp.exp(b)).astype(BF16), st.astype(BF16)) + _dot(a.astype(BF16), vb)
    b_last = b[C - 1:C, :]
    kd = k * jnp.exp(b_last - b)
    st_new = st * jnp.exp(b_last) + _dot_tn(vb, kd.astype(BF16))
    return _rms(o, og) * _silu(zg), st_new


def _hgrn_prompt_kernel(q_ref, f_ref, v_ref, g_ref, lb_ref, og_ref, y_ref, s_ref, st_ref, *, nc, hb):
    c = pl.program_id(2)

    @pl.when(c == 0)
    def _():
        st_ref[...] = jnp.zeros_like(st_ref)

    for i in range(hb):
        sl = slice(i * HEAD, (i + 1) * HEAD)
        y, st_new = _hgrn_chunk(q_ref[:, sl], f_ref[:, sl], v_ref[:, sl], g_ref[:, sl], lb_ref[:, sl],
                                og_ref[...], st_ref[i])
        st_ref[i] = st_new
        y_ref[:, sl] = y.astype(y_ref.dtype)

        @pl.when(c == nc - 1)
        def _():
            s_ref[0, i] = st_new.T


def hgrn_prompt(z, lb, out_g, Bn, T, H):
    C = HGRN_CHUNK
    nc = T // C
    hb = _tile(H, (4, 2, 1))
    w = hb * HEAD
    blk = lambda off: pl.BlockSpec((C, w), lambda b, h, c: (b * nc + c, off + h))
    return pl.pallas_call(
        functools.partial(_hgrn_prompt_kernel, nc=nc, hb=hb),
        grid=(Bn, H // hb, nc),
        in_specs=[blk(0), blk(H // hb), blk(2 * H // hb), blk(3 * H // hb),
                  pl.BlockSpec((1, w), lambda b, h, c: (0, h)),
                  pl.BlockSpec((1, HEAD), lambda b, h, c: (0, 0))],
        out_specs=[pl.BlockSpec((C, w), lambda b, h, c: (b * nc + c, h)),
                   pl.BlockSpec((1, hb, HEAD, HEAD), lambda b, h, c: (b, h, 0, 0))],
        out_shape=[jax.ShapeDtypeStruct((Bn * T, H * HEAD), BF16),
                   jax.ShapeDtypeStruct((Bn, H, HEAD, HEAD), F32)],
        scratch_shapes=[pltpu.VMEM((hb, HEAD, HEAD), F32)],
        compiler_params=_params("parallel", "parallel", "arbitrary"),
        name="hgrn_prompt",
    )(z, z, z, z, lb.reshape(1, H * HEAD), out_g.reshape(1, HEAD))


def _column(rowvec, n):
    eye = _iota((n, n), 0) == _iota((n, n), 1)
    return jnp.sum(jnp.where(eye, jnp.broadcast_to(rowvec, (n, n)), 0.0), axis=1, keepdims=True)


def _hgrn_sample_kernel(q_ref, f_ref, v_ref, g_ref, lb_ref, og_ref, s0_ref, y_ref, s_ref, *, H):
    for h in range(H):
        sl = slice(h * HEAD, (h + 1) * HEAD)
        q, log_f, k = _hgrn_gates(q_ref[0][:, sl], f_ref[0][:, sl], lb_ref[:, sl])
        v = v_ref[0][:, sl]
        s_new = _column(jnp.exp(log_f), HEAD) * s0_ref[0, 0, h] + _column(k, HEAD) * v
        o = jnp.sum(_column(q, HEAD) * s_new, axis=0, keepdims=True)
        y_ref[0, :, sl] = (_rms(o, og_ref[...]) * _silu(g_ref[0][:, sl])).astype(y_ref.dtype)
        s_ref[0, h] = s_new


def hgrn_sample(z3, lb, out_g, state, layer, H):
    Bs = z3.shape[0]
    w = H * HEAD
    blk = lambda off: pl.BlockSpec((1, 1, w), lambda b: (b, 0, off))
    return pl.pallas_call(
        functools.partial(_hgrn_sample_kernel, H=H),
        grid=(Bs,),
        in_specs=[blk(0), blk(1), blk(2), blk(3),
                  pl.BlockSpec((1, w), lambda b: (0, 0)),
                  pl.BlockSpec((1, HEAD), lambda b: (0, 0)),
                  pl.BlockSpec((1, 1, H, HEAD, HEAD), lambda b: (layer, b, 0, 0, 0))],
        out_specs=[pl.BlockSpec((1, 1, w), lambda b: (b, 0, 0)),
                   pl.BlockSpec((1, H, HEAD, HEAD), lambda b: (b, 0, 0, 0))],
        out_shape=[jax.ShapeDtypeStruct((Bs, 1, w), BF16),
                   jax.ShapeDtypeStruct((Bs, H, HEAD, HEAD), F32)],
        compiler_params=_params("parallel"),
        name="hgrn_sample",
    )(z3, z3, z3, z3, lb.reshape(1, w), out_g.reshape(1, HEAD), state)


def _nsa_prep_kernel(qb_ref, kv_ref, ks_ref, kw_ref, pw_ref, g_ref, qn_ref, ksn_ref, kwn_ref, *pool_refs,
                     HQ, G, pool):
    g = g_ref[...]
    for h in range(HQ):
        sl = slice(h * HEAD, (h + 1) * HEAD)
        qn_ref[:, sl] = (_rms(qb_ref[:, sl], g[0:1]) * ATTN_SCALE).astype(qn_ref.dtype)
    for j in range(G):
        sl = slice(j * HEAD, (j + 1) * HEAD)
        ksn_ref[:, sl] = _rms(ks_ref[:, sl], g[2:3])
        kwn_ref[:, sl] = _rms(kw_ref[:, sl], g[3:4])
    if pool:
        pool_ref = pool_refs[0]
        kvw = G * HEAD
        w = pw_ref[...]
        for n in range(kv_ref.shape[0] // BLOCK):
            blk = jnp.sum(kv_ref[n * BLOCK:(n + 1) * BLOCK, :] * w, axis=0, keepdims=True)
            for j in range(G):
                sl = slice(j * HEAD, (j + 1) * HEAD)
                pool_ref[n:n + 1, sl] = _rms(blk[:, sl], g[1:2])
            pool_ref[n:n + 1, kvw:] = blk[:, kvw:]


def nsa_prep(z, pool_w2, qk_g, HQ, G, col_q, pool):
    M = z.shape[0]
    qw, kvw = HQ * HEAD, G * HEAD
    tm = _tile(M, (512, 256, 128, 64, 8))
    col_kv = col_q + qw
    out_shape = [jax.ShapeDtypeStruct((M, qw), BF16 if pool else F32),
                 jax.ShapeDtypeStruct((M, kvw), F32), jax.ShapeDtypeStruct((M, kvw), F32)]
    out_specs = [pl.BlockSpec((tm, qw), lambda i: (i, 0)),
                 pl.BlockSpec((tm, kvw), lambda i: (i, 0)), pl.BlockSpec((tm, kvw), lambda i: (i, 0))]
    if pool:
        out_shape.append(jax.ShapeDtypeStruct((M // BLOCK, 2 * kvw), F32))
        out_specs.append(pl.BlockSpec((tm // BLOCK, 2 * kvw), lambda i: (i, 0)))
    return pl.pallas_call(
        functools.partial(_nsa_prep_kernel, HQ=HQ, G=G, pool=pool),
        grid=(M // tm,),
        in_specs=[pl.BlockSpec((tm, qw), lambda i: (i, col_q // qw)),
                  pl.BlockSpec((tm, 2 * kvw), lambda i: (i, col_kv // (2 * kvw))),
                  pl.BlockSpec((tm, kvw), lambda i: (i, col_kv // kvw + 2)),
                  pl.BlockSpec((tm, kvw), lambda i: (i, col_kv // kvw + 4)),
                  pl.BlockSpec((BLOCK, 2 * kvw), lambda i: (0, 0)),
                  pl.BlockSpec((4, HEAD), lambda i: (0, 0))],
        out_specs=out_specs,
        out_shape=out_shape,
        compiler_params=_params("parallel"),
        name="nsa_prep",
    )(z, z, z, z, pool_w2, qk_g)


def _gate_col(gates, c):
    return jnp.sum(jnp.where(_iota(gates.shape, 1) == c, gates, 0.0), axis=1, keepdims=True)


def _top_blocks(imp, cur, n_top):
    nb = imp.shape[1]
    j = _iota(imp.shape, 1)
    forced = (j == 0) | (j == cur) | (j == cur - 1)
    causal = j <= cur
    score = jnp.where(causal, imp + jnp.where(forced, FORCE_BONUS, 0.0), NEG)
    rank = jnp.zeros(imp.shape, I32)
    for i in range(nb):
        si = score[:, i:i + 1]
        rank = rank + ((si > score) | ((si == score) & (i < j))).astype(I32)
    return (rank < n_top) & causal


def _nsa_prompt_kernel(q_ref, pool_ref, vpool_ref, ks_ref, vs_ref, kw_ref, vw_ref, bg_ref, o_ref,
                       bias_ref, s_ref, e_ref, l_ref, osel_ref, *, tq, T, R, kc):
    g = pl.program_id(1)
    t0 = pl.program_id(2) * tq
    rows = R * tq
    nb = T // BLOCK
    q = jnp.concatenate([q_ref[:, r * HEAD:(r + 1) * HEAD] for r in range(R)], axis=0)

    kcb = pool_ref[0].astype(BF16)
    vcb = vpool_ref[0].astype(BF16)
    sc = _dot_nt(q, kcb)
    tpos = t0 + _iota((rows, nb), 0) % tq
    vis = (_iota((rows, nb), 1) + 1) * BLOCK <= tpos + 1
    p = jnp.where(vis, _softmax_masked(sc, vis), 0.0)
    o_cmp = _dot(p.astype(BF16), vcb)
    imp = p[0:tq]
    for r in range(1, R):
        imp = imp + p[r * tq:(r + 1) * tq]

    cur = (t0 + _iota((tq, nb), 0)) // BLOCK
    sel = jnp.where(_top_blocks(imp, cur, min(TOP_N, nb)), 1.0, 0.0).astype(BF16)

    def attend(k, v, W):
        s_ref[:, :W] = _dot_nt(q, k)
        for r in range(R):
            for i in range(tq // STRIP):
                bs = slice(i * STRIP, (i + 1) * STRIP)
                rs = slice(r * tq + i * STRIP, r * tq + (i + 1) * STRIP)
                sr = s_ref[rs, :W] + bias_ref[bs, :W]
                e = jnp.exp(sr - jnp.max(sr, axis=1, keepdims=True))
                l_ref[rs, :] = jnp.sum(e, axis=1, keepdims=True)
                e_ref[rs, :W] = e.astype(BF16)
        return _dot(e_ref[:, :W], v) / l_ref[...]

    for c in range(T // kc):
        @pl.when((t0 + tq - 1) // kc == c)
        def _():
            W = (c + 1) * kc
            expand = jnp.where(_iota((nb, W), 1) // BLOCK == _iota((nb, W), 0), 1.0, 0.0).astype(BF16)
            allowed = (_dot(sel, expand) > 0.5) & (_iota((tq, W), 1) <= t0 + _iota((tq, W), 0))
            bias_ref[:, :W] = jnp.where(allowed, 0.0, NEG)
            osel_ref[...] = attend(ks_ref[:W, :].astype(BF16), vs_ref[:W, :].astype(BF16), W)

    wlen = min(T, WINDOW + tq)
    start = pl.multiple_of(jnp.clip(t0 - WINDOW, 0, T - wlen), HEAD)
    diff = (t0 + _iota((tq, wlen), 0)) - (start + _iota((tq, wlen), 1))
    bias_ref[:, :wlen] = jnp.where((diff >= 0) & (diff <= WINDOW), 0.0, NEG)
    o_win = attend(kw_ref[pl.ds(start, wlen), :].astype(BF16), vw_ref[pl.ds(start, wlen), :].astype(BF16), wlen)

    gates = _sigmoid(bg_ref[...])
    for r in range(R):
        c0 = (g * R + r) * 3
        sl = slice(r * tq, (r + 1) * tq)
        y = (_gate_col(gates, c0) * o_cmp[sl] + _gate_col(gates, c0 + 1) * osel_ref[sl, :]
             + _gate_col(gates, c0 + 2) * o_win[sl])
        o_ref[:, r * HEAD:(r + 1) * HEAD] = y.astype(o_ref.dtype)


def nsa_prompt(qn, pooled, ksn, kwn, z, zbg, Bn, T, HQ, G, col_kv):
    R = HQ // G
    tq = _tile(T, (128, 64))
    nq = T // tq
    nb = T // BLOCK
    kvw = G * HEAD
    pooled3 = pooled.reshape(Bn, nb, 2 * kvw)
    cv = col_kv // HEAD
    kc = _tile(T, (512, 256, 128, 64))
    seq = lambda off: pl.BlockSpec((T, HEAD), lambda b, g, i: (b, off + g))
    return pl.pallas_call(
        functools.partial(_nsa_prompt_kernel, tq=tq, T=T, R=R, kc=kc),
        grid=(Bn, G, nq),
        in_specs=[pl.BlockSpec((tq, R * HEAD), lambda b, g, i: (b * nq + i, g)),
                  pl.BlockSpec((1, nb, HEAD), lambda b, g, i: (b, 0, g)),
                  pl.BlockSpec((1, nb, HEAD), lambda b, g, i: (b, 0, G + g)),
                  seq(0), seq(cv + 3 * G), seq(0), seq(cv + 5 * G),
                  pl.BlockSpec((tq, HEAD), lambda b, g, i: (b * nq + i, 0))],
        out_specs=pl.BlockSpec((tq, R * HEAD), lambda b, g, i: (b * nq + i, g)),
        out_shape=jax.ShapeDtypeStruct((Bn * T, HQ * HEAD), BF16),
        scratch_shapes=[pltpu.VMEM((tq, T), F32), pltpu.VMEM((R * tq, T), F32), pltpu.VMEM((R * tq, T), BF16),
                        pltpu.VMEM((R * tq, 1), F32), pltpu.VMEM((R * tq, HEAD), F32)],
        compiler_params=_params("parallel", "parallel", "arbitrary"),
        name="nsa_prompt",
    )(qn, pooled3, pooled3, ksn, z, kwn, z, zbg)


def _pool_pages_kernel(pt_ref, *refs, pp):
    ck_refs, cv_refs = refs[:pp], refs[pp:2 * pp]
    pw_ref, pk_ref, pv_ref = refs[2 * pp:]
    per = PAGE_SIZE // BLOCK
    for q in range(pp):
        for half in range(per):
            sl = slice(half * BLOCK, (half + 1) * BLOCK)
            pk_ref[0, q * per + half] = jnp.sum(ck_refs[q][0, 0, sl] * pw_ref[0], axis=0)
            pv_ref[0, q * per + half] = jnp.sum(cv_refs[q][0, 0, sl] * pw_ref[1], axis=0)


def pool_pages(cache_k, cache_v, page_table, pool_w3, layer):
    Bs, n_pages = page_table.shape
    G = cache_k.shape[3]
    per = PAGE_SIZE // BLOCK
    pp = _tile(n_pages, (8, 4, 2, 1))
    page = lambda q: pl.BlockSpec((1, 1, PAGE_SIZE, G, HEAD),
                                  lambda b, p, pt: (layer, pt[b * n_pages + p * pp + q], 0, 0, 0))
    out = pl.BlockSpec((1, pp * per, G, HEAD), lambda b, p, pt: (b, p, 0, 0))
    return pl.pallas_call(
        functools.partial(_pool_pages_kernel, pp=pp),
        grid_spec=pltpu.PrefetchScalarGridSpec(
            num_scalar_prefetch=1,
            grid=(Bs, n_pages // pp),
            in_specs=[page(q) for q in range(pp)] * 2
                     + [pl.BlockSpec((2, BLOCK, G, HEAD), lambda b, p, pt: (0, 0, 0, 0))],
            out_specs=[out, out]),
        out_shape=[jax.ShapeDtypeStruct((Bs, n_pages * per, G, HEAD), F32)] * 2,
        compiler_params=_params("parallel", "parallel"),
        name="pool_pages",
    )(page_table.reshape(-1), *([cache_k] * pp), *([cache_v] * pp), pool_w3)


def _head_rows(qrow, g, R):
    rows = [qrow[:, (g * R + r) * HEAD:(g * R + r + 1) * HEAD] for r in range(R)]
    rows.append(jnp.zeros((8 - R, HEAD), F32))
    return jnp.concatenate(rows, axis=0).astype(BF16)


def _sample_cmp_win_kernel(q_ref, pk_ref, pv_ref, cwk_ref, cwv_ref, kwn_ref, vw_ref, bg_ref, g_ref,
                           pc_ref, pw_ref, idx_ref, *, G, R, past):
    nbp = pk_ref.shape[1]
    wb = cwk_ref.shape[2]
    qpos = past
    cur = qpos // BLOCK
    nlane = (cur // HEAD + 1) * HEAD
    qrow = q_ref[0]
    gates = _sigmoid(bg_ref[0])
    qk_g = g_ref[...]
    for g in range(G):
        sl = slice(g * HEAD, (g + 1) * HEAD)
        qg = _head_rows(qrow, g, R)

        kcb = _rms(pk_ref[0, :, g, :], qk_g[1:2]).astype(BF16)
        sc = _dot_nt(qg, kcb)
        vis = (_iota((8, nbp), 1) + 1) * BLOCK <= qpos + 1
        p = jnp.where(vis, _softmax_masked(sc, vis), 0.0)
        o_cmp = _dot(p.astype(BF16), pv_ref[0, :, g, :].astype(BF16))
        imp = p[0:1]
        for r in range(1, R):
            imp = imp + p[r:r + 1]

        imp = jnp.concatenate([imp, jnp.zeros((1, nlane - nbp), F32)], axis=1)
        j = _iota((1, nlane), 1)
        forced = (j == 0) | (j == cur) | (j == cur - 1)
        score = jnp.where(j <= cur, imp + jnp.where(forced, FORCE_BONUS, 0.0), NEG)
        s_col = _column(score, nlane)
        ii = _iota((nlane, nlane), 0)
        jj = _iota((nlane, nlane), 1)
        ahead = (s_col > score) | ((s_col == score) & (ii < jj))
        rank = jnp.sum(ahead.astype(F32), axis=0, keepdims=True)
        rank_col = _column(rank, nlane)
        hit = rank_col == _iota((nlane, HEAD), 1).astype(F32)
        idx = jnp.sum(jnp.where(hit, _iota((nlane, HEAD), 0).astype(F32), 0.0), axis=0, keepdims=True)
        idx_ref[0, g:g + 1, :] = idx.astype(I32)

        kwc = cwk_ref[0, 0, :, g, :].astype(BF16)
        sw = _dot_nt(qg, kwc)
        diff = qpos - (past - wb + _iota((8, wb), 1))
        sw = jnp.where((diff >= 0) & (diff <= WINDOW), sw, NEG)
        kn = kwn_ref[0][:, sl].astype(BF16).astype(F32)
        vn = vw_ref[0][:, sl].astype(BF16).astype(F32)
        sn = jnp.sum(qg.astype(F32) * kn, axis=1, keepdims=True)
        m = jnp.maximum(jnp.max(sw, axis=1, keepdims=True), sn)
        e = jnp.exp(sw - m)
        en = jnp.exp(sn - m)
        den = jnp.sum(e, axis=1, keepdims=True) + en
        o_win = (_dot((e / den).astype(BF16), cwv_ref[0, 0, :, g, :].astype(BF16))
                 + (en / den).astype(BF16).astype(F32) * vn)

        for r in range(R):
            c0 = (g * R + r) * 3
            hs = slice((g * R + r) * HEAD, (g * R + r + 1) * HEAD)
            pc_ref[0, :, hs] = gates[:, c0:c0 + 1] * o_cmp[r:r + 1]
            pw_ref[0, :, hs] = gates[:, c0 + 2:c0 + 3] * o_win[r:r + 1]


def sample_cmp_win(qn3, pk, pv, cache_win_k, cache_win_v, kwn3, z3, zbg3, qk_g, layer, G, R, past, col_vw):
    Bs = qn3.shape[0]
    depth, _, wb = cache_win_k.shape[:3]
    kvw = G * HEAD
    qw = G * R * HEAD
    nbp = pk.shape[1]
    row = lambda w, j=0: pl.BlockSpec((1, 1, w), lambda b: (b, 0, j))
    return pl.pallas_call(
        functools.partial(_sample_cmp_win_kernel, G=G, R=R, past=past),
        grid=(Bs,),
        in_specs=[row(qw),
                  pl.BlockSpec((1, nbp, G, HEAD), lambda b: (b, 0, 0, 0)),
                  pl.BlockSpec((1, nbp, G, HEAD), lambda b: (b, 0, 0, 0)),
                  pl.BlockSpec((1, 1, wb, G, HEAD), lambda b: (layer, b, 0, 0, 0)),
                  pl.BlockSpec((1, 1, wb, G, HEAD), lambda b: (layer, b, 0, 0, 0)),
                  row(kvw), row(kvw, col_vw // kvw), row(HEAD),
                  pl.BlockSpec((4, HEAD), lambda b: (0, 0))],
        out_specs=[row(qw), row(qw), pl.BlockSpec((1, G, HEAD), lambda b: (b, 0, 0))],
        out_shape=[jax.ShapeDtypeStruct((Bs, 1, qw), F32), jax.ShapeDtypeStruct((Bs, 1, qw), F32),
                   jax.ShapeDtypeStruct((Bs, G, HEAD), I32)],
        compiler_params=_params("parallel"),
        name="sample_cmp_win",
    )(qn3, pk, pv, cache_win_k, cache_win_v, kwn3, z3, zbg3, qk_g)


def _sample_select_kernel(idx_ref, pt_ref, q_ref, *refs, G, R, n_top, past):
    ck_refs, cv_refs = refs[:G], refs[G:2 * G]
    kn_ref, vn_ref, bg_ref, pc_ref, pw_ref, o_ref, m_ref, l_ref, acc_ref = refs[2 * G:]
    b, n = pl.program_id(0), pl.program_id(1)
    qpos = past

    @pl.when(n == 0)
    def _():
        m_ref[...] = jnp.full_like(m_ref, NEG)
        l_ref[...] = jnp.zeros_like(l_ref)
        acc_ref[...] = jnp.zeros_like(acc_ref)

    qrow = q_ref[0]
    for g in range(G):
        sl = slice(g * HEAD, (g + 1) * HEAD)
        j = idx_ref[(b * G + g) * n_top + n]
        qg = _head_rows(qrow, g, R)
        s = _dot_nt(qg, ck_refs[g][0, 0, :, g, :].astype(BF16))
        pos = j * BLOCK + _iota((8, BLOCK), 1)
        s = jnp.where((pos < past) & (pos <= qpos), s, NEG)
        kn = kn_ref[0][:, sl].astype(BF16).astype(F32)
        vn = vn_ref[0][:, sl].astype(BF16).astype(F32)
        sn = jnp.sum(qg.astype(F32) * kn, axis=1, keepdims=True)
        sn = jnp.where(j == past // BLOCK, sn, NEG)

        m_old = m_ref[g][:, 0:1]
        m_new = jnp.maximum(m_old, jnp.maximum(jnp.max(s, axis=1, keepdims=True), sn))
        alpha = jnp.exp(m_old - m_new)
        e = jnp.exp(s - m_new)
        en = jnp.exp(sn - m_new)
        l_new = alpha * l_ref[g][:, 0:1] + jnp.sum(e, axis=1, keepdims=True) + en
        acc = (alpha * acc_ref[g] + _dot(e.astype(BF16), cv_refs[g][0, 0, :, g, :].astype(BF16))
               + en.astype(BF16).astype(F32) * vn)
        m_ref[g] = jnp.broadcast_to(m_new, (8, HEAD))
        l_ref[g] = jnp.broadcast_to(l_new, (8, HEAD))
        acc_ref[g] = acc

    @pl.when(n == n_top - 1)
    def _():
        gates = _sigmoid(bg_ref[0])
        for g in range(G):
            o_sel = acc_ref[g] / l_ref[g][:, 0:1]
            for r in range(R):
                c = (g * R + r) * 3 + 1
                hs = slice((g * R + r) * HEAD, (g * R + r + 1) * HEAD)
                o_ref[0, :, hs] = ((pc_ref[0, :, hs] + gates[:, c:c + 1] * o_sel[r:r + 1])
                                   + pw_ref[0, :, hs]).astype(o_ref.dtype)


def sample_select(idx, page_table, qn3, cache_k, cache_v, ksn3, z3, zbg3, pc, pw, layer, G, R, past, col_vs):
    Bs, n_pages = page_table.shape
    n_top = idx.shape[-1]
    depth, n_pool = cache_k.shape[:2]
    kvw, qw = G * HEAD, G * R * HEAD
    per = PAGE_SIZE // BLOCK
    ck = cache_k.reshape(depth, n_pool * per, BLOCK, G, HEAD)
    cv = cache_v.reshape(depth, n_pool * per, BLOCK, G, HEAD)
    nbp = past // BLOCK

    def blk(g):
        def index(b, n, idx_ref, pt_ref):
            jc = jnp.minimum(idx_ref[(b * G + g) * n_top + n], nbp - 1)
            return (layer, pt_ref[b * n_pages + jc // per] * per + jc % per, 0, 0, 0)
        return pl.BlockSpec((1, 1, BLOCK, G, HEAD), index)

    row = lambda w, j=0: pl.BlockSpec((1, 1, w), lambda b, n, i_, p_: (b, 0, j))
    return pl.pallas_call(
        functools.partial(_sample_select_kernel, G=G, R=R, n_top=n_top, past=past),
        grid_spec=pltpu.PrefetchScalarGridSpec(
            num_scalar_prefetch=2,
            grid=(Bs, n_top),
            in_specs=[row(qw)] + [blk(g) for g in range(G)] * 2
                     + [row(kvw), row(kvw, col_vs // kvw), row(HEAD), row(qw), row(qw)],
            out_specs=row(qw),
            scratch_shapes=[pltpu.VMEM((G, 8, HEAD), F32)] * 3),
        out_shape=jax.ShapeDtypeStruct((Bs, 1, qw), BF16),
        compiler_params=_params("parallel", "arbitrary"),
        name="sample_select",
    )(idx.reshape(-1), page_table.reshape(-1), qn3, *([ck] * G), *([cv] * G), ksn3, z3, zbg3, pc, pw)


def _cast_kernel(x_ref, o_ref):
    o_ref[...] = x_ref[0].astype(o_ref.dtype)


def cast_layer_bf16(w, layer):
    _, R, C = w.shape
    tr = next((t for t in (512, 256, 128, 64, 32, 16) if R % t == 0 and t * C * 4 <= CAST_BLOCK_BYTES), R)
    return pl.pallas_call(
        _cast_kernel,
        grid=(R // tr,),
        in_specs=[pl.BlockSpec((1, tr, C), lambda i: (layer, i, 0))],
        out_specs=pl.BlockSpec((tr, C), lambda i: (i, 0)),
        out_shape=jax.ShapeDtypeStruct((R, C), BF16),
        compiler_params=_params("parallel"),
        name="cast_layer_bf16",
    )(w)


def _cast_cut_kernel(a_ref, b_ref, o_ref, *, nj0, cut):
    j = pl.program_id(0)
    tn = o_ref.shape[1]

    @pl.when(j < nj0)
    def _():
        o_ref[...] = a_ref[0].T.astype(o_ref.dtype)

    @pl.when(j >= nj0)
    def _():
        x = jnp.concatenate([a_ref[0], b_ref[0]], axis=0)
        o_ref[...] = x[cut:cut + tn].T.astype(o_ref.dtype)


def cast_cut_bf16(w_t, layer, c0, cut):
    _, C, R = w_t.shape
    N = C - cut
    tn = next(t for t in (256, 128) if c0 % t == 0 and N % t == 0)
    assert cut < HEAD and cut % 8 == 0
    return pl.pallas_call(
        functools.partial(_cast_cut_kernel, nj0=c0 // tn, cut=cut),
        grid=(N // tn,),
        in_specs=[pl.BlockSpec((1, tn, R), lambda j: (layer, j, 0)),
                  pl.BlockSpec((1, HEAD, R), lambda j: (layer, (j + 1) * (tn // HEAD), 0))],
        out_specs=pl.BlockSpec((R, tn), lambda j: (0, j)),
        out_shape=jax.ShapeDtypeStruct((R, N), BF16),
        compiler_params=_params("parallel"),
        name="cast_cut_bf16",
    )(w_t, w_t)


def _cast_t_kernel(x_ref, o_ref):
    o_ref[...] = x_ref[0].T.astype(o_ref.dtype)


def cast_cols_bf16(w_t, layer, col_block):
    R = w_t.shape[2]
    return pl.pallas_call(
        _cast_t_kernel,
        grid=(1,),
        in_specs=[pl.BlockSpec((1, HEAD, R), lambda i: (layer, col_block, 0))],
        out_specs=pl.BlockSpec((R, HEAD), lambda i: (0, 0)),
        out_shape=jax.ShapeDtypeStruct((R, HEAD), BF16),
        compiler_params=_params("arbitrary"),
        name="cast_cols_bf16",
    )(w_t)


def _layer_weights(l, norm_g, w_in, qk_g, cmp_pool, w_pa, w_pb, w_o, f1_gu, f1_dn, f2_gu, f2_dn, H, HQ, G):
    kw, qw, kvw = H * HEAD, HQ * HEAD, G * HEAD
    D = w_in.shape[1]
    c_bg = 4 * kw + qw + 6 * kvw
    n_bg = 3 * HQ
    w_in_t = jnp.transpose(w_in, (0, 2, 1))
    w_main = cast_cut_bf16(w_in_t, l, c_bg, n_bg)
    w_bg = cast_cols_bf16(w_in_t, l, c_bg // HEAD)
    pool_w2 = jnp.repeat(jnp.transpose(cmp_pool[l], (1, 0, 2)).reshape(BLOCK, 2 * G), HEAD, axis=1)
    pool_w3 = jnp.broadcast_to(cmp_pool[l][..., None], cmp_pool.shape[1:] + (HEAD,))
    cast = lambda w: cast_layer_bf16(w, l)
    return dict(norm_g=norm_g[l], w_main=w_main, w_bg=w_bg, qk_g=qk_g[l], pool_w2=pool_w2, pool_w3=pool_w3,
                w_pa=cast(w_pa), w_pb=cast(w_pb), w_o=cast(w_o),
                ffn1=(f1_gu, cast(f1_dn)), ffn2=(f2_gu, cast(f2_dn)), c_bg=c_bg, D=D, l=l)


def _ffn(xp, xs, g, ffn_w, l):
    gu, dn = ffn_w
    hp, hs = matmul_swiglu(rmsnorm_rows(xp, g), rmsnorm_rows(xs, g), gu, l)
    return matmul(hp, dn, res=xp, scale=0.5), matmul(hs, dn, res=xs, scale=0.5)


def _layer(xp, xs, lw, prompt_mixers, sample_mixers):
    hp, hs = _ffn(xp, xs, lw["norm_g"][0], lw["ffn1"], lw["l"])
    out, states = [], []
    for h, mixers in ((hp, prompt_mixers), (hs, sample_mixers)):
        u = rmsnorm_rows(h, lw["norm_g"][1])
        z = matmul(u, lw["w_main"])
        zbg = matmul(u, lw["w_bg"])
        ya, yb, state = mixers(z, zbg)
        m = merge_project(ya, yb, lw["w_pa"], lw["w_pb"], z, lw["c_bg"], lw["c_bg"] + lw["D"])
        out.append(matmul(m, lw["w_o"], res=h, scale=1.0))
        states.append(state)
    yp, ys = _ffn(out[0], out[1], lw["norm_g"][2], lw["ffn2"], lw["l"])
    return yp, ys, states[0], states[1]


def kernel(x_prompt, x_sample, state_hgrn, cache_cmp_k, cache_cmp_v, cache_sel_k, cache_sel_v, cache_win_k, cache_win_v, page_table, norm_g, w_in, hgrn_lb, hgrn_out_g, qk_g, cmp_pool, w_proj_a, w_proj_b, w_out, ffn1_gu, ffn1_down, ffn2_gu, ffn2_down):
    Bn, T, D = x_prompt.shape
    Bs, Tn, _ = x_sample.shape
    depth = w_in.shape[0]
    H = hgrn_lb.shape[1] // HEAD
    HQ = w_proj_b.shape[1] // HEAD
    G = cache_cmp_k.shape[3]
    R = HQ // G
    kw, qw, kvw = H * HEAD, HQ * HEAD, G * HEAD
    past = page_table.shape[1] * PAGE_SIZE
    wb = cache_win_k.shape[2]
    assert Tn == 1 and T % HGRN_CHUNK == 0 and past % PAGE_SIZE == 0 and R <= 8
    assert hgrn_out_g.shape[1] == HEAD and w_proj_a.shape[1] == kw and wb <= WINDOW and wb <= past
    col_q = 4 * kw
    col_kv = col_q + qw

    p = jax.nn.softmax(hgrn_lb.astype(F32), axis=0)
    lbs = jnp.cumsum(p, axis=0) - p[0]

    y_p = x_prompt.reshape(Bn * T, D)
    y_s = x_sample.reshape(Bs, D)
    st_p, st_s = [], []
    for l in range(depth):
        lw = _layer_weights(l, norm_g, w_in, qk_g, cmp_pool, w_proj_a, w_proj_b, w_out,
                            ffn1_gu, ffn1_down, ffn2_gu, ffn2_down, H, HQ, G)

        def prompt_mixers(z, zbg):
            ya, s_new = hgrn_prompt(z, lbs[l], hgrn_out_g[l], Bn, T, H)
            qn, ksn, kwn, pooled = nsa_prep(z, lw["pool_w2"], lw["qk_g"], HQ, G, col_q, True)
            yb = nsa_prompt(qn, pooled, ksn, kwn, z, zbg, Bn, T, HQ, G, col_kv)
            kv = lambda i: z[:, col_kv + i * kvw:col_kv + (i + 1) * kvw].reshape(Bn, T, G, HEAD)
            tail = lambda a: a[:, T - min(WINDOW, T):]
            return ya, yb, (s_new, kv(0), kv(1), ksn.reshape(Bn, T, G, HEAD), kv(3),
                            tail(kwn.reshape(Bn, T, G, HEAD)), tail(kv(5)))

        def sample_mixers(z, zbg):
            z3, zbg3 = z.reshape(Bs, 1, -1), zbg.reshape(Bs, 1, HEAD)
            ya, s_new = hgrn_sample(z3, lbs[l], hgrn_out_g[l], state_hgrn, l, H)
            qn, ksn, kwn = nsa_prep(z, lw["pool_w2"], lw["qk_g"], HQ, G, col_q, False)
            qn3, ksn3, kwn3 = qn.reshape(Bs, 1, qw), ksn.reshape(Bs, 1, kvw), kwn.reshape(Bs, 1, kvw)
            pk, pv = pool_pages(cache_cmp_k, cache_cmp_v, page_table, lw["pool_w3"], l)
            pc, pw, idx = sample_cmp_win(qn3, pk, pv, cache_win_k, cache_win_v, kwn3, z3, zbg3, lw["qk_g"],
                                         l, G, R, past, col_kv + 5 * kvw)
            n_top = min(TOP_N, -(-(past + Tn) // BLOCK))
            yb = sample_select(idx[:, :, :n_top], page_table, qn3, cache_sel_k, cache_sel_v, ksn3, z3, zbg3,
                               pc, pw, l, G, R, past, col_kv + 3 * kvw)
            kv = lambda i: z[:, col_kv + i * kvw:col_kv + (i + 1) * kvw].reshape(Bs, 1, G, HEAD)
            win_k = jnp.concatenate([cache_win_k[l], kwn.reshape(Bs, 1, G, HEAD)], axis=1)[:, -wb:]
            win_v = jnp.concatenate([cache_win_v[l], kv(5)], axis=1)[:, -wb:]
            return ya.reshape(Bs, kw), yb.reshape(Bs, qw), (
                s_new, kv(0), kv(1), ksn.reshape(Bs, 1, G, HEAD), kv(3), win_k, win_v)

        y_p, y_s, sp, ss = _layer(y_p, y_s, lw, prompt_mixers, sample_mixers)
        st_p.append(sp)
        st_s.append(ss)

    stack = lambda states, i: jnp.stack([s[i] for s in states])
    return ((y_p.reshape(Bn, T, D), y_s.reshape(Bs, Tn, D))
            + tuple(stack(st_p, i) for i in range(7)) + tuple(stack(st_s, i) for i in range(7)))
```

```python
import functools

import jax
import jax.numpy as jnp
from jax import lax
from jax.experimental import pallas as pl
from jax.experimental.pallas import tpu as pltpu

F32 = jnp.float32
BF16 = jnp.bfloat16
I32 = jnp.int32

HEAD = 128
BLOCK = 64
TOP_N = 16
WINDOW = 512
PAGE_SIZE = 128
EPS = 1e-6
NEG = -1e30
LB_FLOOR = 1e-30
FORCE_BONUS = 1e4
ATTN_SCALE = HEAD ** -0.5
HGRN_CHUNK = 128
DIAG = 4
STRIP = 16
CAST_BLOCK_BYTES = 8 * 1024 * 1024
VMEM_BYTES_V7X = 64 * 1024 * 1024
VMEM_LIMIT = VMEM_BYTES_V7X - 8 * 1024 * 1024


def _params(*sem):
    return pltpu.CompilerParams(dimension_semantics=sem, vmem_limit_bytes=VMEM_LIMIT)


def _tile(dim, candidates):
    for c in candidates:
        if dim % c == 0:
            return c
    return dim


def _dot(a, b):
    return jnp.dot(a, b, preferred_element_type=F32)


def _dot_nt(a, b):
    return lax.dot_general(a, b, (((1,), (1,)), ((), ())), preferred_element_type=F32)


def _dot_tn(a, b):
    return lax.dot_general(a, b, (((0,), (0,)), ((), ())), preferred_element_type=F32)


def _sigmoid(x):
    return jax.nn.sigmoid(x)


def _silu(x):
    return x * jax.nn.sigmoid(x)


def _log_sigmoid(x):
    return jnp.minimum(x, 0.0) - jnp.log1p(jnp.exp(-jnp.abs(x)))


def _logaddexp(a, b):
    return jnp.maximum(a, b) + jnp.log1p(jnp.exp(-jnp.abs(a - b)))


def _rms(x, g):
    ms = jnp.mean(x * x, axis=-1, keepdims=True)
    return x * lax.rsqrt(ms + EPS) * g


def _softmax_masked(s, mask):
    s = jnp.where(mask, s, NEG)
    m = jnp.max(s, axis=-1, keepdims=True)
    e = jnp.exp(s - m)
    return e / jnp.sum(e, axis=-1, keepdims=True)


def _iota(shape, dim):
    return lax.broadcasted_iota(I32, shape, dim)


def _rmsnorm_kernel(x_ref, g_ref, o_ref):
    o_ref[...] = _rms(x_ref[...], g_ref[...]).astype(o_ref.dtype)


def rmsnorm_rows(x, g):
    M, D = x.shape
    tm = _tile(M, (256, 128, 64, 32, 16, 8))
    return pl.pallas_call(
        _rmsnorm_kernel,
        grid=(M // tm,),
        in_specs=[pl.BlockSpec((tm, D), lambda i: (i, 0)), pl.BlockSpec((1, D), lambda i: (0, 0))],
        out_specs=pl.BlockSpec((tm, D), lambda i: (i, 0)),
        out_shape=jax.ShapeDtypeStruct((M, D), BF16),
        compiler_params=_params("parallel"),
        name="rmsnorm_rows",
    )(x, g.reshape(1, D))


def _mm_kernel(x_ref, w_ref, *rest, nk, scale, has_res):
    if has_res:
        r_ref, o_ref = rest[0], rest[1]
        scratch = rest[2:]
    else:
        r_ref, o_ref = None, rest[0]
        scratch = rest[1:]

    def finish(acc):
        if has_res:
            o_ref[...] = (r_ref[...] + scale * acc).astype(o_ref.dtype)
        else:
            o_ref[...] = acc.astype(o_ref.dtype)

    if nk == 1:
        finish(_dot(x_ref[...], w_ref[...]))
    else:
        acc_ref = scratch[0]
        k = pl.program_id(2)

        @pl.when(k == 0)
        def _():
            acc_ref[...] = jnp.zeros_like(acc_ref)

        acc_ref[...] += _dot(x_ref[...], w_ref[...])

        @pl.when(k == nk - 1)
        def _():
            finish(acc_ref[...])


def matmul(x, w, *, res=None, scale=1.0):
    M, K = x.shape
    N = w.shape[1]
    has_res = res is not None
    tm = _tile(M, (1024, 512, 256, 128, 64, 32, 16, 8))
    if K <= 4096:
        tk = K
        tn = _tile(N, (512, 256, 128) if has_res else (1024, 512, 256, 128))
    else:
        tk = next((K // d for d in (2, 3, 4, 5, 6, 8) if K % (d * HEAD) == 0), HEAD)
        tn = _tile(N, (512, 256, 128))
    nk = K // tk
    in_specs = [pl.BlockSpec((tm, tk), lambda i, j, k: (i, k)), pl.BlockSpec((tk, tn), lambda i, j, k: (k, j))]
    args = [x, w]
    if has_res:
        in_specs.append(pl.BlockSpec((tm, tn), lambda i, j, k: (i, j)))
        args.append(res)
    return pl.pallas_call(
        functools.partial(_mm_kernel, nk=nk, scale=scale, has_res=has_res),
        grid=(M // tm, N // tn, nk),
        in_specs=in_specs,
        out_specs=pl.BlockSpec((tm, tn), lambda i, j, k: (i, j)),
        out_shape=jax.ShapeDtypeStruct((M, N), F32),
        scratch_shapes=[pltpu.VMEM((tm, tn), F32)] if nk > 1 else [],
        compiler_params=_params("parallel", "parallel", "arbitrary"),
        name="matmul",
    )(*args)


def _swiglu_kernel(x_ref, xs_ref, wg_ref, wu_ref, o_ref, os_ref):
    i, j = pl.program_id(0), pl.program_id(1)
    wg = wg_ref[0].astype(BF16)
    wu = wu_ref[0].astype(BF16)
    x = x_ref[...]
    o_ref[...] = (_silu(_dot(x, wg)) * _dot(x, wu)).astype(o_ref.dtype)

    @pl.when(i == 0)
    def _():
        xs = xs_ref[...]
        os_ref[j] = (_silu(_dot(xs, wg)) * _dot(xs, wu)).astype(os_ref.dtype)


def matmul_swiglu(x, xs, w_gu, layer):
    M, K = x.shape
    Ms = xs.shape[0]
    F = w_gu.shape[2] // 2
    tn = _tile(F, (256, 128))
    tm = _tile(M, (2048, 1024, 512, 256, 128, 64, 32, 16, 8))
    nj = F // tn
    o, os = pl.pallas_call(
        _swiglu_kernel,
        grid=(M // tm, nj),
        in_specs=[pl.BlockSpec((tm, K), lambda i, j: (i, 0), pipeline_mode=pl.Buffered(1)),
                  pl.BlockSpec((Ms, K), lambda i, j: (0, 0)),
                  pl.BlockSpec((1, K, tn), lambda i, j: (layer, 0, j)),
                  pl.BlockSpec((1, K, tn), lambda i, j: (layer, 0, j + nj))],
        out_specs=[pl.BlockSpec((tm, tn), lambda i, j: (i, j)),
                   pl.BlockSpec((nj, Ms, tn), lambda i, j: (0, 0, 0))],
        out_shape=[jax.ShapeDtypeStruct((M, F), BF16), jax.ShapeDtypeStruct((nj, Ms, tn), BF16)],
        compiler_params=_params("arbitrary", "arbitrary"),
        name="matmul_swiglu",
    )(x, xs, w_gu, w_gu)
    return o, jnp.transpose(os, (1, 0, 2)).reshape(Ms, F)


def _merge_kernel(ya_ref, yb_ref, wa_ref, wb_ref, ga_ref, gb_ref, o_ref):
    pa = _dot(ya_ref[...], wa_ref[...])
    pb = _dot(yb_ref[...], wb_ref[...])
    o_ref[...] = (_sigmoid(ga_ref[...]) * pa + _sigmoid(gb_ref[...]) * pb).astype(o_ref.dtype)


def merge_project(ya, yb, wa, wb, z, col_a, col_b):
    M, Ka = ya.shape
    Kb = yb.shape[1]
    N = wa.shape[1]
    tm = _tile(M, (1024, 512, 256, 128, 64, 32, 16, 8))
    tn = _tile(N, (512, 256, 128))
    ja, jb = col_a // tn, col_b // tn
    return pl.pallas_call(
        _merge_kernel,
        grid=(M // tm, N // tn),
        in_specs=[pl.BlockSpec((tm, Ka), lambda i, j: (i, 0)),
                  pl.BlockSpec((tm, Kb), lambda i, j: (i, 0)),
                  pl.BlockSpec((Ka, tn), lambda i, j: (0, j)),
                  pl.BlockSpec((Kb, tn), lambda i, j: (0, j)),
                  pl.BlockSpec((tm, tn), lambda i, j: (i, j + ja)),
                  pl.BlockSpec((tm, tn), lambda i, j: (i, j + jb))],
        out_specs=pl.BlockSpec((tm, tn), lambda i, j: (i, j)),
        out_shape=jax.ShapeDtypeStruct((M, N), BF16),
        compiler_params=_params("parallel", "parallel"),
        name="merge_project",
    )(ya, yb, wa, wb, z, z)


def _split3(x):
    hi = x.astype(BF16)
    r1 = x - hi.astype(F32)
    mid = r1.astype(BF16)
    lo = (r1 - mid.astype(F32)).astype(BF16)
    return hi, mid, lo


def _hgrn_gates(zq, zf, lb):
    q = _silu(zq)
    log_f = _logaddexp(jnp.log(jnp.maximum(lb, LB_FLOOR)), jnp.log1p(-lb) + _log_sigmoid(zf))
    k = (1.0 - lb) * _sigmoid(-zf)
    return q, log_f, k


def _hgrn_chunk(zq, zf, v, zg, lb, og, st):
    C = HGRN_CHUNK
    q, log_f, k = _hgrn_gates(zq, zf, lb)
    vb = v.astype(BF16)
    row = _iota((C, C), 0)
    col = _iota((C, C), 1)

    tri = jnp.where(col <= row, 1.0, 0.0).astype(BF16)
    hi, mid, lo = _split3(log_f)
    b = _dot(tri, hi) + _dot(tri, mid) + _dot(tri, lo)

    a = jnp.zeros((C, C), F32)
    h = C // 2
    while h >= DIAG:
        P = 2 * h
        bref = jnp.concatenate(
            [jnp.broadcast_to(b[m * P + h - 1:m * P + h, :], (P, HEAD)) for m in range(C // P)], axis=0)
        second = (row & (P - 1)) >= h
        qt = q * jnp.exp(jnp.where(second, b - bref, NEG))
        kt = k * jnp.exp(jnp.where(second, NEG, bref - b))
        lvl = _dot_nt(qt.astype(BF16), kt.astype(BF16))
        if P < C:
            lvl = jnp.where((row // P) == (col // P), lvl, 0.0)
        a = a + lvl
        h //= 2

    rloc = row & (DIAG - 1)
    for d in range(DIAG):
        if d == 0:
            p = q * k
        else:
            bs = pltpu.roll(b, d, 0)
            ks = pltpu.roll(k, d, 0)
            p = q * jnp.exp(jnp.where(rloc >= d, b - bs, NEG)) * ks
        a = jnp.where((col == row - d) & (rloc >= d), jnp.sum(p, axis=1, keepdims=True), a)

    o = _dot_nt((q * jnp.exp(b)).astype(BF16), st.astype(BF16)) + _dot(a.astype(BF16), vb)
    b_last = b[C - 1:C, :]
    kd = k * jnp.exp(b_last - b)
    st_new = st * jnp.exp(b_last) + _dot_tn(vb, kd.astype(BF16))
    return _rms(o, og) * _silu(zg), st_new


def _hgrn_prompt_kernel(q_ref, f_ref, v_ref, g_ref, lb_ref, og_ref, y_ref, s_ref, st_ref, *, nc, hb):
    c = pl.program_id(2)

    @pl.when(c == 0)
    def _():
        st_ref[...] = jnp.zeros_like(st_ref)

    for i in range(hb):
        sl = slice(i * HEAD, (i + 1) * HEAD)
        y, st_new = _hgrn_chunk(q_ref[:, sl], f_ref[:, sl], v_ref[:, sl], g_ref[:, sl], lb_ref[:, sl],
                                og_ref[...], st_ref[i])
        st_ref[i] = st_new
        y_ref[:, sl] = y.astype(y_ref.dtype)

        @pl.when(c == nc - 1)
        def _():
            s_ref[0, i] = st_new.T


def hgrn_prompt(z, lb, out_g, Bn, T, H):
    C = HGRN_CHUNK
    nc = T // C
    hb = _tile(H, (4, 2, 1))
    w = hb * HEAD
    blk = lambda off: pl.BlockSpec((C, w), lambda b, h, c: (b * nc + c, off + h))
    return pl.pallas_call(
        functools.partial(_hgrn_prompt_kernel, nc=nc, hb=hb),
        grid=(Bn, H // hb, nc),
        in_specs=[blk(0), blk(H // hb), blk(2 * H // hb), blk(3 * H // hb),
                  pl.BlockSpec((1, w), lambda b, h, c: (0, h)),
                  pl.BlockSpec((1, HEAD), lambda b, h, c: (0, 0))],
        out_specs=[pl.BlockSpec((C, w), lambda b, h, c: (b * nc + c, h)),
                   pl.BlockSpec((1, hb, HEAD, HEAD), lambda b, h, c: (b, h, 0, 0))],
        out_shape=[jax.ShapeDtypeStruct((Bn * T, H * HEAD), BF16),
                   jax.ShapeDtypeStruct((Bn, H, HEAD, HEAD), F32)],
        scratch_shapes=[pltpu.VMEM((hb, HEAD, HEAD), F32)],
        compiler_params=_params("parallel", "parallel", "arbitrary"),
        name="hgrn_prompt",
    )(z, z, z, z, lb.reshape(1, H * HEAD), out_g.reshape(1, HEAD))


def _column(rowvec, n):
    eye = _iota((n, n), 0) == _iota((n, n), 1)
    return jnp.sum(jnp.where(eye, jnp.broadcast_to(rowvec, (n, n)), 0.0), axis=1, keepdims=True)


def _hgrn_sample_kernel(q_ref, f_ref, v_ref, g_ref, lb_ref, og_ref, s0_ref, y_ref, s_ref, *, H):
    for h in range(H):
        sl = slice(h * HEAD, (h + 1) * HEAD)
        q, log_f, k = _hgrn_gates(q_ref[0][:, sl], f_ref[0][:, sl], lb_ref[:, sl])
        v = v_ref[0][:, sl]
        s_new = _column(jnp.exp(log_f), HEAD) * s0_ref[0, 0, h] + _column(k, HEAD) * v
        o = jnp.sum(_column(q, HEAD) * s_new, axis=0, keepdims=True)
        y_ref[0, :, sl] = (_rms(o, og_ref[...]) * _silu(g_ref[0][:, sl])).astype(y_ref.dtype)
        s_ref[0, h] = s_new


def hgrn_sample(z3, lb, out_g, state, layer, H):
    Bs = z3.shape[0]
    w = H * HEAD
    blk = lambda off: pl.BlockSpec((1, 1, w), lambda b: (b, 0, off))
    return pl.pallas_call(
        functools.partial(_hgrn_sample_kernel, H=H),
        grid=(Bs,),
        in_specs=[blk(0), blk(1), blk(2), blk(3),
                  pl.BlockSpec((1, w), lambda b: (0, 0)),
                  pl.BlockSpec((1, HEAD), lambda b: (0, 0)),
                  pl.BlockSpec((1, 1, H, HEAD, HEAD), lambda b: (layer, b, 0, 0, 0))],
        out_specs=[pl.BlockSpec((1, 1, w), lambda b: (b, 0, 0)),
                   pl.BlockSpec((1, H, HEAD, HEAD), lambda b: (b, 0, 0, 0))],
        out_shape=[jax.ShapeDtypeStruct((Bs, 1, w), BF16),
                   jax.ShapeDtypeStruct((Bs, H, HEAD, HEAD), F32)],
        compiler_params=_params("parallel"),
        name="hgrn_sample",
    )(z3, z3, z3, z3, lb.reshape(1, w), out_g.reshape(1, HEAD), state)


def _nsa_prep_kernel(qb_ref, kv_ref, ks_ref, kw_ref, pw_ref, g_ref, qn_ref, ksn_ref, kwn_ref, *pool_refs,
                     HQ, G, pool):
    g = g_ref[...]
    for h in range(HQ):
        sl = slice(h * HEAD, (h + 1) * HEAD)
        qn_ref[:, sl] = (_rms(qb_ref[:, sl], g[0:1]) * ATTN_SCALE).astype(qn_ref.dtype)
    for j in range(G):
        sl = slice(j * HEAD, (j + 1) * HEAD)
        ksn_ref[:, sl] = _rms(ks_ref[:, sl], g[2:3])
        kwn_ref[:, sl] = _rms(kw_ref[:, sl], g[3:4])
    if pool:
        pool_ref = pool_refs[0]
        kvw = G * HEAD
        w = pw_ref[...]
        for n in range(kv_ref.shape[0] // BLOCK):
            blk = jnp.sum(kv_ref[n * BLOCK:(n + 1) * BLOCK, :] * w, axis=0, keepdims=True)
            for j in range(G):
                sl = slice(j * HEAD, (j + 1) * HEAD)
                pool_ref[n:n + 1, sl] = _rms(blk[:, sl], g[1:2])
            pool_ref[n:n + 1, kvw:] = blk[:, kvw:]


def nsa_prep(z, pool_w2, qk_g, HQ, G, col_q, pool):
    M = z.shape[0]
    qw, kvw = HQ * HEAD, G * HEAD
    tm = _tile(M, (512, 256, 128, 64, 8))
    col_kv = col_q + qw
    out_shape = [jax.ShapeDtypeStruct((M, qw), BF16 if pool else F32),
                 jax.ShapeDtypeStruct((M, kvw), F32), jax.ShapeDtypeStruct((M, kvw), F32)]
    out_specs = [pl.BlockSpec((tm, qw), lambda i: (i, 0)),
                 pl.BlockSpec((tm, kvw), lambda i: (i, 0)), pl.BlockSpec((tm, kvw), lambda i: (i, 0))]
    if pool:
        out_shape.append(jax.ShapeDtypeStruct((M // BLOCK, 2 * kvw), F32))
        out_specs.append(pl.BlockSpec((tm // BLOCK, 2 * kvw), lambda i: (i, 0)))
    return pl.pallas_call(
        functools.partial(_nsa_prep_kernel, HQ=HQ, G=G, pool=pool),
        grid=(M // tm,),
        in_specs=[pl.BlockSpec((tm, qw), lambda i: (i, col_q // qw)),
                  pl.BlockSpec((tm, 2 * kvw), lambda i: (i, col_kv // (2 * kvw))),
                  pl.BlockSpec((tm, kvw), lambda i: (i, col_kv // kvw + 2)),
                  pl.BlockSpec((tm, kvw), lambda i: (i, col_kv // kvw + 4)),
                  pl.BlockSpec((BLOCK, 2 * kvw), lambda i: (0, 0)),
                  pl.BlockSpec((4, HEAD), lambda i: (0, 0))],
        out_specs=out_specs,
        out_shape=out_shape,
        compiler_params=_params("parallel"),
        name="nsa_prep",
    )(z, z, z, z, pool_w2, qk_g)


def _gate_col(gates, c):
    return jnp.sum(jnp.where(_iota(gates.shape, 1) == c, gates, 0.0), axis=1, keepdims=True)


def _top_blocks(imp, cur, n_top):
    nb = imp.shape[1]
    j = _iota(imp.shape, 1)
    forced = (j == 0) | (j == cur) | (j == cur - 1)
    causal = j <= cur
    score = jnp.where(causal, imp + jnp.where(forced, FORCE_BONUS, 0.0), NEG)
    rank = jnp.zeros(imp.shape, I32)
    for i in range(nb):
        si = score[:, i:i + 1]
        rank = rank + ((si > score) | ((si == score) & (i < j))).astype(I32)
    return (rank < n_top) & causal


def _nsa_prompt_kernel(q_ref, pool_ref, vpool_ref, ks_ref, vs_ref, kw_ref, vw_ref, bg_ref, o_ref,
                       sbias_ref, ss_ref, se_ref, sl_ref, wbias_ref, ws_ref, we_ref, wl_ref, *, tq, T, R, kc):
    g = pl.program_id(1)
    t0 = pl.program_id(2) * tq
    rows = R * tq
    nb = T // BLOCK
    q = jnp.concatenate([q_ref[:, r * HEAD:(r + 1) * HEAD] for r in range(R)], axis=0)

    kcb = pool_ref[0].astype(BF16)
    vcb = vpool_ref[0].astype(BF16)
    sc = _dot_nt(q, kcb)
    tpos = t0 + _iota((rows, nb), 0) % tq
    vis = (_iota((rows, nb), 1) + 1) * BLOCK <= tpos + 1
    p = jnp.where(vis, _softmax_masked(sc, vis), 0.0)
    o_cmp = _dot(p.astype(BF16), vcb)
    imp = p[0:tq]
    for r in range(1, R):
        imp = imp + p[r * tq:(r + 1) * tq]

    cur = (t0 + _iota((tq, nb), 0)) // BLOCK
    sel = jnp.where(_top_blocks(imp, cur, min(TOP_N, nb)), 1.0, 0.0).astype(BF16)

    def softmax_strips(s_ref, bias_ref, e_ref, l_ref, W):
        for r in range(R):
            for i in range(tq // STRIP):
                bs = slice(i * STRIP, (i + 1) * STRIP)
                rs = slice(r * tq + i * STRIP, r * tq + (i + 1) * STRIP)
                sr = s_ref[rs, :W] + bias_ref[bs, :W]
                e = jnp.exp(sr - jnp.max(sr, axis=1, keepdims=True))
                l_ref[rs, :] = jnp.sum(e, axis=1, keepdims=True)
                e_ref[rs, :W] = e.astype(BF16)

    wlen = min(T, WINDOW + tq)
    start = pl.multiple_of(jnp.clip(t0 - WINDOW, 0, T - wlen), HEAD)
    gates = _sigmoid(bg_ref[...])

    for c in range(T // kc):
        @pl.when((t0 + tq - 1) // kc == c)
        def _():
            W = (c + 1) * kc
            expand = jnp.where(_iota((nb, W), 1) // BLOCK == _iota((nb, W), 0), 1.0, 0.0).astype(BF16)
            allowed = (_dot(sel, expand) > 0.5) & (_iota((tq, W), 1) <= t0 + _iota((tq, W), 0))
            sbias_ref[:, :W] = jnp.where(allowed, 0.0, NEG)
            diff = (t0 + _iota((tq, wlen), 0)) - (start + _iota((tq, wlen), 1))
            wbias_ref[...] = jnp.where((diff >= 0) & (diff <= WINDOW), 0.0, NEG)

            ss_ref[:, :W] = _dot_nt(q, ks_ref[:W, :].astype(BF16))
            ws_ref[...] = _dot_nt(q, kw_ref[pl.ds(start, wlen), :].astype(BF16))
            softmax_strips(ss_ref, sbias_ref, se_ref, sl_ref, W)
            softmax_strips(ws_ref, wbias_ref, we_ref, wl_ref, wlen)
            o_sel = _dot(se_ref[:, :W], vs_ref[:W, :].astype(BF16)) / sl_ref[...]
            o_win = _dot(we_ref[...], vw_ref[pl.ds(start, wlen), :].astype(BF16)) / wl_ref[...]

            for r in range(R):
                c0 = (g * R + r) * 3
                sl = slice(r * tq, (r + 1) * tq)
                y = (_gate_col(gates, c0) * o_cmp[sl] + _gate_col(gates, c0 + 1) * o_sel[sl]
                     + _gate_col(gates, c0 + 2) * o_win[sl])
                o_ref[:, r * HEAD:(r + 1) * HEAD] = y.astype(o_ref.dtype)


def nsa_prompt(qn, pooled, ksn, kwn, z, zbg, Bn, T, HQ, G, col_kv):
    R = HQ // G
    tq = _tile(T, (128, 64))
    nq = T // tq
    nb = T // BLOCK
    kvw = G * HEAD
    pooled3 = pooled.reshape(Bn, nb, 2 * kvw)
    cv = col_kv // HEAD
    kc = _tile(T, (512, 256, 128, 64))
    seq = lambda off: pl.BlockSpec((T, HEAD), lambda b, g, i: (b, off + g))

    def branch_scratch(w):
        return [pltpu.VMEM((tq, w), F32), pltpu.VMEM((R * tq, w), F32), pltpu.VMEM((R * tq, w), BF16),
                pltpu.VMEM((R * tq, 1), F32)]

    return pl.pallas_call(
        functools.partial(_nsa_prompt_kernel, tq=tq, T=T, R=R, kc=kc),
        grid=(Bn, G, nq),
        in_specs=[pl.BlockSpec((tq, R * HEAD), lambda b, g, i: (b * nq + i, g)),
                  pl.BlockSpec((1, nb, HEAD), lambda b, g, i: (b, 0, g)),
                  pl.BlockSpec((1, nb, HEAD), lambda b, g, i: (b, 0, G + g)),
                  seq(0), seq(cv + 3 * G), seq(0), seq(cv + 5 * G),
                  pl.BlockSpec((tq, HEAD), lambda b, g, i: (b * nq + i, 0))],
        out_specs=pl.BlockSpec((tq, R * HEAD), lambda b, g, i: (b * nq + i, g)),
        out_shape=jax.ShapeDtypeStruct((Bn * T, HQ * HEAD), BF16),
        scratch_shapes=branch_scratch(T) + branch_scratch(min(T, WINDOW + tq)),
        compiler_params=_params("parallel", "parallel", "arbitrary"),
        name="nsa_prompt",
    )(qn, pooled3, pooled3, ksn, z, kwn, z, zbg)


def _pool_pages_kernel(pt_ref, *refs, pp):
    ck_refs, cv_refs = refs[:pp], refs[pp:2 * pp]
    pw_ref, pk_ref, pv_ref = refs[2 * pp:]
    per = PAGE_SIZE // BLOCK
    for q in range(pp):
        for half in range(per):
            sl = slice(half * BLOCK, (half + 1) * BLOCK)
            pk_ref[0, q * per + half] = jnp.sum(ck_refs[q][0, 0, sl] * pw_ref[0], axis=0)
            pv_ref[0, q * per + half] = jnp.sum(cv_refs[q][0, 0, sl] * pw_ref[1], axis=0)


def pool_pages(cache_k, cache_v, page_table, pool_w3, layer):
    Bs, n_pages = page_table.shape
    G = cache_k.shape[3]
    per = PAGE_SIZE // BLOCK
    pp = _tile(n_pages, (16, 8, 4, 2, 1))
    page = lambda q: pl.BlockSpec((1, 1, PAGE_SIZE, G, HEAD),
                                  lambda b, p, pt: (layer, pt[b * n_pages + p * pp + q], 0, 0, 0))
    out = pl.BlockSpec((1, pp * per, G, HEAD), lambda b, p, pt: (b, p, 0, 0))
    return pl.pallas_call(
        functools.partial(_pool_pages_kernel, pp=pp),
        grid_spec=pltpu.PrefetchScalarGridSpec(
            num_scalar_prefetch=1,
            grid=(Bs, n_pages // pp),
            in_specs=[page(q) for q in range(pp)] * 2
                     + [pl.BlockSpec((2, BLOCK, G, HEAD), lambda b, p, pt: (0, 0, 0, 0))],
            out_specs=[out, out]),
        out_shape=[jax.ShapeDtypeStruct((Bs, n_pages * per, G, HEAD), F32)] * 2,
        compiler_params=_params("parallel", "parallel"),
        name="pool_pages",
    )(page_table.reshape(-1), *([cache_k] * pp), *([cache_v] * pp), pool_w3)


def _head_rows(qrow, g, R):
    rows = [qrow[:, (g * R + r) * HEAD:(g * R + r + 1) * HEAD] for r in range(R)]
    rows.append(jnp.zeros((8 - R, HEAD), F32))
    return jnp.concatenate(rows, axis=0).astype(BF16)


def _sample_cmp_win_kernel(q_ref, pk_ref, pv_ref, cwk_ref, cwv_ref, kwn_ref, vw_ref, bg_ref, g_ref,
                           pc_ref, pw_ref, idx_ref, *, G, R, past):
    nbp = pk_ref.shape[1]
    wb = cwk_ref.shape[2]
    qpos = past
    cur = qpos // BLOCK
    nlane = (cur // HEAD + 1) * HEAD
    qrow = q_ref[0]
    gates = _sigmoid(bg_ref[0])
    qk_g = g_ref[...]
    for g in range(G):
        sl = slice(g * HEAD, (g + 1) * HEAD)
        qg = _head_rows(qrow, g, R)

        kcb = _rms(pk_ref[0, :, g, :], qk_g[1:2]).astype(BF16)
        sc = _dot_nt(qg, kcb)
        vis = (_iota((8, nbp), 1) + 1) * BLOCK <= qpos + 1
        p = jnp.where(vis, _softmax_masked(sc, vis), 0.0)
        o_cmp = _dot(p.astype(BF16), pv_ref[0, :, g, :].astype(BF16))
        imp = p[0:1]
        for r in range(1, R):
            imp = imp + p[r:r + 1]

        imp = jnp.concatenate([imp, jnp.zeros((1, nlane - nbp), F32)], axis=1)
        j = _iota((1, nlane), 1)
        forced = (j == 0) | (j == cur) | (j == cur - 1)
        score = jnp.where(j <= cur, imp + jnp.where(forced, FORCE_BONUS, 0.0), NEG)
        s_col = _column(score, nlane)
        ii = _iota((nlane, nlane), 0)
        jj = _iota((nlane, nlane), 1)
        ahead = (s_col > score) | ((s_col == score) & (ii < jj))
        rank = jnp.sum(ahead.astype(F32), axis=0, keepdims=True)
        rank_col = _column(rank, nlane)
        hit = rank_col == _iota((nlane, HEAD), 1).astype(F32)
        idx = jnp.sum(jnp.where(hit, _iota((nlane, HEAD), 0).astype(F32), 0.0), axis=0, keepdims=True)
        idx_ref[0, g:g + 1, :] = idx.astype(I32)

        kwc = cwk_ref[0, 0, :, g, :].astype(BF16)
        sw = _dot_nt(qg, kwc)
        diff = qpos - (past - wb + _iota((8, wb), 1))
        sw = jnp.where((diff >= 0) & (diff <= WINDOW), sw, NEG)
        kn = kwn_ref[0][:, sl].astype(BF16).astype(F32)
        vn = vw_ref[0][:, sl].astype(BF16).astype(F32)
        sn = jnp.sum(qg.astype(F32) * kn, axis=1, keepdims=True)
        m = jnp.maximum(jnp.max(sw, axis=1, keepdims=True), sn)
        e = jnp.exp(sw - m)
        en = jnp.exp(sn - m)
        den = jnp.sum(e, axis=1, keepdims=True) + en
        o_win = (_dot((e / den).astype(BF16), cwv_ref[0, 0, :, g, :].astype(BF16))
                 + (en / den).astype(BF16).astype(F32) * vn)

        for r in range(R):
            c0 = (g * R + r) * 3
            hs = slice((g * R + r) * HEAD, (g * R + r + 1) * HEAD)
            pc_ref[0, :, hs] = gates[:, c0:c0 + 1] * o_cmp[r:r + 1]
            pw_ref[0, :, hs] = gates[:, c0 + 2:c0 + 3] * o_win[r:r + 1]


def sample_cmp_win(qn3, pk, pv, cache_win_k, cache_win_v, kwn3, z3, zbg3, qk_g, layer, G, R, past, col_vw):
    Bs = qn3.shape[0]
    depth, _, wb = cache_win_k.shape[:3]
    kvw = G * HEAD
    qw = G * R * HEAD
    nbp = pk.shape[1]
    row = lambda w, j=0: pl.BlockSpec((1, 1, w), lambda b: (b, 0, j))
    return pl.pallas_call(
        functools.partial(_sample_cmp_win_kernel, G=G, R=R, past=past),
        grid=(Bs,),
        in_specs=[row(qw),
                  pl.BlockSpec((1, nbp, G, HEAD), lambda b: (b, 0, 0, 0)),
                  pl.BlockSpec((1, nbp, G, HEAD), lambda b: (b, 0, 0, 0)),
                  pl.BlockSpec((1, 1, wb, G, HEAD), lambda b: (layer, b, 0, 0, 0)),
                  pl.BlockSpec((1, 1, wb, G, HEAD), lambda b: (layer, b, 0, 0, 0)),
                  row(kvw), row(kvw, col_vw // kvw), row(HEAD),
                  pl.BlockSpec((4, HEAD), lambda b: (0, 0))],
        out_specs=[row(qw), row(qw), pl.BlockSpec((1, G, HEAD), lambda b: (b, 0, 0))],
        out_shape=[jax.ShapeDtypeStruct((Bs, 1, qw), F32), jax.ShapeDtypeStruct((Bs, 1, qw), F32),
                   jax.ShapeDtypeStruct((Bs, G, HEAD), I32)],
        compiler_params=_params("parallel"),
        name="sample_cmp_win",
    )(qn3, pk, pv, cache_win_k, cache_win_v, kwn3, z3, zbg3, qk_g)


def _sample_select_kernel(idx_ref, pt_ref, q_ref, *refs, G, R, n_top, past):
    ck_refs, cv_refs = refs[:G], refs[G:2 * G]
    kn_ref, vn_ref, bg_ref, pc_ref, pw_ref, o_ref, m_ref, l_ref, acc_ref = refs[2 * G:]
    b, n = pl.program_id(0), pl.program_id(1)
    qpos = past

    @pl.when(n == 0)
    def _():
        m_ref[...] = jnp.full_like(m_ref, NEG)
        l_ref[...] = jnp.zeros_like(l_ref)
        acc_ref[...] = jnp.zeros_like(acc_ref)

    qrow = q_ref[0]
    for g in range(G):
        sl = slice(g * HEAD, (g + 1) * HEAD)
        j = idx_ref[(b * G + g) * n_top + n]
        qg = _head_rows(qrow, g, R)
        s = _dot_nt(qg, ck_refs[g][0, 0, :, g, :].astype(BF16))
        pos = j * BLOCK + _iota((8, BLOCK), 1)
        s = jnp.where((pos < past) & (pos <= qpos), s, NEG)
        kn = kn_ref[0][:, sl].astype(BF16).astype(F32)
        vn = vn_ref[0][:, sl].astype(BF16).astype(F32)
        sn = jnp.sum(qg.astype(F32) * kn, axis=1, keepdims=True)
        sn = jnp.where(j == past // BLOCK, sn, NEG)

        m_old = m_ref[g][:, 0:1]
        m_new = jnp.maximum(m_old, jnp.maximum(jnp.max(s, axis=1, keepdims=True), sn))
        alpha = jnp.exp(m_old - m_new)
        e = jnp.exp(s - m_new)
        en = jnp.exp(sn - m_new)
        l_new = alpha * l_ref[g][:, 0:1] + jnp.sum(e, axis=1, keepdims=True) + en
        acc = (alpha * acc_ref[g] + _dot(e.astype(BF16), cv_refs[g][0, 0, :, g, :].astype(BF16))
               + en.astype(BF16).astype(F32) * vn)
        m_ref[g] = jnp.broadcast_to(m_new, (8, HEAD))
        l_ref[g] = jnp.broadcast_to(l_new, (8, HEAD))
        acc_ref[g] = acc

    @pl.when(n == n_top - 1)
    def _():
        gates = _sigmoid(bg_ref[0])
        for g in range(G):
            o_sel = acc_ref[g] / l_ref[g][:, 0:1]
            for r in range(R):
                c = (g * R + r) * 3 + 1
                hs = slice((g * R + r) * HEAD, (g * R + r + 1) * HEAD)
                o_ref[0, :, hs] = ((pc_ref[0, :, hs] + gates[:, c:c + 1] * o_sel[r:r + 1])
                                   + pw_ref[0, :, hs]).astype(o_ref.dtype)


def sample_select(idx, page_table, qn3, cache_k, cache_v, ksn3, z3, zbg3, pc, pw, layer, G, R, past, col_vs):
    Bs, n_pages = page_table.shape
    n_top = idx.shape[-1]
    depth, n_pool = cache_k.shape[:2]
    kvw, qw = G * HEAD, G * R * HEAD
    per = PAGE_SIZE // BLOCK
    ck = cache_k.reshape(depth, n_pool * per, BLOCK, G, HEAD)
    cv = cache_v.reshape(depth, n_pool * per, BLOCK, G, HEAD)
    nbp = past // BLOCK

    def blk(g):
        def index(b, n, idx_ref, pt_ref):
            jc = jnp.minimum(idx_ref[(b * G + g) * n_top + n], nbp - 1)
            return (layer, pt_ref[b * n_pages + jc // per] * per + jc % per, 0, 0, 0)
        return pl.BlockSpec((1, 1, BLOCK, G, HEAD), index)

    row = lambda w, j=0: pl.BlockSpec((1, 1, w), lambda b, n, i_, p_: (b, 0, j))
    return pl.pallas_call(
        functools.partial(_sample_select_kernel, G=G, R=R, n_top=n_top, past=past),
        grid_spec=pltpu.PrefetchScalarGridSpec(
            num_scalar_prefetch=2,
            grid=(Bs, n_top),
            in_specs=[row(qw)] + [blk(g) for g in range(G)] * 2
                     + [row(kvw), row(kvw, col_vs // kvw), row(HEAD), row(qw), row(qw)],
            out_specs=row(qw),
            scratch_shapes=[pltpu.VMEM((G, 8, HEAD), F32)] * 3),
        out_shape=jax.ShapeDtypeStruct((Bs, 1, qw), BF16),
        compiler_params=_params("parallel", "arbitrary"),
        name="sample_select",
    )(idx.reshape(-1), page_table.reshape(-1), qn3, *([ck] * G), *([cv] * G), ksn3, z3, zbg3, pc, pw)


def _cast_kernel(x_ref, o_ref):
    o_ref[...] = x_ref[0].astype(o_ref.dtype)


def cast_layer_bf16(w, layer):
    _, R, C = w.shape
    tr = next((t for t in (512, 256, 128, 64, 32, 16) if R % t == 0 and t * C * 4 <= CAST_BLOCK_BYTES), R)
    return pl.pallas_call(
        _cast_kernel,
        grid=(R // tr,),
        in_specs=[pl.BlockSpec((1, tr, C), lambda i: (layer, i, 0))],
        out_specs=pl.BlockSpec((tr, C), lambda i: (i, 0)),
        out_shape=jax.ShapeDtypeStruct((R, C), BF16),
        compiler_params=_params("parallel"),
        name="cast_layer_bf16",
    )(w)


def _cast_cut_kernel(a_ref, b_ref, o_ref, *, nj0, cut):
    j = pl.program_id(0)
    tn = o_ref.shape[1]

    @pl.when(j < nj0)
    def _():
        o_ref[...] = a_ref[0].T.astype(o_ref.dtype)

    @pl.when(j >= nj0)
    def _():
        x = jnp.concatenate([a_ref[0], b_ref[0]], axis=0)
        o_ref[...] = x[cut:cut + tn].T.astype(o_ref.dtype)


def cast_cut_bf16(w_t, layer, c0, cut):
    _, C, R = w_t.shape
    N = C - cut
    tn = next(t for t in (256, 128) if c0 % t == 0 and N % t == 0)
    assert cut < HEAD and cut % 8 == 0
    return pl.pallas_call(
        functools.partial(_cast_cut_kernel, nj0=c0 // tn, cut=cut),
        grid=(N // tn,),
        in_specs=[pl.BlockSpec((1, tn, R), lambda j: (layer, j, 0)),
                  pl.BlockSpec((1, HEAD, R), lambda j: (layer, (j + 1) * (tn // HEAD), 0))],
        out_specs=pl.BlockSpec((R, tn), lambda j: (0, j)),
        out_shape=jax.ShapeDtypeStruct((R, N), BF16),
        compiler_params=_params("parallel"),
        name="cast_cut_bf16",
    )(w_t, w_t)


def _cast_t_kernel(x_ref, o_ref):
    o_ref[...] = x_ref[0].T.astype(o_ref.dtype)


def cast_cols_bf16(w_t, layer, col_block):
    R = w_t.shape[2]
    return pl.pallas_call(
        _cast_t_kernel,
        grid=(1,),
        in_specs=[pl.BlockSpec((1, HEAD, R), lambda i: (layer, col_block, 0))],
        out_specs=pl.BlockSpec((R, HEAD), lambda i: (0, 0)),
        out_shape=jax.ShapeDtypeStruct((R, HEAD), BF16),
        compiler_params=_params("arbitrary"),
        name="cast_cols_bf16",
    )(w_t)


def _layer_weights(l, norm_g, w_in, qk_g, cmp_pool, w_pa, w_pb, w_o, f1_gu, f1_dn, f2_gu, f2_dn, H, HQ, G):
    kw, qw, kvw = H * HEAD, HQ * HEAD, G * HEAD
    D = w_in.shape[1]
    c_bg = 4 * kw + qw + 6 * kvw
    n_bg = 3 * HQ
    w_in_t = jnp.transpose(w_in, (0, 2, 1))
    w_main = cast_cut_bf16(w_in_t, l, c_bg, n_bg)
    w_bg = cast_cols_bf16(w_in_t, l, c_bg // HEAD)
    pool_w2 = jnp.repeat(jnp.transpose(cmp_pool[l], (1, 0, 2)).reshape(BLOCK, 2 * G), HEAD, axis=1)
    pool_w3 = jnp.broadcast_to(cmp_pool[l][..., None], cmp_pool.shape[1:] + (HEAD,))
    cast = lambda w: cast_layer_bf16(w, l)
    return dict(norm_g=norm_g[l], w_main=w_main, w_bg=w_bg, qk_g=qk_g[l], pool_w2=pool_w2, pool_w3=pool_w3,
                w_pa=cast(w_pa), w_pb=cast(w_pb), w_o=cast(w_o),
                ffn1=(f1_gu, cast(f1_dn)), ffn2=(f2_gu, cast(f2_dn)), c_bg=c_bg, D=D, l=l)


def _ffn(xp, xs, g, ffn_w, l):
    gu, dn = ffn_w
    hp, hs = matmul_swiglu(rmsnorm_rows(xp, g), rmsnorm_rows(xs, g), gu, l)
    return matmul(hp, dn, res=xp, scale=0.5), matmul(hs, dn, res=xs, scale=0.5)


def _layer(xp, xs, lw, prompt_mixers, sample_mixers):
    hp, hs = _ffn(xp, xs, lw["norm_g"][0], lw["ffn1"], lw["l"])
    out, states = [], []
    for h, mixers in ((hp, prompt_mixers), (hs, sample_mixers)):
        u = rmsnorm_rows(h, lw["norm_g"][1])
        z = matmul(u, lw["w_main"])
        zbg = matmul(u, lw["w_bg"])
        ya, yb, state = mixers(z, zbg)
        m = merge_project(ya, yb, lw["w_pa"], lw["w_pb"], z, lw["c_bg"], lw["c_bg"] + lw["D"])
        out.append(matmul(m, lw["w_o"], res=h, scale=1.0))
        states.append(state)
    yp, ys = _ffn(out[0], out[1], lw["norm_g"][2], lw["ffn2"], lw["l"])
    return yp, ys, states[0], states[1]


def kernel(x_prompt, x_sample, state_hgrn, cache_cmp_k, cache_cmp_v, cache_sel_k, cache_sel_v, cache_win_k, cache_win_v, page_table, norm_g, w_in, hgrn_lb, hgrn_out_g, qk_g, cmp_pool, w_proj_a, w_proj_b, w_out, ffn1_gu, ffn1_down, ffn2_gu, ffn2_down):
    Bn, T, D = x_prompt.shape
    Bs, Tn, _ = x_sample.shape
    depth = w_in.shape[0]
    H = hgrn_lb.shape[1] // HEAD
    HQ = w_proj_b.shape[1] // HEAD
    G = cache_cmp_k.shape[3]
    R = HQ // G
    kw, qw, kvw = H * HEAD, HQ * HEAD, G * HEAD
    past = page_table.shape[1] * PAGE_SIZE
    wb = cache_win_k.shape[2]
    assert Tn == 1 and T % HGRN_CHUNK == 0 and past % PAGE_SIZE == 0 and R <= 8
    assert hgrn_out_g.shape[1] == HEAD and w_proj_a.shape[1] == kw and wb <= WINDOW and wb <= past
    col_q = 4 * kw
    col_kv = col_q + qw

    p = jax.nn.softmax(hgrn_lb.astype(F32), axis=0)
    lbs = jnp.cumsum(p, axis=0) - p[0]

    y_p = x_prompt.reshape(Bn * T, D)
    y_s = x_sample.reshape(Bs, D)
    st_p, st_s = [], []
    for l in range(depth):
        lw = _layer_weights(l, norm_g, w_in, qk_g, cmp_pool, w_proj_a, w_proj_b, w_out,
                            ffn1_gu, ffn1_down, ffn2_gu, ffn2_down, H, HQ, G)

        def prompt_mixers(z, zbg):
            ya, s_new = hgrn_prompt(z, lbs[l], hgrn_out_g[l], Bn, T, H)
            qn, ksn, kwn, pooled = nsa_prep(z, lw["pool_w2"], lw["qk_g"], HQ, G, col_q, True)
            yb = nsa_prompt(qn, pooled, ksn, kwn, z, zbg, Bn, T, HQ, G, col_kv)
            kv = lambda i: z[:, col_kv + i * kvw:col_kv + (i + 1) * kvw].reshape(Bn, T, G, HEAD)
            tail = lambda a: a[:, T - min(WINDOW, T):]
            return ya, yb, (s_new, kv(0), kv(1), ksn.reshape(Bn, T, G, HEAD), kv(3),
                            tail(kwn.reshape(Bn, T, G, HEAD)), tail(kv(5)))

        def sample_mixers(z, zbg):
            z3, zbg3 = z.reshape(Bs, 1, -1), zbg.reshape(Bs, 1, HEAD)
            ya, s_new = hgrn_sample(z3, lbs[l], hgrn_out_g[l], state_hgrn, l, H)
            qn, ksn, kwn = nsa_prep(z, lw["pool_w2"], lw["qk_g"], HQ, G, col_q, False)
            qn3, ksn3, kwn3 = qn.reshape(Bs, 1, qw), ksn.reshape(Bs, 1, kvw), kwn.reshape(Bs, 1, kvw)
            pk, pv = pool_pages(cache_cmp_k, cache_cmp_v, page_table, lw["pool_w3"], l)
            pc, pw, idx = sample_cmp_win(qn3, pk, pv, cache_win_k, cache_win_v, kwn3, z3, zbg3, lw["qk_g"],
                                         l, G, R, past, col_kv + 5 * kvw)
            n_top = min(TOP_N, -(-(past + Tn) // BLOCK))
            yb = sample_select(idx[:, :, :n_top], page_table, qn3, cache_sel_k, cache_sel_v, ksn3, z3, zbg3,
                               pc, pw, l, G, R, past, col_kv + 3 * kvw)
            kv = lambda i: z[:, col_kv + i * kvw:col_kv + (i + 1) * kvw].reshape(Bs, 1, G, HEAD)
            win_k = jnp.concatenate([cache_win_k[l], kwn.reshape(Bs, 1, G, HEAD)], axis=1)[:, -wb:]
            win_v = jnp.concatenate([cache_win_v[l], kv(5)], axis=1)[:, -wb:]
            return ya.reshape(Bs, kw), yb.reshape(Bs, qw), (
                s_new, kv(0), kv(1), ksn.reshape(Bs, 1, G, HEAD), kv(3), win_k, win_v)

        y_p, y_s, sp, ss = _layer(y_p, y_s, lw, prompt_mixers, sample_mixers)
        st_p.append(sp)
        st_s.append(ss)

    stack = lambda states, i: jnp.stack([s[i] for s in states])
    return ((y_p.reshape(Bn, T, D), y_s.reshape(Bs, Tn, D))
            + tuple(stack(st_p, i) for i in range(7)) + tuple(stack(st_s, i) for i in range(7)))
```

```python
import functools

import jax
import jax.numpy as jnp
from jax import lax
from jax.experimental import pallas as pl
from jax.experimental.pallas import tpu as pltpu

F32 = jnp.float32
BF16 = jnp.bfloat16
I32 = jnp.int32

HEAD = 128
BLOCK = 64
TOP_N = 16
WINDOW = 512
PAGE_SIZE = 128
EPS = 1e-6
NEG = -1e30
LB_FLOOR = 1e-30
FORCE_BONUS = 1e4
ATTN_SCALE = HEAD ** -0.5
HGRN_CHUNK = 128
DIAG = 4
STRIP = 16
CAST_BLOCK_BYTES = 8 * 1024 * 1024
VMEM_BYTES_V7X = 64 * 1024 * 1024
VMEM_LIMIT = VMEM_BYTES_V7X - 8 * 1024 * 1024


def _params(*sem):
    return pltpu.CompilerParams(dimension_semantics=sem, vmem_limit_bytes=VMEM_LIMIT)


def _tile(dim, candidates):
    for c in candidates:
        if dim % c == 0:
            return c
    return dim


def _dot(a, b):
    return jnp.dot(a, b, preferred_element_type=F32)


def _dot_nt(a, b):
    return lax.dot_general(a, b, (((1,), (1,)), ((), ())), preferred_element_type=F32)


def _dot_tn(a, b):
    return lax.dot_general(a, b, (((0,), (0,)), ((), ())), preferred_element_type=F32)


def _sigmoid(x):
    return jax.nn.sigmoid(x)


def _silu(x):
    return x * jax.nn.sigmoid(x)


def _log_sigmoid(x):
    return jnp.minimum(x, 0.0) - jnp.log1p(jnp.exp(-jnp.abs(x)))


def _logaddexp(a, b):
    return jnp.maximum(a, b) + jnp.log1p(jnp.exp(-jnp.abs(a - b)))


def _rms(x, g):
    ms = jnp.mean(x * x, axis=-1, keepdims=True)
    return x * lax.rsqrt(ms + EPS) * g


def _softmax_masked(s, mask):
    s = jnp.where(mask, s, NEG)
    m = jnp.max(s, axis=-1, keepdims=True)
    e = jnp.exp(s - m)
    return e / jnp.sum(e, axis=-1, keepdims=True)


def _iota(shape, dim):
    return lax.broadcasted_iota(I32, shape, dim)


def _rmsnorm_kernel(x_ref, g_ref, o_ref):
    o_ref[...] = _rms(x_ref[...], g_ref[...]).astype(o_ref.dtype)


def rmsnorm_rows(x, g):
    M, D = x.shape
    tm = _tile(M, (256, 128, 64, 32, 16, 8))
    return pl.pallas_call(
        _rmsnorm_kernel,
        grid=(M // tm,),
        in_specs=[pl.BlockSpec((tm, D), lambda i: (i, 0)), pl.BlockSpec((1, D), lambda i: (0, 0))],
        out_specs=pl.BlockSpec((tm, D), lambda i: (i, 0)),
        out_shape=jax.ShapeDtypeStruct((M, D), BF16),
        compiler_params=_params("parallel"),
        name="rmsnorm_rows",
    )(x, g.reshape(1, D))


def _mm_kernel(x_ref, w_ref, *rest, nk, scale, has_res):
    if has_res:
        r_ref, o_ref = rest[0], rest[1]
        scratch = rest[2:]
    else:
        r_ref, o_ref = None, rest[0]
        scratch = rest[1:]

    def finish(acc):
        if has_res:
            o_ref[...] = (r_ref[...] + scale * acc).astype(o_ref.dtype)
        else:
            o_ref[...] = acc.astype(o_ref.dtype)

    if nk == 1:
        finish(_dot(x_ref[...], w_ref[...]))
    else:
        acc_ref = scratch[0]
        k = pl.program_id(2)

        @pl.when(k == 0)
        def _():
            acc_ref[...] = jnp.zeros_like(acc_ref)

        acc_ref[...] += _dot(x_ref[...], w_ref[...])

        @pl.when(k == nk - 1)
        def _():
            finish(acc_ref[...])


def matmul(x, w, *, res=None, scale=1.0):
    M, K = x.shape
    N = w.shape[1]
    has_res = res is not None
    tm = _tile(M, (1024, 512, 256, 128, 64, 32, 16, 8))
    if K <= 4096:
        tk = K
        tn = _tile(N, (512, 256, 128) if has_res else (1024, 512, 256, 128))
    else:
        tk = next((K // d for d in (2, 3, 4, 5, 6, 8) if K % (d * HEAD) == 0), HEAD)
        tn = _tile(N, (512, 256, 128))
    nk = K // tk
    in_specs = [pl.BlockSpec((tm, tk), lambda i, j, k: (i, k)), pl.BlockSpec((tk, tn), lambda i, j, k: (k, j))]
    args = [x, w]
    if has_res:
        in_specs.append(pl.BlockSpec((tm, tn), lambda i, j, k: (i, j)))
        args.append(res)
    return pl.pallas_call(
        functools.partial(_mm_kernel, nk=nk, scale=scale, has_res=has_res),
        grid=(M // tm, N // tn, nk),
        in_specs=in_specs,
        out_specs=pl.BlockSpec((tm, tn), lambda i, j, k: (i, j)),
        out_shape=jax.ShapeDtypeStruct((M, N), F32),
        scratch_shapes=[pltpu.VMEM((tm, tn), F32)] if nk > 1 else [],
        compiler_params=_params("parallel", "parallel", "arbitrary"),
        name="matmul",
    )(*args)


def _swiglu_kernel(x_ref, xs_ref, wg_ref, wu_ref, o_ref, os_ref):
    i, j = pl.program_id(0), pl.program_id(1)
    wg = wg_ref[0].astype(BF16)
    wu = wu_ref[0].astype(BF16)
    x = x_ref[...]
    o_ref[...] = (_silu(_dot(x, wg)) * _dot(x, wu)).astype(o_ref.dtype)

    @pl.when(i == 0)
    def _():
        xs = xs_ref[...]
        os_ref[j] = (_silu(_dot(xs, wg)) * _dot(xs, wu)).astype(os_ref.dtype)


def matmul_swiglu(x, xs, w_gu, layer):
    M, K = x.shape
    Ms = xs.shape[0]
    F = w_gu.shape[2] // 2
    tn = _tile(F, (256, 128))
    tm = _tile(M, (2048, 1024, 512, 256, 128, 64, 32, 16, 8))
    nj = F // tn
    o, os = pl.pallas_call(
        _swiglu_kernel,
        grid=(M // tm, nj),
        in_specs=[pl.BlockSpec((tm, K), lambda i, j: (i, 0), pipeline_mode=pl.Buffered(1)),
                  pl.BlockSpec((Ms, K), lambda i, j: (0, 0)),
                  pl.BlockSpec((1, K, tn), lambda i, j: (layer, 0, j)),
                  pl.BlockSpec((1, K, tn), lambda i, j: (layer, 0, j + nj))],
        out_specs=[pl.BlockSpec((tm, tn), lambda i, j: (i, j)),
                   pl.BlockSpec((nj, Ms, tn), lambda i, j: (0, 0, 0))],
        out_shape=[jax.ShapeDtypeStruct((M, F), BF16), jax.ShapeDtypeStruct((nj, Ms, tn), BF16)],
        compiler_params=_params("arbitrary", "arbitrary"),
        name="matmul_swiglu",
    )(x, xs, w_gu, w_gu)
    return o, jnp.transpose(os, (1, 0, 2)).reshape(Ms, F)


def _merge_kernel(ya_ref, yb_ref, wa_ref, wb_ref, ga_ref, gb_ref, o_ref):
    pa = _dot(ya_ref[...], wa_ref[...])
    pb = _dot(yb_ref[...], wb_ref[...])
    o_ref[...] = (_sigmoid(ga_ref[...]) * pa + _sigmoid(gb_ref[...]) * pb).astype(o_ref.dtype)


def merge_project(ya, yb, wa, wb, z, col_a, col_b):
    M, Ka = ya.shape
    Kb = yb.shape[1]
    N = wa.shape[1]
    tm = _tile(M, (1024, 512, 256, 128, 64, 32, 16, 8))
    tn = _tile(N, (512, 256, 128))
    ja, jb = col_a // tn, col_b // tn
    return pl.pallas_call(
        _merge_kernel,
        grid=(M // tm, N // tn),
        in_specs=[pl.BlockSpec((tm, Ka), lambda i, j: (i, 0)),
                  pl.BlockSpec((tm, Kb), lambda i, j: (i, 0)),
                  pl.BlockSpec((Ka, tn), lambda i, j: (0, j)),
                  pl.BlockSpec((Kb, tn), lambda i, j: (0, j)),
                  pl.BlockSpec((tm, tn), lambda i, j: (i, j + ja)),
                  pl.BlockSpec((tm, tn), lambda i, j: (i, j + jb))],
        out_specs=pl.BlockSpec((tm, tn), lambda i, j: (i, j)),
        out_shape=jax.ShapeDtypeStruct((M, N), BF16),
        compiler_params=_params("parallel", "parallel"),
        name="merge_project",
    )(ya, yb, wa, wb, z, z)


def _split3(x):
    hi = x.astype(BF16)
    r1 = x - hi.astype(F32)
    mid = r1.astype(BF16)
    lo = (r1 - mid.astype(F32)).astype(BF16)
    return hi, mid, lo


def _hgrn_gates(zq, zf, lb):
    q = _silu(zq)
    log_f = _logaddexp(jnp.log(jnp.maximum(lb, LB_FLOOR)), jnp.log1p(-lb) + _log_sigmoid(zf))
    k = (1.0 - lb) * _sigmoid(-zf)
    return q, log_f, k


def _hgrn_masks():
    C = HGRN_CHUNK
    row = _iota((C, C), 0)
    col = _iota((C, C), 1)
    tri = jnp.where(col <= row, 1.0, 0.0).astype(BF16)
    levels = []
    h = C // 2
    while h >= DIAG:
        P = 2 * h
        levels.append((h, (row & (P - 1)) >= h, (row // P) == (col // P) if P < C else None))
        h //= 2
    rloc = row & (DIAG - 1)
    diags = [(rloc >= d, (col == row - d) & (rloc >= d)) for d in range(DIAG)]
    return tri, levels, diags


def _hgrn_chunk(zq, zf, v, zg, lb, og, st, masks):
    C = HGRN_CHUNK
    tri, levels, diags = masks
    q, log_f, k = _hgrn_gates(zq, zf, lb)
    vb = v.astype(BF16)

    hi, mid, lo = _split3(log_f)
    b = _dot(tri, hi) + _dot(tri, mid) + _dot(tri, lo)

    a = jnp.zeros((C, C), F32)
    for h, second, same_parent in levels:
        P = 2 * h
        bref = jnp.concatenate(
            [jnp.broadcast_to(b[m * P + h - 1:m * P + h, :], (P, HEAD)) for m in range(C // P)], axis=0)
        qt = q * jnp.exp(jnp.where(second, b - bref, NEG))
        kt = k * jnp.exp(jnp.where(second, NEG, bref - b))
        lvl = _dot_nt(qt.astype(BF16), kt.astype(BF16))
        a = a + (lvl if same_parent is None else jnp.where(same_parent, lvl, 0.0))

    for d, (in_block, on_diag) in enumerate(diags):
        if d == 0:
            p = q * k
        else:
            bs = pltpu.roll(b, d, 0)
            ks = pltpu.roll(k, d, 0)
            p = q * jnp.exp(jnp.where(in_block, b - bs, NEG)) * ks
        a = jnp.where(on_diag, jnp.sum(p, axis=1, keepdims=True), a)

    o = _dot_nt((q * jnp.exp(b)).astype(BF16), st.astype(BF16)) + _dot(a.astype(BF16), vb)
    b_last = b[C - 1:C, :]
    kd = k * jnp.exp(b_last - b)
    st_new = st * jnp.exp(b_last) + _dot_tn(vb, kd.astype(BF16))
    return _rms(o, og) * _silu(zg), st_new


def _hgrn_prompt_kernel(q_ref, f_ref, v_ref, g_ref, lb_ref, og_ref, y_ref, s_ref, st_ref, *, nc, hb):
    c = pl.program_id(2)

    @pl.when(c == 0)
    def _():
        st_ref[...] = jnp.zeros_like(st_ref)

    masks = _hgrn_masks()
    for i in range(hb):
        sl = slice(i * HEAD, (i + 1) * HEAD)
        y, st_new = _hgrn_chunk(q_ref[:, sl], f_ref[:, sl], v_ref[:, sl], g_ref[:, sl], lb_ref[:, sl],
                                og_ref[...], st_ref[i], masks)
        st_ref[i] = st_new
        y_ref[:, sl] = y.astype(y_ref.dtype)

    @pl.when(c == nc - 1)
    def _():
        for i in range(hb):
            s_ref[0, i] = st_ref[i].T


def hgrn_prompt(z, lb, out_g, Bn, T, H):
    C = HGRN_CHUNK
    nc = T // C
    hb = _tile(H, (8, 4, 2, 1))
    w = hb * HEAD
    blk = lambda off: pl.BlockSpec((C, w), lambda b, h, c: (b * nc + c, off + h))
    return pl.pallas_call(
        functools.partial(_hgrn_prompt_kernel, nc=nc, hb=hb),
        grid=(Bn, H // hb, nc),
        in_specs=[blk(0), blk(H // hb), blk(2 * H // hb), blk(3 * H // hb),
                  pl.BlockSpec((1, w), lambda b, h, c: (0, h)),
                  pl.BlockSpec((1, HEAD), lambda b, h, c: (0, 0))],
        out_specs=[pl.BlockSpec((C, w), lambda b, h, c: (b * nc + c, h)),
                   pl.BlockSpec((1, hb, HEAD, HEAD), lambda b, h, c: (b, h, 0, 0))],
        out_shape=[jax.ShapeDtypeStruct((Bn * T, H * HEAD), BF16),
                   jax.ShapeDtypeStruct((Bn, H, HEAD, HEAD), F32)],
        scratch_shapes=[pltpu.VMEM((hb, HEAD, HEAD), F32)],
        compiler_params=_params("parallel", "parallel", "arbitrary"),
        name="hgrn_prompt",
    )(z, z, z, z, lb.reshape(1, H * HEAD), out_g.reshape(1, HEAD))


def _column(rowvec, n):
    eye = _iota((n, n), 0) == _iota((n, n), 1)
    return jnp.sum(jnp.where(eye, jnp.broadcast_to(rowvec, (n, n)), 0.0), axis=1, keepdims=True)


def _hgrn_sample_kernel(q_ref, f_ref, v_ref, g_ref, lb_ref, og_ref, s0_ref, y_ref, s_ref, *, H):
    for h in range(H):
        sl = slice(h * HEAD, (h + 1) * HEAD)
        q, log_f, k = _hgrn_gates(q_ref[0][:, sl], f_ref[0][:, sl], lb_ref[:, sl])
        v = v_ref[0][:, sl]
        s_new = _column(jnp.exp(log_f), HEAD) * s0_ref[0, 0, h] + _column(k, HEAD) * v
        o = jnp.sum(_column(q, HEAD) * s_new, axis=0, keepdims=True)
        y_ref[0, :, sl] = (_rms(o, og_ref[...]) * _silu(g_ref[0][:, sl])).astype(y_ref.dtype)
        s_ref[0, h] = s_new


def hgrn_sample(z3, lb, out_g, state, layer, H):
    Bs = z3.shape[0]
    w = H * HEAD
    blk = lambda off: pl.BlockSpec((1, 1, w), lambda b: (b, 0, off))
    return pl.pallas_call(
        functools.partial(_hgrn_sample_kernel, H=H),
        grid=(Bs,),
        in_specs=[blk(0), blk(1), blk(2), blk(3),
                  pl.BlockSpec((1, w), lambda b: (0, 0)),
                  pl.BlockSpec((1, HEAD), lambda b: (0, 0)),
                  pl.BlockSpec((1, 1, H, HEAD, HEAD), lambda b: (layer, b, 0, 0, 0))],
        out_specs=[pl.BlockSpec((1, 1, w), lambda b: (b, 0, 0)),
                   pl.BlockSpec((1, H, HEAD, HEAD), lambda b: (b, 0, 0, 0))],
        out_shape=[jax.ShapeDtypeStruct((Bs, 1, w), BF16),
                   jax.ShapeDtypeStruct((Bs, H, HEAD, HEAD), F32)],
        compiler_params=_params("parallel"),
        name="hgrn_sample",
    )(z3, z3, z3, z3, lb.reshape(1, w), out_g.reshape(1, HEAD), state)


def _nsa_prep_kernel(qb_ref, kv_ref, ks_ref, kw_ref, pw_ref, g_ref, qn_ref, ksn_ref, kwn_ref, *pool_refs,
                     HQ, G, pool):
    g = g_ref[...]
    for h in range(HQ):
        sl = slice(h * HEAD, (h + 1) * HEAD)
        qn_ref[:, sl] = (_rms(qb_ref[:, sl], g[0:1]) * ATTN_SCALE).astype(qn_ref.dtype)
    for j in range(G):
        sl = slice(j * HEAD, (j + 1) * HEAD)
        ksn_ref[:, sl] = _rms(ks_ref[:, sl], g[2:3])
        kwn_ref[:, sl] = _rms(kw_ref[:, sl], g[3:4])
    if pool:
        pool_ref = pool_refs[0]
        kvw = G * HEAD
        w = pw_ref[...]
        for n in range(kv_ref.shape[0] // BLOCK):
            blk = jnp.sum(kv_ref[n * BLOCK:(n + 1) * BLOCK, :] * w, axis=0, keepdims=True)
            for j in range(G):
                sl = slice(j * HEAD, (j + 1) * HEAD)
                pool_ref[n:n + 1, sl] = _rms(blk[:, sl], g[1:2])
            pool_ref[n:n + 1, kvw:] = blk[:, kvw:]


def nsa_prep(z, pool_w2, qk_g, HQ, G, col_q, pool):
    M = z.shape[0]
    qw, kvw = HQ * HEAD, G * HEAD
    tm = _tile(M, (512, 256, 128, 64, 8))
    col_kv = col_q + qw
    out_shape = [jax.ShapeDtypeStruct((M, qw), BF16 if pool else F32),
                 jax.ShapeDtypeStruct((M, kvw), F32), jax.ShapeDtypeStruct((M, kvw), F32)]
    out_specs = [pl.BlockSpec((tm, qw), lambda i: (i, 0)),
                 pl.BlockSpec((tm, kvw), lambda i: (i, 0)), pl.BlockSpec((tm, kvw), lambda i: (i, 0))]
    if pool:
        out_shape.append(jax.ShapeDtypeStruct((M // BLOCK, 2 * kvw), F32))
        out_specs.append(pl.BlockSpec((tm // BLOCK, 2 * kvw), lambda i: (i, 0)))
    return pl.pallas_call(
        functools.partial(_nsa_prep_kernel, HQ=HQ, G=G, pool=pool),
        grid=(M // tm,),
        in_specs=[pl.BlockSpec((tm, qw), lambda i: (i, col_q // qw)),
                  pl.BlockSpec((tm, 2 * kvw), lambda i: (i, col_kv // (2 * kvw))),
                  pl.BlockSpec((tm, kvw), lambda i: (i, col_kv // kvw + 2)),
                  pl.BlockSpec((tm, kvw), lambda i: (i, col_kv // kvw + 4)),
                  pl.BlockSpec((BLOCK, 2 * kvw), lambda i: (0, 0)),
                  pl.BlockSpec((4, HEAD), lambda i: (0, 0))],
        out_specs=out_specs,
        out_shape=out_shape,
        compiler_params=_params("parallel"),
        name="nsa_prep",
    )(z, z, z, z, pool_w2, qk_g)


def _gate_col(gates, c):
    return jnp.sum(jnp.where(_iota(gates.shape, 1) == c, gates, 0.0), axis=1, keepdims=True)


def _top_blocks(imp, cur, n_top):
    nb = imp.shape[1]
    j = _iota(imp.shape, 1)
    forced = (j == 0) | (j == cur) | (j == cur - 1)
    causal = j <= cur
    score = jnp.where(causal, imp + jnp.where(forced, FORCE_BONUS, 0.0), NEG)
    rank = jnp.zeros(imp.shape, I32)
    for i in range(nb):
        si = score[:, i:i + 1]
        rank = rank + ((si > score) | ((si == score) & (i < j))).astype(I32)
    return (rank < n_top) & causal


def _nsa_prompt_kernel(q_ref, pool_ref, vpool_ref, ks_ref, vs_ref, kw_ref, vw_ref, bg_ref, o_ref,
                       sbias_ref, ss_ref, se_ref, sl_ref, wbias_ref, ws_ref, we_ref, wl_ref, *, tq, T, R, kc):
    g = pl.program_id(1)
    t0 = pl.program_id(2) * tq
    rows = R * tq
    nb = T // BLOCK
    q = jnp.concatenate([q_ref[:, r * HEAD:(r + 1) * HEAD] for r in range(R)], axis=0)

    kcb = pool_ref[0].astype(BF16)
    vcb = vpool_ref[0].astype(BF16)
    sc = _dot_nt(q, kcb)
    tpos = t0 + _iota((rows, nb), 0) % tq
    vis = (_iota((rows, nb), 1) + 1) * BLOCK <= tpos + 1
    p = jnp.where(vis, _softmax_masked(sc, vis), 0.0)
    o_cmp = _dot(p.astype(BF16), vcb)
    imp = p[0:tq]
    for r in range(1, R):
        imp = imp + p[r * tq:(r + 1) * tq]

    cur = (t0 + _iota((tq, nb), 0)) // BLOCK
    sel = jnp.where(_top_blocks(imp, cur, min(TOP_N, nb)), 1.0, 0.0).astype(BF16)

    def softmax_strips(s_ref, bias_ref, e_ref, l_ref, W):
        for r in range(R):
            for i in range(tq // STRIP):
                bs = slice(i * STRIP, (i + 1) * STRIP)
                rs = slice(r * tq + i * STRIP, r * tq + (i + 1) * STRIP)
                sr = s_ref[rs, :W] + bias_ref[bs, :W]
                e = jnp.exp(sr - jnp.max(sr, axis=1, keepdims=True))
                l_ref[rs, :] = jnp.sum(e, axis=1, keepdims=True)
                e_ref[rs, :W] = e.astype(BF16)

    wlen = min(T, WINDOW + tq)
    start = pl.multiple_of(jnp.clip(t0 - WINDOW, 0, T - wlen), HEAD)
    gates = _sigmoid(bg_ref[...])

    for c in range(T // kc):
        @pl.when((t0 + tq - 1) // kc == c)
        def _():
            W = (c + 1) * kc
            expand = jnp.where(_iota((nb, W), 1) // BLOCK == _iota((nb, W), 0), 1.0, 0.0).astype(BF16)
            allowed = (_dot(sel, expand) > 0.5) & (_iota((tq, W), 1) <= t0 + _iota((tq, W), 0))
            sbias_ref[:, :W] = jnp.where(allowed, 0.0, NEG)
            diff = (t0 + _iota((tq, wlen), 0)) - (start + _iota((tq, wlen), 1))
            wbias_ref[...] = jnp.where((diff >= 0) & (diff <= WINDOW), 0.0, NEG)

            ss_ref[:, :W] = _dot_nt(q, ks_ref[:W, :].astype(BF16))
            ws_ref[...] = _dot_nt(q, kw_ref[pl.ds(start, wlen), :].astype(BF16))
            softmax_strips(ss_ref, sbias_ref, se_ref, sl_ref, W)
            softmax_strips(ws_ref, wbias_ref, we_ref, wl_ref, wlen)
            o_sel = _dot(se_ref[:, :W], vs_ref[:W, :].astype(BF16)) / sl_ref[...]
            o_win = _dot(we_ref[...], vw_ref[pl.ds(start, wlen), :].astype(BF16)) / wl_ref[...]

            for r in range(R):
                c0 = (g * R + r) * 3
                sl = slice(r * tq, (r + 1) * tq)
                y = (_gate_col(gates, c0) * o_cmp[sl] + _gate_col(gates, c0 + 1) * o_sel[sl]
                     + _gate_col(gates, c0 + 2) * o_win[sl])
                o_ref[:, r * HEAD:(r + 1) * HEAD] = y.astype(o_ref.dtype)


def nsa_prompt(qn, pooled, ksn, kwn, z, zbg, Bn, T, HQ, G, col_kv):
    R = HQ // G
    tq = _tile(T, (128, 64))
    nq = T // tq
    nb = T // BLOCK
    kvw = G * HEAD
    pooled3 = pooled.reshape(Bn, nb, 2 * kvw)
    cv = col_kv // HEAD
    kc = _tile(T, (512, 256, 128, 64))
    seq = lambda off: pl.BlockSpec((T, HEAD), lambda b, g, i: (b, off + g))

    def branch_scratch(w):
        return [pltpu.VMEM((tq, w), F32), pltpu.VMEM((R * tq, w), F32), pltpu.VMEM((R * tq, w), BF16),
                pltpu.VMEM((R * tq, 1), F32)]

    return pl.pallas_call(
        functools.partial(_nsa_prompt_kernel, tq=tq, T=T, R=R, kc=kc),
        grid=(Bn, G, nq),
        in_specs=[pl.BlockSpec((tq, R * HEAD), lambda b, g, i: (b * nq + i, g)),
                  pl.BlockSpec((1, nb, HEAD), lambda b, g, i: (b, 0, g)),
                  pl.BlockSpec((1, nb, HEAD), lambda b, g, i: (b, 0, G + g)),
                  seq(0), seq(cv + 3 * G), seq(0), seq(cv + 5 * G),
                  pl.BlockSpec((tq, HEAD), lambda b, g, i: (b * nq + i, 0))],
        out_specs=pl.BlockSpec((tq, R * HEAD), lambda b, g, i: (b * nq + i, g)),
        out_shape=jax.ShapeDtypeStruct((Bn * T, HQ * HEAD), BF16),
        scratch_shapes=branch_scratch(T) + branch_scratch(min(T, WINDOW + tq)),
        compiler_params=_params("parallel", "parallel", "arbitrary"),
        name="nsa_prompt",
    )(qn, pooled3, pooled3, ksn, z, kwn, z, zbg)


def _pool_pages_kernel(pt_ref, *refs, pp):
    ck_refs, cv_refs = refs[:pp], refs[pp:2 * pp]
    pw_ref, pk_ref, pv_ref = refs[2 * pp:]
    per = PAGE_SIZE // BLOCK
    for q in range(pp):
        for half in range(per):
            sl = slice(half * BLOCK, (half + 1) * BLOCK)
            pk_ref[0, q * per + half] = jnp.sum(ck_refs[q][0, 0, sl] * pw_ref[0], axis=0)
            pv_ref[0, q * per + half] = jnp.sum(cv_refs[q][0, 0, sl] * pw_ref[1], axis=0)


def pool_pages(cache_k, cache_v, page_table, pool_w3, layer):
    Bs, n_pages = page_table.shape
    G = cache_k.shape[3]
    per = PAGE_SIZE // BLOCK
    pp = _tile(n_pages, (16, 8, 4, 2, 1))
    page = lambda q: pl.BlockSpec((1, 1, PAGE_SIZE, G, HEAD),
                                  lambda b, p, pt: (layer, pt[b * n_pages + p * pp + q], 0, 0, 0))
    out = pl.BlockSpec((1, pp * per, G, HEAD), lambda b, p, pt: (b, p, 0, 0))
    return pl.pallas_call(
        functools.partial(_pool_pages_kernel, pp=pp),
        grid_spec=pltpu.PrefetchScalarGridSpec(
            num_scalar_prefetch=1,
            grid=(Bs, n_pages // pp),
            in_specs=[page(q) for q in range(pp)] * 2
                     + [pl.BlockSpec((2, BLOCK, G, HEAD), lambda b, p, pt: (0, 0, 0, 0))],
            out_specs=[out, out]),
        out_shape=[jax.ShapeDtypeStruct((Bs, n_pages * per, G, HEAD), F32)] * 2,
        compiler_params=_params("parallel", "parallel"),
        name="pool_pages",
    )(page_table.reshape(-1), *([cache_k] * pp), *([cache_v] * pp), pool_w3)


def _head_rows(qrow, g, R):
    rows = [qrow[:, (g * R + r) * HEAD:(g * R + r + 1) * HEAD] for r in range(R)]
    rows.append(jnp.zeros((8 - R, HEAD), F32))
    return jnp.concatenate(rows, axis=0).astype(BF16)


def _sample_cmp_win_kernel(q_ref, pk_ref, pv_ref, cwk_ref, cwv_ref, kwn_ref, vw_ref, bg_ref, g_ref,
                           pc_ref, pw_ref, idx_ref, *, G, R, past):
    nbp = pk_ref.shape[1]
    wb = cwk_ref.shape[2]
    qpos = past
    cur = qpos // BLOCK
    nlane = (cur // HEAD + 1) * HEAD
    qrow = q_ref[0]
    gates = _sigmoid(bg_ref[0])
    qk_g = g_ref[...]
    for g in range(G):
        sl = slice(g * HEAD, (g + 1) * HEAD)
        qg = _head_rows(qrow, g, R)

        kcb = _rms(pk_ref[0, :, g, :], qk_g[1:2]).astype(BF16)
        sc = _dot_nt(qg, kcb)
        vis = (_iota((8, nbp), 1) + 1) * BLOCK <= qpos + 1
        p = jnp.where(vis, _softmax_masked(sc, vis), 0.0)
        o_cmp = _dot(p.astype(BF16), pv_ref[0, :, g, :].astype(BF16))
        imp = p[0:1]
        for r in range(1, R):
            imp = imp + p[r:r + 1]

        imp = jnp.concatenate([imp, jnp.zeros((1, nlane - nbp), F32)], axis=1)
        j = _iota((1, nlane), 1)
        forced = (j == 0) | (j == cur) | (j == cur - 1)
        score = jnp.where(j <= cur, imp + jnp.where(forced, FORCE_BONUS, 0.0), NEG)
        s_col = _column(score, nlane)
        ii = _iota((nlane, nlane), 0)
        jj = _iota((nlane, nlane), 1)
        ahead = (s_col > score) | ((s_col == score) & (ii < jj))
        rank = jnp.sum(ahead.astype(F32), axis=0, keepdims=True)
        rank_col = _column(rank, nlane)
        hit = rank_col == _iota((nlane, HEAD), 1).astype(F32)
        idx = jnp.sum(jnp.where(hit, _iota((nlane, HEAD), 0).astype(F32), 0.0), axis=0, keepdims=True)
        idx_ref[0, g:g + 1, :] = idx.astype(I32)

        kwc = cwk_ref[0, 0, :, g, :].astype(BF16)
        sw = _dot_nt(qg, kwc)
        diff = qpos - (past - wb + _iota((8, wb), 1))
        sw = jnp.where((diff >= 0) & (diff <= WINDOW), sw, NEG)
        kn = kwn_ref[0][:, sl].astype(BF16).astype(F32)
        vn = vw_ref[0][:, sl].astype(BF16).astype(F32)
        sn = jnp.sum(qg.astype(F32) * kn, axis=1, keepdims=True)
        m = jnp.maximum(jnp.max(sw, axis=1, keepdims=True), sn)
        e = jnp.exp(sw - m)
        en = jnp.exp(sn - m)
        den = jnp.sum(e, axis=1, keepdims=True) + en
        o_win = (_dot((e / den).astype(BF16), cwv_ref[0, 0, :, g, :].astype(BF16))
                 + (en / den).astype(BF16).astype(F32) * vn)

        for r in range(R):
            c0 = (g * R + r) * 3
            hs = slice((g * R + r) * HEAD, (g * R + r + 1) * HEAD)
            pc_ref[0, :, hs] = gates[:, c0:c0 + 1] * o_cmp[r:r + 1]
            pw_ref[0, :, hs] = gates[:, c0 + 2:c0 + 3] * o_win[r:r + 1]


def sample_cmp_win(qn3, pk, pv, cache_win_k, cache_win_v, kwn3, z3, zbg3, qk_g, layer, G, R, past, col_vw):
    Bs = qn3.shape[0]
    depth, _, wb = cache_win_k.shape[:3]
    kvw = G * HEAD
    qw = G * R * HEAD
    nbp = pk.shape[1]
    row = lambda w, j=0: pl.BlockSpec((1, 1, w), lambda b: (b, 0, j))
    return pl.pallas_call(
        functools.partial(_sample_cmp_win_kernel, G=G, R=R, past=past),
        grid=(Bs,),
        in_specs=[row(qw),
                  pl.BlockSpec((1, nbp, G, HEAD), lambda b: (b, 0, 0, 0)),
                  pl.BlockSpec((1, nbp, G, HEAD), lambda b: (b, 0, 0, 0)),
                  pl.BlockSpec((1, 1, wb, G, HEAD), lambda b: (layer, b, 0, 0, 0)),
                  pl.BlockSpec((1, 1, wb, G, HEAD), lambda b: (layer, b, 0, 0, 0)),
                  row(kvw), row(kvw, col_vw // kvw), row(HEAD),
                  pl.BlockSpec((4, HEAD), lambda b: (0, 0))],
        out_specs=[row(qw), row(qw), pl.BlockSpec((1, G, HEAD), lambda b: (b, 0, 0))],
        out_shape=[jax.ShapeDtypeStruct((Bs, 1, qw), F32), jax.ShapeDtypeStruct((Bs, 1, qw), F32),
                   jax.ShapeDtypeStruct((Bs, G, HEAD), I32)],
        compiler_params=_params("parallel"),
        name="sample_cmp_win",
    )(qn3, pk, pv, cache_win_k, cache_win_v, kwn3, z3, zbg3, qk_g)


def _sample_select_kernel(idx_ref, pt_ref, q_ref, *refs, G, R, n_top, past):
    ck_refs, cv_refs = refs[:G], refs[G:2 * G]
    kn_ref, vn_ref, bg_ref, pc_ref, pw_ref, o_ref, m_ref, l_ref, acc_ref = refs[2 * G:]
    b, n = pl.program_id(0), pl.program_id(1)
    qpos = past

    @pl.when(n == 0)
    def _():
        m_ref[...] = jnp.full_like(m_ref, NEG)
        l_ref[...] = jnp.zeros_like(l_ref)
        acc_ref[...] = jnp.zeros_like(acc_ref)

    qrow = q_ref[0]
    for g in range(G):
        sl = slice(g * HEAD, (g + 1) * HEAD)
        j = idx_ref[(b * G + g) * n_top + n]
        qg = _head_rows(qrow, g, R)
        s = _dot_nt(qg, ck_refs[g][0, 0, :, g, :].astype(BF16))
        pos = j * BLOCK + _iota((8, BLOCK), 1)
        s = jnp.where((pos < past) & (pos <= qpos), s, NEG)
        kn = kn_ref[0][:, sl].astype(BF16).astype(F32)
        vn = vn_ref[0][:, sl].astype(BF16).astype(F32)
        sn = jnp.sum(qg.astype(F32) * kn, axis=1, keepdims=True)
        sn = jnp.where(j == past // BLOCK, sn, NEG)

        m_old = m_ref[g][:, 0:1]
        m_new = jnp.maximum(m_old, jnp.maximum(jnp.max(s, axis=1, keepdims=True), sn))
        alpha = jnp.exp(m_old - m_new)
        e = jnp.exp(s - m_new)
        en = jnp.exp(sn - m_new)
        l_new = alpha * l_ref[g][:, 0:1] + jnp.sum(e, axis=1, keepdims=True) + en
        acc = (alpha * acc_ref[g] + _dot(e.astype(BF16), cv_refs[g][0, 0, :, g, :].astype(BF16))
               + en.astype(BF16).astype(F32) * vn)
        m_ref[g] = jnp.broadcast_to(m_new, (8, HEAD))
        l_ref[g] = jnp.broadcast_to(l_new, (8, HEAD))
        acc_ref[g] = acc

    @pl.when(n == n_top - 1)
    def _():
        gates = _sigmoid(bg_ref[0])
        for g in range(G):
            o_sel = acc_ref[g] / l_ref[g][:, 0:1]
            for r in range(R):
                c = (g * R + r) * 3 + 1
                hs = slice((g * R + r) * HEAD, (g * R + r + 1) * HEAD)
                o_ref[0, :, hs] = ((pc_ref[0, :, hs] + gates[:, c:c + 1] * o_sel[r:r + 1])
                                   + pw_ref[0, :, hs]).astype(o_ref.dtype)


def sample_select(idx, page_table, qn3, cache_k, cache_v, ksn3, z3, zbg3, pc, pw, layer, G, R, past, col_vs):
    Bs, n_pages = page_table.shape
    n_top = idx.shape[-1]
    depth, n_pool = cache_k.shape[:2]
    kvw, qw = G * HEAD, G * R * HEAD
    per = PAGE_SIZE // BLOCK
    ck = cache_k.reshape(depth, n_pool * per, BLOCK, G, HEAD)
    cv = cache_v.reshape(depth, n_pool * per, BLOCK, G, HEAD)
    nbp = past // BLOCK

    def blk(g):
        def index(b, n, idx_ref, pt_ref):
            jc = jnp.minimum(idx_ref[(b * G + g) * n_top + n], nbp - 1)
            return (layer, pt_ref[b * n_pages + jc // per] * per + jc % per, 0, 0, 0)
        return pl.BlockSpec((1, 1, BLOCK, G, HEAD), index)

    row = lambda w, j=0: pl.BlockSpec((1, 1, w), lambda b, n, i_, p_: (b, 0, j))
    return pl.pallas_call(
        functools.partial(_sample_select_kernel, G=G, R=R, n_top=n_top, past=past),
        grid_spec=pltpu.PrefetchScalarGridSpec(
            num_scalar_prefetch=2,
            grid=(Bs, n_top),
            in_specs=[row(qw)] + [blk(g) for g in range(G)] * 2
                     + [row(kvw), row(kvw, col_vs // kvw), row(HEAD), row(qw), row(qw)],
            out_specs=row(qw),
            scratch_shapes=[pltpu.VMEM((G, 8, HEAD), F32)] * 3),
        out_shape=jax.ShapeDtypeStruct((Bs, 1, qw), BF16),
        compiler_params=_params("parallel", "arbitrary"),
        name="sample_select",
    )(idx.reshape(-1), page_table.reshape(-1), qn3, *([ck] * G), *([cv] * G), ksn3, z3, zbg3, pc, pw)


def _cast_kernel(x_ref, o_ref):
    o_ref[...] = x_ref[0].astype(o_ref.dtype)


def cast_layer_bf16(w, layer):
    _, R, C = w.shape
    tr = next((t for t in (512, 256, 128, 64, 32, 16) if R % t == 0 and t * C * 4 <= CAST_BLOCK_BYTES), R)
    return pl.pallas_call(
        _cast_kernel,
        grid=(R // tr,),
        in_specs=[pl.BlockSpec((1, tr, C), lambda i: (layer, i, 0))],
        out_specs=pl.BlockSpec((tr, C), lambda i: (i, 0)),
        out_shape=jax.ShapeDtypeStruct((R, C), BF16),
        compiler_params=_params("parallel"),
        name="cast_layer_bf16",
    )(w)


def _cast_cut_kernel(a_ref, b_ref, o_ref, *, nj0, cut):
    j = pl.program_id(0)
    tn = o_ref.shape[1]

    @pl.when(j < nj0)
    def _():
        o_ref[...] = a_ref[0].T.astype(o_ref.dtype)

    @pl.when(j >= nj0)
    def _():
        x = jnp.concatenate([a_ref[0], b_ref[0]], axis=0)
        o_ref[...] = x[cut:cut + tn].T.astype(o_ref.dtype)


def cast_cut_bf16(w_t, layer, c0, cut):
    _, C, R = w_t.shape
    N = C - cut
    tn = next(t for t in (256, 128) if c0 % t == 0 and N % t == 0)
    assert cut < HEAD and cut % 8 == 0
    return pl.pallas_call(
        functools.partial(_cast_cut_kernel, nj0=c0 // tn, cut=cut),
        grid=(N // tn,),
        in_specs=[pl.BlockSpec((1, tn, R), lambda j: (layer, j, 0)),
                  pl.BlockSpec((1, HEAD, R), lambda j: (layer, (j + 1) * (tn // HEAD), 0))],
        out_specs=pl.BlockSpec((R, tn), lambda j: (0, j)),
        out_shape=jax.ShapeDtypeStruct((R, N), BF16),
        compiler_params=_params("parallel"),
        name="cast_cut_bf16",
    )(w_t, w_t)


def _cast_t_kernel(x_ref, o_ref):
    o_ref[...] = x_ref[0].T.astype(o_ref.dtype)


def cast_cols_bf16(w_t, layer, col_block):
    R = w_t.shape[2]
    return pl.pallas_call(
        _cast_t_kernel,
        grid=(1,),
        in_specs=[pl.BlockSpec((1, HEAD, R), lambda i: (layer, col_block, 0))],
        out_specs=pl.BlockSpec((R, HEAD), lambda i: (0, 0)),
        out_shape=jax.ShapeDtypeStruct((R, HEAD), BF16),
        compiler_params=_params("arbitrary"),
        name="cast_cols_bf16",
    )(w_t)


def _layer_weights(l, norm_g, w_in, qk_g, cmp_pool, w_pa, w_pb, w_o, f1_gu, f1_dn, f2_gu, f2_dn, H, HQ, G):
    kw, qw, kvw = H * HEAD, HQ * HEAD, G * HEAD
    D = w_in.shape[1]
    c_bg = 4 * kw + qw + 6 * kvw
    n_bg = 3 * HQ
    w_in_t = jnp.transpose(w_in, (0, 2, 1))
    w_main = cast_cut_bf16(w_in_t, l, c_bg, n_bg)
    w_bg = cast_cols_bf16(w_in_t, l, c_bg // HEAD)
    pool_w2 = jnp.repeat(jnp.transpose(cmp_pool[l], (1, 0, 2)).reshape(BLOCK, 2 * G), HEAD, axis=1)
    pool_w3 = jnp.broadcast_to(cmp_pool[l][..., None], cmp_pool.shape[1:] + (HEAD,))
    cast = lambda w: cast_layer_bf16(w, l)
    return dict(norm_g=norm_g[l], w_main=w_main, w_bg=w_bg, qk_g=qk_g[l], pool_w2=pool_w2, pool_w3=pool_w3,
                w_pa=cast(w_pa), w_pb=cast(w_pb), w_o=cast(w_o),
                ffn1=(f1_gu, cast(f1_dn)), ffn2=(f2_gu, cast(f2_dn)), c_bg=c_bg, D=D, l=l)


def _ffn(xp, xs, g, ffn_w, l):
    gu, dn = ffn_w
    hp, hs = matmul_swiglu(rmsnorm_rows(xp, g), rmsnorm_rows(xs, g), gu, l)
    return matmul(hp, dn, res=xp, scale=0.5), matmul(hs, dn, res=xs, scale=0.5)


def _layer(xp, xs, lw, prompt_mixers, sample_mixers):
    hp, hs = _ffn(xp, xs, lw["norm_g"][0], lw["ffn1"], lw["l"])
    out, states = [], []
    for h, mixers in ((hp, prompt_mixers), (hs, sample_mixers)):
        u = rmsnorm_rows(h, lw["norm_g"][1])
        z = matmul(u, lw["w_main"])
        zbg = matmul(u, lw["w_bg"])
        ya, yb, state = mixers(z, zbg)
        m = merge_project(ya, yb, lw["w_pa"], lw["w_pb"], z, lw["c_bg"], lw["c_bg"] + lw["D"])
        out.append(matmul(m, lw["w_o"], res=h, scale=1.0))
        states.append(state)
    yp, ys = _ffn(out[0], out[1], lw["norm_g"][2], lw["ffn2"], lw["l"])
    return yp, ys, states[0], states[1]


def kernel(x_prompt, x_sample, state_hgrn, cache_cmp_k, cache_cmp_v, cache_sel_k, cache_sel_v, cache_win_k, cache_win_v, page_table, norm_g, w_in, hgrn_lb, hgrn_out_g, qk_g, cmp_pool, w_proj_a, w_proj_b, w_out, ffn1_gu, ffn1_down, ffn2_gu, ffn2_down):
    Bn, T, D = x_prompt.shape
    Bs, Tn, _ = x_sample.shape
    depth = w_in.shape[0]
    H = hgrn_lb.shape[1] // HEAD
    HQ = w_proj_b.shape[1] // HEAD
    G = cache_cmp_k.shape[3]
    R = HQ // G
    kw, qw, kvw = H * HEAD, HQ * HEAD, G * HEAD
    past = page_table.shape[1] * PAGE_SIZE
    wb = cache_win_k.shape[2]
    assert Tn == 1 and T % HGRN_CHUNK == 0 and past % PAGE_SIZE == 0 and R <= 8
    assert hgrn_out_g.shape[1] == HEAD and w_proj_a.shape[1] == kw and wb <= WINDOW and wb <= past
    col_q = 4 * kw
    col_kv = col_q + qw

    p = jax.nn.softmax(hgrn_lb.astype(F32), axis=0)
    lbs = jnp.cumsum(p, axis=0) - p[0]

    y_p = x_prompt.reshape(Bn * T, D)
    y_s = x_sample.reshape(Bs, D)
    st_p, st_s = [], []
    for l in range(depth):
        lw = _layer_weights(l, norm_g, w_in, qk_g, cmp_pool, w_proj_a, w_proj_b, w_out,
                            ffn1_gu, ffn1_down, ffn2_gu, ffn2_down, H, HQ, G)

        def prompt_mixers(z, zbg):
            ya, s_new = hgrn_prompt(z, lbs[l], hgrn_out_g[l], Bn, T, H)
            qn, ksn, kwn, pooled = nsa_prep(z, lw["pool_w2"], lw["qk_g"], HQ, G, col_q, True)
            yb = nsa_prompt(qn, pooled, ksn, kwn, z, zbg, Bn, T, HQ, G, col_kv)
            kv = lambda i: z[:, col_kv + i * kvw:col_kv + (i + 1) * kvw].reshape(Bn, T, G, HEAD)
            tail = lambda a: a[:, T - min(WINDOW, T):]
            return ya, yb, (s_new, kv(0), kv(1), ksn.reshape(Bn, T, G, HEAD), kv(3),
                            tail(kwn.reshape(Bn, T, G, HEAD)), tail(kv(5)))

        def sample_mixers(z, zbg):
            z3, zbg3 = z.reshape(Bs, 1, -1), zbg.reshape(Bs, 1, HEAD)
            ya, s_new = hgrn_sample(z3, lbs[l], hgrn_out_g[l], state_hgrn, l, H)
            qn, ksn, kwn = nsa_prep(z, lw["pool_w2"], lw["qk_g"], HQ, G, col_q, False)
            qn3, ksn3, kwn3 = qn.reshape(Bs, 1, qw), ksn.reshape(Bs, 1, kvw), kwn.reshape(Bs, 1, kvw)
            pk, pv = pool_pages(cache_cmp_k, cache_cmp_v, page_table, lw["pool_w3"], l)
            pc, pw, idx = sample_cmp_win(qn3, pk, pv, cache_win_k, cache_win_v, kwn3, z3, zbg3, lw["qk_g"],
                                         l, G, R, past, col_kv + 5 * kvw)
            n_top = min(TOP_N, -(-(past + Tn) // BLOCK))
            yb = sample_select(idx[:, :, :n_top], page_table, qn3, cache_sel_k, cache_sel_v, ksn3, z3, zbg3,
                               pc, pw, l, G, R, past, col_kv + 3 * kvw)
            kv = lambda i: z[:, col_kv + i * kvw:col_kv + (i + 1) * kvw].reshape(Bs, 1, G, HEAD)
            win_k = jnp.concatenate([cache_win_k[l], kwn.reshape(Bs, 1, G, HEAD)], axis=1)[:, -wb:]
            win_v = jnp.concatenate([cache_win_v[l], kv(5)], axis=1)[:, -wb:]
            return ya.reshape(Bs, kw), yb.reshape(Bs, qw), (
                s_new, kv(0), kv(1), ksn.reshape(Bs, 1, G, HEAD), kv(3), win_k, win_v)

        y_p, y_s, sp, ss = _layer(y_p, y_s, lw, prompt_mixers, sample_mixers)
        st_p.append(sp)
        st_s.append(ss)

    stack = lambda states, i: jnp.stack([s[i] for s in states])
    return ((y_p.reshape(Bn, T, D), y_s.reshape(Bs, Tn, D))
            + tuple(stack(st_p, i) for i in range(7)) + tuple(stack(st_s, i) for i in range(7)))
```

```python
import functools

import jax
import jax.numpy as jnp
from jax import lax
from jax.experimental import pallas as pl
from jax.experimental.pallas import tpu as pltpu

F32 = jnp.float32
BF16 = jnp.bfloat16
I32 = jnp.int32

HEAD = 128
BLOCK = 64
TOP_N = 16
WINDOW = 512
PAGE_SIZE = 128
EPS = 1e-6
NEG = -1e30
LB_FLOOR = 1e-30
FORCE_BONUS = 1e4
ATTN_SCALE = HEAD ** -0.5
HGRN_CHUNK = 128
DIAG = 4
STRIP = 16
CAST_BLOCK_BYTES = 8 * 1024 * 1024
VMEM_BYTES_V7X = 64 * 1024 * 1024
VMEM_LIMIT = VMEM_BYTES_V7X - 8 * 1024 * 1024


def _params(*sem):
    return pltpu.CompilerParams(dimension_semantics=sem, vmem_limit_bytes=VMEM_LIMIT)


def _tile(dim, candidates):
    for c in candidates:
        if dim % c == 0:
            return c
    return dim


def _dot(a, b):
    return jnp.dot(a, b, preferred_element_type=F32)


def _dot_nt(a, b):
    return lax.dot_general(a, b, (((1,), (1,)), ((), ())), preferred_element_type=F32)


def _dot_tn(a, b):
    return lax.dot_general(a, b, (((0,), (0,)), ((), ())), preferred_element_type=F32)


def _sigmoid(x):
    return jax.nn.sigmoid(x)


def _silu(x):
    return x * jax.nn.sigmoid(x)


def _log_sigmoid(x):
    return jnp.minimum(x, 0.0) - jnp.log1p(jnp.exp(-jnp.abs(x)))


def _logaddexp(a, b):
    return jnp.maximum(a, b) + jnp.log1p(jnp.exp(-jnp.abs(a - b)))


def _rms(x, g):
    ms = jnp.mean(x * x, axis=-1, keepdims=True)
    return x * lax.rsqrt(ms + EPS) * g


def _softmax_masked(s, mask):
    s = jnp.where(mask, s, NEG)
    m = jnp.max(s, axis=-1, keepdims=True)
    e = jnp.exp(s - m)
    return e / jnp.sum(e, axis=-1, keepdims=True)


def _iota(shape, dim):
    return lax.broadcasted_iota(I32, shape, dim)


def _rmsnorm_kernel(x_ref, g_ref, o_ref):
    o_ref[...] = _rms(x_ref[...], g_ref[...]).astype(o_ref.dtype)


def rmsnorm_rows(x, g):
    M, D = x.shape
    tm = _tile(M, (256, 128, 64, 32, 16, 8))
    return pl.pallas_call(
        _rmsnorm_kernel,
        grid=(M // tm,),
        in_specs=[pl.BlockSpec((tm, D), lambda i: (i, 0)), pl.BlockSpec((1, D), lambda i: (0, 0))],
        out_specs=pl.BlockSpec((tm, D), lambda i: (i, 0)),
        out_shape=jax.ShapeDtypeStruct((M, D), BF16),
        compiler_params=_params("parallel"),
        name="rmsnorm_rows",
    )(x, g.reshape(1, D))


def _mm_kernel(x_ref, w_ref, *rest, nk, scale, has_res):
    if has_res:
        r_ref, o_ref = rest[0], rest[1]
        scratch = rest[2:]
    else:
        r_ref, o_ref = None, rest[0]
        scratch = rest[1:]

    def finish(acc):
        if has_res:
            o_ref[...] = (r_ref[...] + scale * acc).astype(o_ref.dtype)
        else:
            o_ref[...] = acc.astype(o_ref.dtype)

    if nk == 1:
        finish(_dot(x_ref[...], w_ref[...]))
    else:
        acc_ref = scratch[0]
        k = pl.program_id(2)

        @pl.when(k == 0)
        def _():
            acc_ref[...] = jnp.zeros_like(acc_ref)

        acc_ref[...] += _dot(x_ref[...], w_ref[...])

        @pl.when(k == nk - 1)
        def _():
            finish(acc_ref[...])


def matmul(x, w, *, res=None, scale=1.0):
    M, K = x.shape
    N = w.shape[1]
    has_res = res is not None
    tm = _tile(M, (1024, 512, 256, 128, 64, 32, 16, 8))
    if K <= 4096:
        tk = K
        tn = _tile(N, (512, 256, 128) if has_res else (1024, 512, 256, 128))
    else:
        tk = next((K // d for d in (2, 3, 4, 5, 6, 8) if K % (d * HEAD) == 0), HEAD)
        tn = _tile(N, (512, 256, 128))
    nk = K // tk
    in_specs = [pl.BlockSpec((tm, tk), lambda i, j, k: (i, k)), pl.BlockSpec((tk, tn), lambda i, j, k: (k, j))]
    args = [x, w]
    if has_res:
        in_specs.append(pl.BlockSpec((tm, tn), lambda i, j, k: (i, j)))
        args.append(res)
    return pl.pallas_call(
        functools.partial(_mm_kernel, nk=nk, scale=scale, has_res=has_res),
        grid=(M // tm, N // tn, nk),
        in_specs=in_specs,
        out_specs=pl.BlockSpec((tm, tn), lambda i, j, k: (i, j)),
        out_shape=jax.ShapeDtypeStruct((M, N), F32),
        scratch_shapes=[pltpu.VMEM((tm, tn), F32)] if nk > 1 else [],
        compiler_params=_params("parallel", "parallel", "arbitrary"),
        name="matmul",
    )(*args)


def _swiglu_kernel(x_ref, xs_ref, wg_ref, wu_ref, o_ref, os_ref):
    i, j = pl.program_id(0), pl.program_id(1)
    wg = wg_ref[0].astype(BF16)
    wu = wu_ref[0].astype(BF16)
    x = x_ref[...]
    o_ref[...] = (_silu(_dot(x, wg)) * _dot(x, wu)).astype(o_ref.dtype)

    @pl.when(i == 0)
    def _():
        xs = xs_ref[...]
        os_ref[j] = (_silu(_dot(xs, wg)) * _dot(xs, wu)).astype(os_ref.dtype)


def matmul_swiglu(x, xs, w_gu, layer):
    M, K = x.shape
    Ms = xs.shape[0]
    F = w_gu.shape[2] // 2
    tn = _tile(F, (256, 128))
    tm = _tile(M, (2048, 1024, 512, 256, 128, 64, 32, 16, 8))
    nj = F // tn
    o, os = pl.pallas_call(
        _swiglu_kernel,
        grid=(M // tm, nj),
        in_specs=[pl.BlockSpec((tm, K), lambda i, j: (i, 0), pipeline_mode=pl.Buffered(1)),
                  pl.BlockSpec((Ms, K), lambda i, j: (0, 0)),
                  pl.BlockSpec((1, K, tn), lambda i, j: (layer, 0, j)),
                  pl.BlockSpec((1, K, tn), lambda i, j: (layer, 0, j + nj))],
        out_specs=[pl.BlockSpec((tm, tn), lambda i, j: (i, j)),
                   pl.BlockSpec((nj, Ms, tn), lambda i, j: (0, 0, 0))],
        out_shape=[jax.ShapeDtypeStruct((M, F), BF16), jax.ShapeDtypeStruct((nj, Ms, tn), BF16)],
        compiler_params=_params("arbitrary", "arbitrary"),
        name="matmul_swiglu",
    )(x, xs, w_gu, w_gu)
    return o, jnp.transpose(os, (1, 0, 2)).reshape(Ms, F)


def _merge_kernel(ya_ref, yb_ref, wa_ref, wb_ref, ga_ref, gb_ref, o_ref):
    pa = _dot(ya_ref[...], wa_ref[...])
    pb = _dot(yb_ref[...], wb_ref[...])
    o_ref[...] = (_sigmoid(ga_ref[...]) * pa + _sigmoid(gb_ref[...]) * pb).astype(o_ref.dtype)


def merge_project(ya, yb, wa, wb, z, col_a, col_b):
    M, Ka = ya.shape
    Kb = yb.shape[1]
    N = wa.shape[1]
    tm = _tile(M, (1024, 512, 256, 128, 64, 32, 16, 8))
    tn = _tile(N, (512, 256, 128))
    ja, jb = col_a // tn, col_b // tn
    return pl.pallas_call(
        _merge_kernel,
        grid=(M // tm, N // tn),
        in_specs=[pl.BlockSpec((tm, Ka), lambda i, j: (i, 0)),
                  pl.BlockSpec((tm, Kb), lambda i, j: (i, 0)),
                  pl.BlockSpec((Ka, tn), lambda i, j: (0, j)),
                  pl.BlockSpec((Kb, tn), lambda i, j: (0, j)),
                  pl.BlockSpec((tm, tn), lambda i, j: (i, j + ja)),
                  pl.BlockSpec((tm, tn), lambda i, j: (i, j + jb))],
        out_specs=pl.BlockSpec((tm, tn), lambda i, j: (i, j)),
        out_shape=jax.ShapeDtypeStruct((M, N), BF16),
        compiler_params=_params("parallel", "parallel"),
        name="merge_project",
    )(ya, yb, wa, wb, z, z)


def _split3(x):
    hi = x.astype(BF16)
    r1 = x - hi.astype(F32)
    mid = r1.astype(BF16)
    lo = (r1 - mid.astype(F32)).astype(BF16)
    return hi, mid, lo


def _hgrn_gates(zq, zf, lb):
    q = _silu(zq)
    log_f = _logaddexp(jnp.log(jnp.maximum(lb, LB_FLOOR)), jnp.log1p(-lb) + _log_sigmoid(zf))
    k = (1.0 - lb) * _sigmoid(-zf)
    return q, log_f, k


def _hgrn_masks():
    C = HGRN_CHUNK
    row = _iota((C, C), 0)
    col = _iota((C, C), 1)
    tri = jnp.where(col <= row, 1.0, 0.0).astype(BF16)
    levels = []
    h = C // 2
    while h >= DIAG:
        P = 2 * h
        levels.append((h, (row & (P - 1)) >= h, (row // P) == (col // P) if P < C else None))
        h //= 2
    rloc = row & (DIAG - 1)
    diags = [(rloc >= d, (col == row - d) & (rloc >= d)) for d in range(DIAG)]
    return tri, levels, diags


def _hgrn_chunk(zq, zf, v, zg, lb, og, st, masks):
    C = HGRN_CHUNK
    tri, levels, diags = masks
    q, log_f, k = _hgrn_gates(zq, zf, lb)
    vb = v.astype(BF16)

    hi, mid, lo = _split3(log_f)
    b = _dot(tri, hi) + _dot(tri, mid) + _dot(tri, lo)

    a = jnp.zeros((C, C), F32)
    for h, second, same_parent in levels:
        P = 2 * h
        bref = jnp.concatenate(
            [jnp.broadcast_to(b[m * P + h - 1:m * P + h, :], (P, HEAD)) for m in range(C // P)], axis=0)
        qt = q * jnp.exp(jnp.where(second, b - bref, NEG))
        kt = k * jnp.exp(jnp.where(second, NEG, bref - b))
        lvl = _dot_nt(qt.astype(BF16), kt.astype(BF16))
        a = a + (lvl if same_parent is None else jnp.where(same_parent, lvl, 0.0))

    for d, (in_block, on_diag) in enumerate(diags):
        if d == 0:
            p = q * k
        else:
            bs = pltpu.roll(b, d, 0)
            ks = pltpu.roll(k, d, 0)
            p = q * jnp.exp(jnp.where(in_block, b - bs, NEG)) * ks
        a = jnp.where(on_diag, jnp.sum(p, axis=1, keepdims=True), a)

    o = _dot_nt((q * jnp.exp(b)).astype(BF16), st.astype(BF16)) + _dot(a.astype(BF16), vb)
    b_last = b[C - 1:C, :]
    kd = k * jnp.exp(b_last - b)
    st_new = st * jnp.exp(b_last) + _dot_tn(vb, kd.astype(BF16))
    return _rms(o, og) * _silu(zg), st_new


def _hgrn_prompt_kernel(q_ref, f_ref, v_ref, g_ref, lb_ref, og_ref, y_ref, s_ref, st_ref, *, nc, hb):
    c = pl.program_id(2)

    @pl.when(c == 0)
    def _():
        st_ref[...] = jnp.zeros_like(st_ref)

    masks = _hgrn_masks()
    for i in range(hb):
        sl = slice(i * HEAD, (i + 1) * HEAD)
        y, st_new = _hgrn_chunk(q_ref[:, sl], f_ref[:, sl], v_ref[:, sl], g_ref[:, sl], lb_ref[:, sl],
                                og_ref[...], st_ref[i], masks)
        st_ref[i] = st_new
        y_ref[:, sl] = y.astype(y_ref.dtype)

    @pl.when(c == nc - 1)
    def _():
        for i in range(hb):
            s_ref[0, i] = st_ref[i].T


def hgrn_prompt(z, lb, out_g, Bn, T, H):
    C = HGRN_CHUNK
    nc = T // C
    hb = _tile(H, (16, 8, 4, 2, 1))
    w = hb * HEAD
    blk = lambda off: pl.BlockSpec((C, w), lambda b, h, c: (b * nc + c, off + h))
    return pl.pallas_call(
        functools.partial(_hgrn_prompt_kernel, nc=nc, hb=hb),
        grid=(Bn, H // hb, nc),
        in_specs=[blk(0), blk(H // hb), blk(2 * H // hb), blk(3 * H // hb),
                  pl.BlockSpec((1, w), lambda b, h, c: (0, h)),
                  pl.BlockSpec((1, HEAD), lambda b, h, c: (0, 0))],
        out_specs=[pl.BlockSpec((C, w), lambda b, h, c: (b * nc + c, h)),
                   pl.BlockSpec((1, hb, HEAD, HEAD), lambda b, h, c: (b, h, 0, 0))],
        out_shape=[jax.ShapeDtypeStruct((Bn * T, H * HEAD), BF16),
                   jax.ShapeDtypeStruct((Bn, H, HEAD, HEAD), F32)],
        scratch_shapes=[pltpu.VMEM((hb, HEAD, HEAD), F32)],
        compiler_params=_params("parallel", "parallel", "arbitrary"),
        name="hgrn_prompt",
    )(z, z, z, z, lb.reshape(1, H * HEAD), out_g.reshape(1, HEAD))


def _column(rowvec, n):
    eye = _iota((n, n), 0) == _iota((n, n), 1)
    return jnp.sum(jnp.where(eye, jnp.broadcast_to(rowvec, (n, n)), 0.0), axis=1, keepdims=True)


def _hgrn_sample_kernel(q_ref, f_ref, v_ref, g_ref, lb_ref, og_ref, s0_ref, y_ref, s_ref, *, H):
    for h in range(H):
        sl = slice(h * HEAD, (h + 1) * HEAD)
        q, log_f, k = _hgrn_gates(q_ref[0][:, sl], f_ref[0][:, sl], lb_ref[:, sl])
        v = v_ref[0][:, sl]
        s_new = _column(jnp.exp(log_f), HEAD) * s0_ref[0, 0, h] + _column(k, HEAD) * v
        o = jnp.sum(_column(q, HEAD) * s_new, axis=0, keepdims=True)
        y_ref[0, :, sl] = (_rms(o, og_ref[...]) * _silu(g_ref[0][:, sl])).astype(y_ref.dtype)
        s_ref[0, h] = s_new


def hgrn_sample(z3, lb, out_g, state, layer, H):
    Bs = z3.shape[0]
    w = H * HEAD
    blk = lambda off: pl.BlockSpec((1, 1, w), lambda b: (b, 0, off))
    return pl.pallas_call(
        functools.partial(_hgrn_sample_kernel, H=H),
        grid=(Bs,),
        in_specs=[blk(0), blk(1), blk(2), blk(3),
                  pl.BlockSpec((1, w), lambda b: (0, 0)),
                  pl.BlockSpec((1, HEAD), lambda b: (0, 0)),
                  pl.BlockSpec((1, 1, H, HEAD, HEAD), lambda b: (layer, b, 0, 0, 0))],
        out_specs=[pl.BlockSpec((1, 1, w), lambda b: (b, 0, 0)),
                   pl.BlockSpec((1, H, HEAD, HEAD), lambda b: (b, 0, 0, 0))],
        out_shape=[jax.ShapeDtypeStruct((Bs, 1, w), BF16),
                   jax.ShapeDtypeStruct((Bs, H, HEAD, HEAD), F32)],
        compiler_params=_params("parallel"),
        name="hgrn_sample",
    )(z3, z3, z3, z3, lb.reshape(1, w), out_g.reshape(1, HEAD), state)


def _nsa_prep_kernel(qb_ref, kv_ref, ks_ref, kw_ref, pw_ref, g_ref, qn_ref, ksn_ref, kwn_ref, *pool_refs,
                     HQ, G, pool):
    g = g_ref[...]
    for h in range(HQ):
        sl = slice(h * HEAD, (h + 1) * HEAD)
        qn_ref[:, sl] = (_rms(qb_ref[:, sl], g[0:1]) * ATTN_SCALE).astype(qn_ref.dtype)
    for j in range(G):
        sl = slice(j * HEAD, (j + 1) * HEAD)
        ksn_ref[:, sl] = _rms(ks_ref[:, sl], g[2:3])
        kwn_ref[:, sl] = _rms(kw_ref[:, sl], g[3:4])
    if pool:
        pool_ref = pool_refs[0]
        kvw = G * HEAD
        w = pw_ref[...]
        for n in range(kv_ref.shape[0] // BLOCK):
            blk = jnp.sum(kv_ref[n * BLOCK:(n + 1) * BLOCK, :] * w, axis=0, keepdims=True)
            for j in range(G):
                sl = slice(j * HEAD, (j + 1) * HEAD)
                pool_ref[n:n + 1, sl] = _rms(blk[:, sl], g[1:2])
            pool_ref[n:n + 1, kvw:] = blk[:, kvw:]


def nsa_prep(z, pool_w2, qk_g, HQ, G, col_q, pool):
    M = z.shape[0]
    qw, kvw = HQ * HEAD, G * HEAD
    tm = _tile(M, (512, 256, 128, 64, 8))
    col_kv = col_q + qw
    out_shape = [jax.ShapeDtypeStruct((M, qw), BF16 if pool else F32),
                 jax.ShapeDtypeStruct((M, kvw), F32), jax.ShapeDtypeStruct((M, kvw), F32)]
    out_specs = [pl.BlockSpec((tm, qw), lambda i: (i, 0)),
                 pl.BlockSpec((tm, kvw), lambda i: (i, 0)), pl.BlockSpec((tm, kvw), lambda i: (i, 0))]
    if pool:
        out_shape.append(jax.ShapeDtypeStruct((M // BLOCK, 2 * kvw), F32))
        out_specs.append(pl.BlockSpec((tm // BLOCK, 2 * kvw), lambda i: (i, 0)))
    return pl.pallas_call(
        functools.partial(_nsa_prep_kernel, HQ=HQ, G=G, pool=pool),
        grid=(M // tm,),
        in_specs=[pl.BlockSpec((tm, qw), lambda i: (i, col_q // qw)),
                  pl.BlockSpec((tm, 2 * kvw), lambda i: (i, col_kv // (2 * kvw))),
                  pl.BlockSpec((tm, kvw), lambda i: (i, col_kv // kvw + 2)),
                  pl.BlockSpec((tm, kvw), lambda i: (i, col_kv // kvw + 4)),
                  pl.BlockSpec((BLOCK, 2 * kvw), lambda i: (0, 0)),
                  pl.BlockSpec((4, HEAD), lambda i: (0, 0))],
        out_specs=out_specs,
        out_shape=out_shape,
        compiler_params=_params("parallel"),
        name="nsa_prep",
    )(z, z, z, z, pool_w2, qk_g)


def _gate_col(gates, c):
    return jnp.sum(jnp.where(_iota(gates.shape, 1) == c, gates, 0.0), axis=1, keepdims=True)


def _top_blocks(imp, cur, n_top):
    nb = imp.shape[1]
    j = _iota(imp.shape, 1)
    forced = (j == 0) | (j == cur) | (j == cur - 1)
    causal = j <= cur
    score = jnp.where(causal, imp + jnp.where(forced, FORCE_BONUS, 0.0), NEG)
    rank = jnp.zeros(imp.shape, I32)
    for i in range(nb):
        si = score[:, i:i + 1]
        rank = rank + ((si > score) | ((si == score) & (i < j))).astype(I32)
    return (rank < n_top) & causal


def _nsa_prompt_kernel(q_ref, pool_ref, vpool_ref, ks_ref, vs_ref, kw_ref, vw_ref, bg_ref, o_ref,
                       sbias_ref, ss_ref, se_ref, sl_ref, wbias_ref, ws_ref, we_ref, wl_ref, *, tq, T, R, kc):
    g = pl.program_id(1)
    t0 = pl.program_id(2) * tq
    rows = R * tq
    nb = T // BLOCK
    q = jnp.concatenate([q_ref[:, r * HEAD:(r + 1) * HEAD] for r in range(R)], axis=0)

    kcb = pool_ref[0].astype(BF16)
    vcb = vpool_ref[0].astype(BF16)
    sc = _dot_nt(q, kcb)
    tpos = t0 + _iota((rows, nb), 0) % tq
    vis = (_iota((rows, nb), 1) + 1) * BLOCK <= tpos + 1
    p = jnp.where(vis, _softmax_masked(sc, vis), 0.0)
    o_cmp = _dot(p.astype(BF16), vcb)
    imp = p[0:tq]
    for r in range(1, R):
        imp = imp + p[r * tq:(r + 1) * tq]

    cur = (t0 + _iota((tq, nb), 0)) // BLOCK
    sel = jnp.where(_top_blocks(imp, cur, min(TOP_N, nb)), 1.0, 0.0).astype(BF16)

    def softmax_strips(s_ref, bias_ref, e_ref, l_ref, W):
        for r in range(R):
            for i in range(tq // STRIP):
                bs = slice(i * STRIP, (i + 1) * STRIP)
                rs = slice(r * tq + i * STRIP, r * tq + (i + 1) * STRIP)
                sr = s_ref[rs, :W] + bias_ref[bs, :W]
                e = jnp.exp(sr - jnp.max(sr, axis=1, keepdims=True))
                l_ref[rs, :] = jnp.sum(e, axis=1, keepdims=True)
                e_ref[rs, :W] = e.astype(BF16)

    wlen = min(T, WINDOW + tq)
    start = pl.multiple_of(jnp.clip(t0 - WINDOW, 0, T - wlen), HEAD)
    gates = _sigmoid(bg_ref[...])

    for c in range(T // kc):
        @pl.when((t0 + tq - 1) // kc == c)
        def _():
            W = (c + 1) * kc
            expand = jnp.where(_iota((nb, W), 1) // BLOCK == _iota((nb, W), 0), 1.0, 0.0).astype(BF16)
            allowed = (_dot(sel, expand) > 0.5) & (_iota((tq, W), 1) <= t0 + _iota((tq, W), 0))
            sbias_ref[:, :W] = jnp.where(allowed, 0.0, NEG)
            diff = (t0 + _iota((tq, wlen), 0)) - (start + _iota((tq, wlen), 1))
            wbias_ref[...] = jnp.where((diff >= 0) & (diff <= WINDOW), 0.0, NEG)

            ss_ref[:, :W] = _dot_nt(q, ks_ref[:W, :].astype(BF16))
            ws_ref[...] = _dot_nt(q, kw_ref[pl.ds(start, wlen), :].astype(BF16))
            softmax_strips(ss_ref, sbias_ref, se_ref, sl_ref, W)
            softmax_strips(ws_ref, wbias_ref, we_ref, wl_ref, wlen)
            o_sel = _dot(se_ref[:, :W], vs_ref[:W, :].astype(BF16)) / sl_ref[...]
            o_win = _dot(we_ref[...], vw_ref[pl.ds(start, wlen), :].astype(BF16)) / wl_ref[...]

            for r in range(R):
                c0 = (g * R + r) * 3
                sl = slice(r * tq, (r + 1) * tq)
                y = (_gate_col(gates, c0) * o_cmp[sl] + _gate_col(gates, c0 + 1) * o_sel[sl]
                     + _gate_col(gates, c0 + 2) * o_win[sl])
                o_ref[:, r * HEAD:(r + 1) * HEAD] = y.astype(o_ref.dtype)


def nsa_prompt(qn, pooled, ksn, kwn, z, zbg, Bn, T, HQ, G, col_kv):
    R = HQ // G
    tq = _tile(T, (128, 64))
    nq = T // tq
    nb = T // BLOCK
    kvw = G * HEAD
    pooled3 = pooled.reshape(Bn, nb, 2 * kvw)
    cv = col_kv // HEAD
    kc = _tile(T, (512, 256, 128, 64))
    seq = lambda off: pl.BlockSpec((T, HEAD), lambda b, g, i: (b, off + g))

    def branch_scratch(w):
        return [pltpu.VMEM((tq, w), F32), pltpu.VMEM((R * tq, w), F32), pltpu.VMEM((R * tq, w), BF16),
                pltpu.VMEM((R * tq, 1), F32)]

    return pl.pallas_call(
        functools.partial(_nsa_prompt_kernel, tq=tq, T=T, R=R, kc=kc),
        grid=(Bn, G, nq),
        in_specs=[pl.BlockSpec((tq, R * HEAD), lambda b, g, i: (b * nq + i, g)),
                  pl.BlockSpec((1, nb, HEAD), lambda b, g, i: (b, 0, g)),
                  pl.BlockSpec((1, nb, HEAD), lambda b, g, i: (b, 0, G + g)),
                  seq(0), seq(cv + 3 * G), seq(0), seq(cv + 5 * G),
                  pl.BlockSpec((tq, HEAD), lambda b, g, i: (b * nq + i, 0))],
        out_specs=pl.BlockSpec((tq, R * HEAD), lambda b, g, i: (b * nq + i, g)),
        out_shape=jax.ShapeDtypeStruct((Bn * T, HQ * HEAD), BF16),
        scratch_shapes=branch_scratch(T) + branch_scratch(min(T, WINDOW + tq)),
        compiler_params=_params("parallel", "parallel", "arbitrary"),
        name="nsa_prompt",
    )(qn, pooled3, pooled3, ksn, z, kwn, z, zbg)


def _pool_pages_kernel(pt_ref, *refs, pp):
    ck_refs, cv_refs = refs[:pp], refs[pp:2 * pp]
    pw_ref, pk_ref, pv_ref = refs[2 * pp:]
    per = PAGE_SIZE // BLOCK
    for q in range(pp):
        for half in range(per):
            sl = slice(half * BLOCK, (half + 1) * BLOCK)
            pk_ref[0, q * per + half] = jnp.sum(ck_refs[q][0, 0, sl] * pw_ref[0], axis=0)
            pv_ref[0, q * per + half] = jnp.sum(cv_refs[q][0, 0, sl] * pw_ref[1], axis=0)


def pool_pages(cache_k, cache_v, page_table, pool_w3, layer):
    Bs, n_pages = page_table.shape
    G = cache_k.shape[3]
    per = PAGE_SIZE // BLOCK
    pp = _tile(n_pages, (16, 8, 4, 2, 1))
    page = lambda q: pl.BlockSpec((1, 1, PAGE_SIZE, G, HEAD),
                                  lambda b, p, pt: (layer, pt[b * n_pages + p * pp + q], 0, 0, 0))
    out = pl.BlockSpec((1, pp * per, G, HEAD), lambda b, p, pt: (b, p, 0, 0))
    return pl.pallas_call(
        functools.partial(_pool_pages_kernel, pp=pp),
        grid_spec=pltpu.PrefetchScalarGridSpec(
            num_scalar_prefetch=1,
            grid=(Bs, n_pages // pp),
            in_specs=[page(q) for q in range(pp)] * 2
                     + [pl.BlockSpec((2, BLOCK, G, HEAD), lambda b, p, pt: (0, 0, 0, 0))],
            out_specs=[out, out]),
        out_shape=[jax.ShapeDtypeStruct((Bs, n_pages * per, G, HEAD), F32)] * 2,
        compiler_params=_params("parallel", "parallel"),
        name="pool_pages",
    )(page_table.reshape(-1), *([cache_k] * pp), *([cache_v] * pp), pool_w3)


def _head_rows(qrow, g, R):
    rows = [qrow[:, (g * R + r) * HEAD:(g * R + r + 1) * HEAD] for r in range(R)]
    rows.append(jnp.zeros((8 - R, HEAD), F32))
    return jnp.concatenate(rows, axis=0).astype(BF16)


def _sample_cmp_win_kernel(q_ref, pk_ref, pv_ref, cwk_ref, cwv_ref, kwn_ref, vw_ref, bg_ref, g_ref,
                           pc_ref, pw_ref, idx_ref, *, G, R, past):
    nbp = pk_ref.shape[1]
    wb = cwk_ref.shape[2]
    qpos = past
    cur = qpos // BLOCK
    nlane = (cur // HEAD + 1) * HEAD
    qrow = q_ref[0]
    gates = _sigmoid(bg_ref[0])
    qk_g = g_ref[...]
    for g in range(G):
        sl = slice(g * HEAD, (g + 1) * HEAD)
        qg = _head_rows(qrow, g, R)

        kcb = _rms(pk_ref[0, :, g, :], qk_g[1:2]).astype(BF16)
        sc = _dot_nt(qg, kcb)
        vis = (_iota((8, nbp), 1) + 1) * BLOCK <= qpos + 1
        p = jnp.where(vis, _softmax_masked(sc, vis), 0.0)
        o_cmp = _dot(p.astype(BF16), pv_ref[0, :, g, :].astype(BF16))
        imp = p[0:1]
        for r in range(1, R):
            imp = imp + p[r:r + 1]

        imp = jnp.concatenate([imp, jnp.zeros((1, nlane - nbp), F32)], axis=1)
        j = _iota((1, nlane), 1)
        forced = (j == 0) | (j == cur) | (j == cur - 1)
        score = jnp.where(j <= cur, imp + jnp.where(forced, FORCE_BONUS, 0.0), NEG)
        s_col = _column(score, nlane)
        ii = _iota((nlane, nlane), 0)
        jj = _iota((nlane, nlane), 1)
        ahead = (s_col > score) | ((s_col == score) & (ii < jj))
        rank = jnp.sum(ahead.astype(F32), axis=0, keepdims=True)
        rank_col = _column(rank, nlane)
        hit = rank_col == _iota((nlane, HEAD), 1).astype(F32)
        idx = jnp.sum(jnp.where(hit, _iota((nlane, HEAD), 0).astype(F32), 0.0), axis=0, keepdims=True)
        idx_ref[0, g:g + 1, :] = idx.astype(I32)

        kwc = cwk_ref[0, 0, :, g, :].astype(BF16)
        sw = _dot_nt(qg, kwc)
        diff = qpos - (past - wb + _iota((8, wb), 1))
        sw = jnp.where((diff >= 0) & (diff <= WINDOW), sw, NEG)
        kn = kwn_ref[0][:, sl].astype(BF16).astype(F32)
        vn = vw_ref[0][:, sl].astype(BF16).astype(F32)
        sn = jnp.sum(qg.astype(F32) * kn, axis=1, keepdims=True)
        m = jnp.maximum(jnp.max(sw, axis=1, keepdims=True), sn)
        e = jnp.exp(sw - m)
        en = jnp.exp(sn - m)
        den = jnp.sum(e, axis=1, keepdims=True) + en
        o_win = (_dot((e / den).astype(BF16), cwv_ref[0, 0, :, g, :].astype(BF16))
                 + (en / den).astype(BF16).astype(F32) * vn)

        for r in range(R):
            c0 = (g * R + r) * 3
            hs = slice((g * R + r) * HEAD, (g * R + r + 1) * HEAD)
            pc_ref[0, :, hs] = gates[:, c0:c0 + 1] * o_cmp[r:r + 1]
            pw_ref[0, :, hs] = gates[:, c0 + 2:c0 + 3] * o_win[r:r + 1]


def sample_cmp_win(qn3, pk, pv, cache_win_k, cache_win_v, kwn3, z3, zbg3, qk_g, layer, G, R, past, col_vw):
    Bs = qn3.shape[0]
    depth, _, wb = cache_win_k.shape[:3]
    kvw = G * HEAD
    qw = G * R * HEAD
    nbp = pk.shape[1]
    row = lambda w, j=0: pl.BlockSpec((1, 1, w), lambda b: (b, 0, j))
    return pl.pallas_call(
        functools.partial(_sample_cmp_win_kernel, G=G, R=R, past=past),
        grid=(Bs,),
        in_specs=[row(qw),
                  pl.BlockSpec((1, nbp, G, HEAD), lambda b: (b, 0, 0, 0)),
                  pl.BlockSpec((1, nbp, G, HEAD), lambda b: (b, 0, 0, 0)),
                  pl.BlockSpec((1, 1, wb, G, HEAD), lambda b: (layer, b, 0, 0, 0)),
                  pl.BlockSpec((1, 1, wb, G, HEAD), lambda b: (layer, b, 0, 0, 0)),
                  row(kvw), row(kvw, col_vw // kvw), row(HEAD),
                  pl.BlockSpec((4, HEAD), lambda b: (0, 0))],
        out_specs=[row(qw), row(qw), pl.BlockSpec((1, G, HEAD), lambda b: (b, 0, 0))],
        out_shape=[jax.ShapeDtypeStruct((Bs, 1, qw), F32), jax.ShapeDtypeStruct((Bs, 1, qw), F32),
                   jax.ShapeDtypeStruct((Bs, G, HEAD), I32)],
        compiler_params=_params("parallel"),
        name="sample_cmp_win",
    )(qn3, pk, pv, cache_win_k, cache_win_v, kwn3, z3, zbg3, qk_g)


def _sample_select_kernel(idx_ref, pt_ref, q_ref, *refs, G, R, n_top, past):
    ck_refs, cv_refs = refs[:G], refs[G:2 * G]
    kn_ref, vn_ref, bg_ref, pc_ref, pw_ref, o_ref, m_ref, l_ref, acc_ref = refs[2 * G:]
    b, n = pl.program_id(0), pl.program_id(1)
    qpos = past

    @pl.when(n == 0)
    def _():
        m_ref[...] = jnp.full_like(m_ref, NEG)
        l_ref[...] = jnp.zeros_like(l_ref)
        acc_ref[...] = jnp.zeros_like(acc_ref)

    qrow = q_ref[0]
    for g in range(G):
        sl = slice(g * HEAD, (g + 1) * HEAD)
        j = idx_ref[(b * G + g) * n_top + n]
        qg = _head_rows(qrow, g, R)
        s = _dot_nt(qg, ck_refs[g][0, 0, :, g, :].astype(BF16))
        pos = j * BLOCK + _iota((8, BLOCK), 1)
        s = jnp.where((pos < past) & (pos <= qpos), s, NEG)
        kn = kn_ref[0][:, sl].astype(BF16).astype(F32)
        vn = vn_ref[0][:, sl].astype(BF16).astype(F32)
        sn = jnp.sum(qg.astype(F32) * kn, axis=1, keepdims=True)
        sn = jnp.where(j == past // BLOCK, sn, NEG)

        m_old = m_ref[g][:, 0:1]
        m_new = jnp.maximum(m_old, jnp.maximum(jnp.max(s, axis=1, keepdims=True), sn))
        alpha = jnp.exp(m_old - m_new)
        e = jnp.exp(s - m_new)
        en = jnp.exp(sn - m_new)
        l_new = alpha * l_ref[g][:, 0:1] + jnp.sum(e, axis=1, keepdims=True) + en
        acc = (alpha * acc_ref[g] + _dot(e.astype(BF16), cv_refs[g][0, 0, :, g, :].astype(BF16))
               + en.astype(BF16).astype(F32) * vn)
        m_ref[g] = jnp.broadcast_to(m_new, (8, HEAD))
        l_ref[g] = jnp.broadcast_to(l_new, (8, HEAD))
        acc_ref[g] = acc

    @pl.when(n == n_top - 1)
    def _():
        gates = _sigmoid(bg_ref[0])
        for g in range(G):
            o_sel = acc_ref[g] / l_ref[g][:, 0:1]
            for r in range(R):
                c = (g * R + r) * 3 + 1
                hs = slice((g * R + r) * HEAD, (g * R + r + 1) * HEAD)
                o_ref[0, :, hs] = ((pc_ref[0, :, hs] + gates[:, c:c + 1] * o_sel[r:r + 1])
                                   + pw_ref[0, :, hs]).astype(o_ref.dtype)


def sample_select(idx, page_table, qn3, cache_k, cache_v, ksn3, z3, zbg3, pc, pw, layer, G, R, past, col_vs):
    Bs, n_pages = page_table.shape
    n_top = idx.shape[-1]
    depth, n_pool = cache_k.shape[:2]
    kvw, qw = G * HEAD, G * R * HEAD
    per = PAGE_SIZE // BLOCK
    ck = cache_k.reshape(depth, n_pool * per, BLOCK, G, HEAD)
    cv = cache_v.reshape(depth, n_pool * per, BLOCK, G, HEAD)
    nbp = past // BLOCK

    def blk(g):
        def index(b, n, idx_ref, pt_ref):
            jc = jnp.minimum(idx_ref[(b * G + g) * n_top + n], nbp - 1)
            return (layer, pt_ref[b * n_pages + jc // per] * per + jc % per, 0, 0, 0)
        return pl.BlockSpec((1, 1, BLOCK, G, HEAD), index)

    row = lambda w, j=0: pl.BlockSpec((1, 1, w), lambda b, n, i_, p_: (b, 0, j))
    return pl.pallas_call(
        functools.partial(_sample_select_kernel, G=G, R=R, n_top=n_top, past=past),
        grid_spec=pltpu.PrefetchScalarGridSpec(
            num_scalar_prefetch=2,
            grid=(Bs, n_top),
            in_specs=[row(qw)] + [blk(g) for g in range(G)] * 2
                     + [row(kvw), row(kvw, col_vs // kvw), row(HEAD), row(qw), row(qw)],
            out_specs=row(qw),
            scratch_shapes=[pltpu.VMEM((G, 8, HEAD), F32)] * 3),
        out_shape=jax.ShapeDtypeStruct((Bs, 1, qw), BF16),
        compiler_params=_params("parallel", "arbitrary"),
        name="sample_select",
    )(idx.reshape(-1), page_table.reshape(-1), qn3, *([ck] * G), *([cv] * G), ksn3, z3, zbg3, pc, pw)


def _cast_kernel(x_ref, o_ref):
    o_ref[...] = x_ref[0].astype(o_ref.dtype)


def cast_layer_bf16(w, layer):
    _, R, C = w.shape
    tr = next((t for t in (512, 256, 128, 64, 32, 16) if R % t == 0 and t * C * 4 <= CAST_BLOCK_BYTES), R)
    return pl.pallas_call(
        _cast_kernel,
        grid=(R // tr,),
        in_specs=[pl.BlockSpec((1, tr, C), lambda i: (layer, i, 0))],
        out_specs=pl.BlockSpec((tr, C), lambda i: (i, 0)),
        out_shape=jax.ShapeDtypeStruct((R, C), BF16),
        compiler_params=_params("parallel"),
        name="cast_layer_bf16",
    )(w)


def _cast_cut_kernel(a_ref, b_ref, o_ref, *, nj0, cut):
    j = pl.program_id(0)
    tn = o_ref.shape[1]

    @pl.when(j < nj0)
    def _():
        o_ref[...] = a_ref[0].T.astype(o_ref.dtype)

    @pl.when(j >= nj0)
    def _():
        x = jnp.concatenate([a_ref[0], b_ref[0]], axis=0)
        o_ref[...] = x[cut:cut + tn].T.astype(o_ref.dtype)


def cast_cut_bf16(w_t, layer, c0, cut):
    _, C, R = w_t.shape
    N = C - cut
    tn = next(t for t in (256, 128) if c0 % t == 0 and N % t == 0)
    assert cut < HEAD and cut % 8 == 0
    return pl.pallas_call(
        functools.partial(_cast_cut_kernel, nj0=c0 // tn, cut=cut),
        grid=(N // tn,),
        in_specs=[pl.BlockSpec((1, tn, R), lambda j: (layer, j, 0)),
                  pl.BlockSpec((1, HEAD, R), lambda j: (layer, (j + 1) * (tn // HEAD), 0))],
        out_specs=pl.BlockSpec((R, tn), lambda j: (0, j)),
        out_shape=jax.ShapeDtypeStruct((R, N), BF16),
        compiler_params=_params("parallel"),
        name="cast_cut_bf16",
    )(w_t, w_t)


def _cast_t_kernel(x_ref, o_ref):
    o_ref[...] = x_ref[0].T.astype(o_ref.dtype)


def cast_cols_bf16(w_t, layer, col_block):
    R = w_t.shape[2]
    return pl.pallas_call(
        _cast_t_kernel,
        grid=(1,),
        in_specs=[pl.BlockSpec((1, HEAD, R), lambda i: (layer, col_block, 0))],
        out_specs=pl.BlockSpec((R, HEAD), lambda i: (0, 0)),
        out_shape=jax.ShapeDtypeStruct((R, HEAD), BF16),
        compiler_params=_params("arbitrary"),
        name="cast_cols_bf16",
    )(w_t)


def _layer_weights(l, norm_g, w_in, qk_g, cmp_pool, w_pa, w_pb, w_o, f1_gu, f1_dn, f2_gu, f2_dn, H, HQ, G):
    kw, qw, kvw = H * HEAD, HQ * HEAD, G * HEAD
    D = w_in.shape[1]
    c_bg = 4 * kw + qw + 6 * kvw
    n_bg = 3 * HQ
    w_in_t = jnp.transpose(w_in, (0, 2, 1))
    w_main = cast_cut_bf16(w_in_t, l, c_bg, n_bg)
    w_bg = cast_cols_bf16(w_in_t, l, c_bg // HEAD)
    pool_w2 = jnp.repeat(jnp.transpose(cmp_pool[l], (1, 0, 2)).reshape(BLOCK, 2 * G), HEAD, axis=1)
    pool_w3 = jnp.broadcast_to(cmp_pool[l][..., None], cmp_pool.shape[1:] + (HEAD,))
    cast = lambda w: cast_layer_bf16(w, l)
    return dict(norm_g=norm_g[l], w_main=w_main, w_bg=w_bg, qk_g=qk_g[l], pool_w2=pool_w2, pool_w3=pool_w3,
                w_pa=cast(w_pa), w_pb=cast(w_pb), w_o=cast(w_o),
                ffn1=(f1_gu, cast(f1_dn)), ffn2=(f2_gu, cast(f2_dn)), c_bg=c_bg, D=D, l=l)


def _ffn(xp, xs, g, ffn_w, l):
    gu, dn = ffn_w
    hp, hs = matmul_swiglu(rmsnorm_rows(xp, g), rmsnorm_rows(xs, g), gu, l)
    return matmul(hp, dn, res=xp, scale=0.5), matmul(hs, dn, res=xs, scale=0.5)


def _layer(xp, xs, lw, prompt_mixers, sample_mixers):
    hp, hs = _ffn(xp, xs, lw["norm_g"][0], lw["ffn1"], lw["l"])
    out, states = [], []
    for h, mixers in ((hp, prompt_mixers), (hs, sample_mixers)):
        u = rmsnorm_rows(h, lw["norm_g"][1])
        z = matmul(u, lw["w_main"])
        zbg = matmul(u, lw["w_bg"])
        ya, yb, state = mixers(z, zbg)
        m = merge_project(ya, yb, lw["w_pa"], lw["w_pb"], z, lw["c_bg"], lw["c_bg"] + lw["D"])
        out.append(matmul(m, lw["w_o"], res=h, scale=1.0))
        states.append(state)
    yp, ys = _ffn(out[0], out[1], lw["norm_g"][2], lw["ffn2"], lw["l"])
    return yp, ys, states[0], states[1]


def kernel(x_prompt, x_sample, state_hgrn, cache_cmp_k, cache_cmp_v, cache_sel_k, cache_sel_v, cache_win_k, cache_win_v, page_table, norm_g, w_in, hgrn_lb, hgrn_out_g, qk_g, cmp_pool, w_proj_a, w_proj_b, w_out, ffn1_gu, ffn1_down, ffn2_gu, ffn2_down):
    Bn, T, D = x_prompt.shape
    Bs, Tn, _ = x_sample.shape
    depth = w_in.shape[0]
    H = hgrn_lb.shape[1] // HEAD
    HQ = w_proj_b.shape[1] // HEAD
    G = cache_cmp_k.shape[3]
    R = HQ // G
    kw, qw, kvw = H * HEAD, HQ * HEAD, G * HEAD
    past = page_table.shape[1] * PAGE_SIZE
    wb = cache_win_k.shape[2]
    assert Tn == 1 and T % HGRN_CHUNK == 0 and past % PAGE_SIZE == 0 and R <= 8
    assert hgrn_out_g.shape[1] == HEAD and w_proj_a.shape[1] == kw and wb <= WINDOW and wb <= past
    col_q = 4 * kw
    col_kv = col_q + qw

    p = jax.nn.softmax(hgrn_lb.astype(F32), axis=0)
    lbs = jnp.cumsum(p, axis=0) - p[0]

    y_p = x_prompt.reshape(Bn * T, D)
    y_s = x_sample.reshape(Bs, D)
    st_p, st_s = [], []
    for l in range(depth):
        lw = _layer_weights(l, norm_g, w_in, qk_g, cmp_pool, w_proj_a, w_proj_b, w_out,
                            ffn1_gu, ffn1_down, ffn2_gu, ffn2_down, H, HQ, G)

        def prompt_mixers(z, zbg):
            ya, s_new = hgrn_prompt(z, lbs[l], hgrn_out_g[l], Bn, T, H)
            qn, ksn, kwn, pooled = nsa_prep(z, lw["pool_w2"], lw["qk_g"], HQ, G, col_q, True)
            yb = nsa_prompt(qn, pooled, ksn, kwn, z, zbg, Bn, T, HQ, G, col_kv)
            kv = lambda i: z[:, col_kv + i * kvw:col_kv + (i + 1) * kvw].reshape(Bn, T, G, HEAD)
            tail = lambda a: a[:, T - min(WINDOW, T):]
            return ya, yb, (s_new, kv(0), kv(1), ksn.reshape(Bn, T, G, HEAD), kv(3),
                            tail(kwn.reshape(Bn, T, G, HEAD)), tail(kv(5)))

        def sample_mixers(z, zbg):
            z3, zbg3 = z.reshape(Bs, 1, -1), zbg.reshape(Bs, 1, HEAD)
            ya, s_new = hgrn_sample(z3, lbs[l], hgrn_out_g[l], state_hgrn, l, H)
            qn, ksn, kwn = nsa_prep(z, lw["pool_w2"], lw["qk_g"], HQ, G, col_q, False)
            qn3, ksn3, kwn3 = qn.reshape(Bs, 1, qw), ksn.reshape(Bs, 1, kvw), kwn.reshape(Bs, 1, kvw)
            pk, pv = pool_pages(cache_cmp_k, cache_cmp_v, page_table, lw["pool_w3"], l)
            pc, pw, idx = sample_cmp_win(qn3, pk, pv, cache_win_k, cache_win_v, kwn3, z3, zbg3, lw["qk_g"],
                                         l, G, R, past, col_kv + 5 * kvw)
            n_top = min(TOP_N, -(-(past + Tn) // BLOCK))
            yb = sample_select(idx[:, :, :n_top], page_table, qn3, cache_sel_k, cache_sel_v, ksn3, z3, zbg3,
                               pc, pw, l, G, R, past, col_kv + 3 * kvw)
            kv = lambda i: z[:, col_kv + i * kvw:col_kv + (i + 1) * kvw].reshape(Bs, 1, G, HEAD)
            win_k = jnp.concatenate([cache_win_k[l], kwn.reshape(Bs, 1, G, HEAD)], axis=1)[:, -wb:]
            win_v = jnp.concatenate([cache_win_v[l], kv(5)], axis=1)[:, -wb:]
            return ya.reshape(Bs, kw), yb.reshape(Bs, qw), (
                s_new, kv(0), kv(1), ksn.reshape(Bs, 1, G, HEAD), kv(3), win_k, win_v)

        y_p, y_s, sp, ss = _layer(y_p, y_s, lw, prompt_mixers, sample_mixers)
        st_p.append(sp)
        st_s.append(ss)

    stack = lambda states, i: jnp.stack([s[i] for s in states])
    return ((y_p.reshape(Bn, T, D), y_s.reshape(Bs, Tn, D))
            + tuple(stack(st_p, i) for i in range(7)) + tuple(stack(st_s, i) for i in range(7)))
```
